```python
import math
import jax, jax.numpy as jnp
from jax import lax
import numpy as np

D_MODEL = 2048
BATCH = 2
SEQ = 4096
DEPTH = 2
DEC_BATCH = 128
DEC_SEQ = 1
PAST_LEN = 2048
PAGE_SIZE = 128

HEAD_DIM = 128
NSA_HEADS = 8
NSA_KV_HEADS = 2
NSA_GROUP = NSA_HEADS // NSA_KV_HEADS
CMP_BLOCK = 64
N_SELECT = 8
WINDOW = 512
Q_BLOCK = 128
SEL_FORCE = 1.0e4
GLA_HEADS = 4
GLA_DK = 64
GLA_DV = 128
GLA_GATE_RANK = 16
GLA_GATE_NORM = 16.0
GLA_CHUNK = 64
GDN_HEADS = 4
GDN_DK = 128
GDN_DV = 128
GDN_CHUNK = 64
CONV_W = 4
N_BUCKETS = 32
T5_MAX_DIST = 128
D_FF = 5632
PLE_DIM = 256
N_NORMS = 8
RMS_EPS = 1e-6
NEG_INF = -1e30
NSA_WIDTH = NSA_HEADS * HEAD_DIM
GLA_WIDTH = GLA_HEADS * GLA_DV
GDN_WIDTH = GDN_HEADS * GDN_DV
MIX_WIDTH = NSA_WIDTH + GLA_WIDTH + GDN_WIDTH
IN_SIZES = (NSA_WIDTH, 6 * NSA_KV_HEADS * HEAD_DIM, 3 * NSA_HEADS, GLA_HEADS * GLA_DK, GLA_HEADS * GLA_DK,
            GLA_WIDTH, GLA_GATE_RANK, GLA_WIDTH, 3 * GDN_WIDTH, GDN_WIDTH, GDN_HEADS, GDN_HEADS)
IN_DIM = sum(IN_SIZES)

kernel_name = 'hybrid_nsa_gla_gdn_decode_step'


def rms_norm(x, w):
    xf = x.astype(jnp.float32)
    y = xf * lax.rsqrt(jnp.mean(xf * xf, axis=-1, keepdims=True) + RMS_EPS)
    return (y * w.astype(jnp.float32)).astype(x.dtype)


def l2_norm(x):
    xf = x.astype(jnp.float32)
    return (xf * lax.rsqrt(jnp.sum(xf * xf, axis=-1, keepdims=True) + 1e-6)).astype(x.dtype)


def swiglu(x, w1, w3, w2):
    return (jax.nn.silu(x @ w1) * (x @ w3)) @ w2


def masked_softmax(s, mask):
    s = jnp.where(mask, s, NEG_INF)
    p = jnp.where(mask, jnp.exp(s - jnp.max(s, axis=-1, keepdims=True)), 0.0)
    return p / jnp.maximum(jnp.sum(p, axis=-1, keepdims=True), 1e-30)


def t5_bucket(dist):
    n = jnp.maximum(dist, 0)
    exact = N_BUCKETS // 2
    large = exact + (jnp.log(jnp.maximum(n, 1).astype(jnp.float32) / exact)
                     / math.log(T5_MAX_DIST / exact) * (N_BUCKETS - exact)).astype(jnp.int32)
    return jnp.where(n < exact, n, jnp.minimum(large, N_BUCKETS - 1))


def split_projection(h, w_in_l):
    offs = [int(o) for o in np.cumsum(IN_SIZES)[:-1]]
    return jnp.split(h @ w_in_l, offs, axis=-1)


def compress_blocks(blocks, w_cmp):
    return jnp.einsum('bnlcgd,lc->bncgd', blocks, w_cmp.astype(blocks.dtype))


def nsa_heads(nq, nkv, ngate):
    B, L, _ = nq.shape
    q = (nq * HEAD_DIM ** -0.5).reshape(B, L, NSA_KV_HEADS, NSA_GROUP, HEAD_DIM)
    kv = nkv.reshape(B, L, 3, 2, NSA_KV_HEADS, HEAD_DIM)
    gates = jax.nn.sigmoid(ngate.reshape(B, L, NSA_KV_HEADS, NSA_GROUP, 3))
    return q, kv, gates


def nsa_core(q, qpos, cmp_kv, gather_sel, win_kv, win_pos, gates, t5_bias):
    f32 = jnp.float32
    B, Lq, G, Hg, _ = q.shape
    nb = cmp_kv.shape[1]
    bias_g = t5_bias.reshape(N_BUCKETS, G, Hg)
    blk = jnp.arange(nb)
    d_c = qpos[:, None] - (blk * CMP_BLOCK + CMP_BLOCK - 1)[None, :]
    s_c = jnp.einsum('bqghd,bngd->bqghn', q, cmp_kv[:, :, 0], preferred_element_type=f32)
    s_c = s_c + bias_g[t5_bucket(d_c)].transpose(0, 2, 3, 1).astype(f32)
    p_c = masked_softmax(s_c, (d_c >= 0)[None, :, None, None, :])
    o_c = jnp.einsum('bqghn,bngd->bqghd', p_c.astype(q.dtype), cmp_kv[:, :, 1])
    n_sel = min(N_SELECT, nb)
    cur = qpos // CMP_BLOCK
    forced = (blk[None] == cur[:, None]) | (blk[None] == cur[:, None] - 1) | (blk[None] == 0)
    started = blk[None] <= cur[:, None]
    score = jnp.where(started[None, :, None, :],
                      jnp.where(forced[None, :, None, :], SEL_FORCE, p_c.sum(axis=3)), -1.0)
    top_val, top_idx = lax.top_k(score, n_sel)
    kv_sel = gather_sel(top_idx)
    k_sel = kv_sel[..., 0, :].reshape(B, Lq, G, n_sel * CMP_BLOCK, HEAD_DIM)
    v_sel = kv_sel[..., 1, :].reshape(B, Lq, G, n_sel * CMP_BLOCK, HEAD_DIM)
    pos_s = (top_idx[..., None] * CMP_BLOCK + jnp.arange(CMP_BLOCK)).reshape(B, Lq, G, n_sel * CMP_BLOCK)
    d_s = qpos[None, :, None, None] - pos_s
    mask_s = jnp.repeat(top_val >= 0, CMP_BLOCK, axis=-1) & (d_s >= 0)
    gi = jnp.arange(G)[None, None, :, None]
    s_s = jnp.einsum('bqghd,bqgkd->bqghk', q, k_sel, preferred_element_type=f32)
    s_s = s_s + bias_g[t5_bucket(d_s), gi].transpose(0, 1, 2, 4, 3).astype(f32)
    p_s = masked_softmax(s_s, mask_s[:, :, :, None, :])
    o_s = jnp.einsum('bqghk,bqgkd->bqghd', p_s.astype(q.dtype), v_sel)
    d_w = qpos[:, None] - win_pos[None, :]
    mask_w = (d_w >= 0) & (d_w < WINDOW) & (win_pos[None, :] >= 0)
    s_w = jnp.einsum('bqghd,bkgd->bqghk', q, win_kv[:, :, 0], preferred_element_type=f32)
    s_w = s_w + bias_g[t5_bucket(d_w)].transpose(0, 2, 3, 1).astype(f32)
    p_w = masked_softmax(s_w, mask_w[None, :, None, None, :])
    o_w = jnp.einsum('bqghk,bkgd->bqghd', p_w.astype(q.dtype), win_kv[:, :, 1])
    return o_c * gates[..., 0:1] + o_s * gates[..., 1:2] + o_w * gates[..., 2:3]


def nsa_prompt(q, kv_c, kv_s, kv_w, gates, w_cmp, t5_bias):
    B, S, G = q.shape[:3]
    nb = S // CMP_BLOCK
    cmp_kv = compress_blocks(kv_c.reshape(B, nb, CMP_BLOCK, 2, G, HEAD_DIM), w_cmp)
    sel_blk = kv_s.reshape(B, nb, CMP_BLOCK, 2, G, HEAD_DIM)
    bi = jnp.arange(B)[:, None, None, None]
    gi = jnp.arange(G)[None, None, :, None]
    gather_sel = lambda idx: sel_blk[bi, idx, :, :, gi]
    win_pad = jnp.pad(kv_w, ((0, 0), (WINDOW, 0), (0, 0), (0, 0), (0, 0)))

    def one_block(j):
        s0 = j * Q_BLOCK
        qb = lax.dynamic_slice_in_dim(q, s0, Q_BLOCK, axis=1)
        gb = lax.dynamic_slice_in_dim(gates, s0, Q_BLOCK, axis=1)
        wb = lax.dynamic_slice_in_dim(win_pad, s0, WINDOW + Q_BLOCK, axis=1)
        qpos = s0 + jnp.arange(Q_BLOCK)
        wpos = s0 - WINDOW + jnp.arange(WINDOW + Q_BLOCK)
        return nsa_core(qb, qpos, cmp_kv, gather_sel, wb, wpos, gb, t5_bias)

    out = lax.map(one_block, jnp.arange(S // Q_BLOCK))
    return out.transpose(1, 0, 2, 3, 4, 5).reshape(B, S, NSA_WIDTH)


def nsa_sample(q, kv_c, kv_s, kv_w, gates, pool_cmp, pool_sel, win_buf, page_table, w_cmp, t5_bias):
    B, L, G = q.shape[:3]
    past = page_table.shape[1] * PAGE_SIZE
    nb_past = past // CMP_BLOCK
    nb_new = -(-L // CMP_BLOCK)
    pad = nb_new * CMP_BLOCK - L
    to_blocks = lambda t: jnp.pad(t, ((0, 0), (0, pad), (0, 0), (0, 0), (0, 0))).reshape(
        B, nb_new, CMP_BLOCK, 2, G, HEAD_DIM)
    past_c = pool_cmp[page_table].astype(q.dtype).reshape(B, nb_past, CMP_BLOCK, 2, G, HEAD_DIM)
    cmp_kv = jnp.concatenate([compress_blocks(past_c, w_cmp), compress_blocks(to_blocks(kv_c), w_cmp)], axis=1)
    bpp = PAGE_SIZE // CMP_BLOCK
    pool_blk = pool_sel.reshape(pool_sel.shape[0], bpp, CMP_BLOCK, 2, G, HEAD_DIM)
    new_blk = to_blocks(kv_s)
    bi = jnp.arange(B)[:, None, None, None]
    gi = jnp.arange(G)[None, None, :, None]

    def gather_sel(idx):
        ip = jnp.clip(idx, 0, nb_past - 1)
        phys = page_table[bi, ip // bpp]
        rows_past = pool_blk[phys, ip % bpp, :, :, gi].astype(q.dtype)
        rows_new = new_blk[bi, jnp.clip(idx - nb_past, 0, nb_new - 1), :, :, gi]
        return jnp.where((idx < nb_past)[..., None, None, None], rows_past, rows_new)

    win_all = jnp.concatenate([win_buf.astype(kv_w.dtype), kv_w], axis=1)
    wb = win_buf.shape[1]
    qpos = past + jnp.arange(L)
    wpos = past - wb + jnp.arange(wb + L)
    o = nsa_core(q, qpos, cmp_kv, gather_sel, win_all, wpos, gates, t5_bias)
    return o.reshape(B, L, NSA_WIDTH), win_all[:, L:]


def to_chunks(t, c):
    B, L = t.shape[:2]
    t = t.reshape((B, L // c, c) + t.shape[2:])
    return t.transpose((1, 0, 3, 2) + tuple(range(4, t.ndim)))


def from_chunks(t):
    n, B, H, c, d = t.shape
    return t.transpose(1, 0, 3, 2, 4).reshape(B, n * c, H, d)


def gla_heads(gq, gk, gv, glr, w_gk2, b_gk):
    B, L, _ = gq.shape
    q = (gq * GLA_DK ** -0.5).reshape(B, L, GLA_HEADS, GLA_DK)
    k = gk.reshape(B, L, GLA_HEADS, GLA_DK)
    v = gv.reshape(B, L, GLA_HEADS, GLA_DV)
    log_a = jax.nn.log_sigmoid((glr @ w_gk2 + b_gk).astype(jnp.float32)) / GLA_GATE_NORM
    return q, k, v, log_a.reshape(B, L, GLA_HEADS, GLA_DK)


def gla_chunked(q, k, v, log_a, s0):
    c = min(GLA_CHUNK, q.shape[1])
    xs = tuple(to_chunks(t.astype(jnp.float32), c) for t in (q, k, v, log_a))
    causal = jnp.tril(jnp.ones((c, c), dtype=bool))

    def step(s, inp):
        qi, ki, vi, ai = inp
        b = jnp.cumsum(ai, axis=-2)
        b_last = b[..., -1:, :]
        q_e = qi * jnp.exp(b)
        att = jnp.where(causal, jnp.einsum('bhtd,bhsd->bhts', q_e, ki * jnp.exp(-b)), 0.0)
        o = jnp.einsum('bhtd,bhdv->bhtv', q_e, s) + jnp.einsum('bhts,bhsv->bhtv', att, vi)
        s = jnp.exp(b_last[..., 0, :])[..., None] * s + jnp.einsum('bhsd,bhsv->bhdv', ki * jnp.exp(b_last - b), vi)
        return s, o

    s, o = lax.scan(step, s0.astype(jnp.float32), xs)
    return from_chunks(o), s


def gla_recurrent(q, k, v, log_a, s0):
    def step(s, inp):
        qt, kt, vt, at = inp
        s = jnp.exp(at)[..., None] * s + kt[..., :, None] * vt[..., None, :]
        return s, jnp.einsum('bhd,bhdv->bhv', qt, s)

    xs = tuple(jnp.swapaxes(t.astype(jnp.float32), 0, 1) for t in (q, k, v, log_a))
    s, o = lax.scan(step, s0.astype(jnp.float32), xs)
    return jnp.swapaxes(o, 0, 1), s


def gla_out(o, gg, nw):
    B, L = o.shape[:2]
    return (rms_norm(o, nw).reshape(B, L, GLA_WIDTH) * jax.nn.silu(gg.astype(jnp.float32))).astype(gg.dtype)


def short_conv(x_ext, w):
    L = x_ext.shape[1] - (CONV_W - 1)
    return jax.nn.silu(sum(x_ext[:, j:j + L] * w[j] for j in range(CONV_W)))


def gdn_heads(c, db, da, a_log, dt_bias):
    B, L, _ = c.shape
    c = c.reshape(B, L, 3, GDN_HEADS, GDN_DK)
    q = l2_norm(c[:, :, 0]) * GDN_DK ** -0.5
    k = l2_norm(c[:, :, 1])
    v = c[:, :, 2]
    beta = jax.nn.sigmoid(db.astype(jnp.float32))
    g = -jnp.exp(a_log.astype(jnp.float32)) * jax.nn.softplus(da.astype(jnp.float32) + dt_bias.astype(jnp.float32))
    return q, k, v, beta, g


def gdn_chunked(q, k, v, beta, g, s0):
    c = min(GDN_CHUNK, q.shape[1])
    qc, kc, vc = (to_chunks(t.astype(jnp.float32), c) for t in (q, k, v))
    bc, gc = to_chunks(beta, c), to_chunks(g, c)
    idx = jnp.arange(c)
    lower = idx[:, None] >= idx[None, :]
    strict = idx[:, None] > idx[None, :]
    eye = jnp.eye(c, dtype=jnp.float32)

    def step(s, inp):
        qi, ki, vi, bi, gi = inp
        gam = jnp.cumsum(gi, axis=-1)
        decay = jnp.exp(jnp.where(lower, gam[..., :, None] - gam[..., None, :], -jnp.inf))
        kb = ki * bi[..., None]
        m = jnp.where(strict, jnp.einsum('bhid,bhjd->bhij', kb, ki) * decay, 0.0)
        rhs = jnp.concatenate([vi * bi[..., None], kb * jnp.exp(gam)[..., None]], axis=-1)
        sol = lax.linalg.triangular_solve(m + eye, rhs, left_side=True, lower=True, unit_diagonal=True)
        u, w = sol[..., :GDN_DV], sol[..., GDN_DV:]
        v_new = u - jnp.einsum('bhcd,bhdv->bhcv', w, s)
        att = jnp.einsum('bhid,bhjd->bhij', qi, ki) * decay
        o = jnp.einsum('bhcd,bhdv->bhcv', qi * jnp.exp(gam)[..., None], s) + jnp.einsum('bhij,bhjv->bhiv', att, v_new)
        g_last = gam[..., -1:]
        s = jnp.exp(g_last)[..., None] * s + jnp.einsum('bhcd,bhcv->bhdv', ki * jnp.exp(g_last - gam)[..., None], v_new)
        return s, o

    s, o = lax.scan(step, s0.astype(jnp.float32), (qc, kc, vc, bc, gc))
    return from_chunks(o), s


def gdn_recurrent(q, k, v, beta, g, s0):
    def step(s, inp):
        qt, kt, vt, bt, gt = inp
        s = jnp.exp(gt)[..., None, None] * s
        delta = (vt - jnp.einsum('bhd,bhdv->bhv', kt, s)) * bt[..., None]
        s = s + kt[..., :, None] * delta[..., None, :]
        return s, jnp.einsum('bhd,bhdv->bhv', qt, s)

    xs = tuple(jnp.swapaxes(t.astype(jnp.float32), 0, 1) for t in (q, k, v, beta, g))
    s, o = lax.scan(step, s0.astype(jnp.float32), xs)
    return jnp.swapaxes(o, 0, 1), s


def gdn_out(o, dz, nw):
    B, L = o.shape[:2]
    return (rms_norm(o, nw).reshape(B, L, GDN_WIDTH) * jax.nn.silu(dz.astype(jnp.float32))).astype(dz.dtype)


def mixer_prompt(h, w_in_l, w_cmp_l, t5_bias, w_gk2_l, b_gk_l, gla_nw_l, conv_w_l, a_log_l, dt_bias_l, gdn_nw_l):
    B, S, _ = h.shape
    nq, nkv, ngate, gq, gk, gv, glr, gg, dqkv, dz, db, da = split_projection(h, w_in_l)
    q, kv, gates = nsa_heads(nq, nkv, ngate)
    kv_c, kv_s, kv_w = kv[:, :, 0], kv[:, :, 1], kv[:, :, 2]
    o_nsa = nsa_prompt(q, kv_c, kv_s, kv_w, gates, w_cmp_l, t5_bias)
    lq, lk, lv, la = gla_heads(gq, gk, gv, glr, w_gk2_l, b_gk_l)
    o_gla, s_gla = gla_chunked(lq, lk, lv, la, jnp.zeros((B, GLA_HEADS, GLA_DK, GLA_DV), jnp.float32))
    x_ext = jnp.pad(dqkv, ((0, 0), (CONV_W - 1, 0), (0, 0)))
    dq, dk, dv, beta, g = gdn_heads(short_conv(x_ext, conv_w_l), db, da, a_log_l, dt_bias_l)
    o_gdn, s_gdn = gdn_chunked(dq, dk, dv, beta, g, jnp.zeros((B, GDN_HEADS, GDN_DK, GDN_DV), jnp.float32))
    y = jnp.concatenate([o_nsa.astype(h.dtype), gla_out(o_gla, gg, gla_nw_l), gdn_out(o_gdn, dz, gdn_nw_l)], axis=-1)
    win = kv_w[:, S - min(WINDOW, S):]
    return y, (kv_c, kv_s, win, s_gla.astype(h.dtype), s_gdn.astype(h.dtype), x_ext[:, -(CONV_W - 1):])


def mixer_sample(h, pool_cmp, pool_sel, win_buf, s_gla0, s_gdn0, conv_buf, page_table,
                 w_in_l, w_cmp_l, t5_bias, w_gk2_l, b_gk_l, gla_nw_l, conv_w_l, a_log_l, dt_bias_l, gdn_nw_l):
    nq, nkv, ngate, gq, gk, gv, glr, gg, dqkv, dz, db, da = split_projection(h, w_in_l)
    q, kv, gates = nsa_heads(nq, nkv, ngate)
    kv_c, kv_s, kv_w = kv[:, :, 0], kv[:, :, 1], kv[:, :, 2]
    o_nsa, win_new = nsa_sample(q, kv_c, kv_s, kv_w, gates, pool_cmp, pool_sel, win_buf, page_table, w_cmp_l, t5_bias)
    lq, lk, lv, la = gla_heads(gq, gk, gv, glr, w_gk2_l, b_gk_l)
    o_gla, s_gla = gla_recurrent(lq, lk, lv, la, s_gla0)
    x_ext = jnp.concatenate([conv_buf.astype(dqkv.dtype), dqkv], axis=1)
    dq, dk, dv, beta, g = gdn_heads(short_conv(x_ext, conv_w_l), db, da, a_log_l, dt_bias_l)
    o_gdn, s_gdn = gdn_recurrent(dq, dk, dv, beta, g, s_gdn0)
    y = jnp.concatenate([o_nsa.astype(h.dtype), gla_out(o_gla, gg, gla_nw_l), gdn_out(o_gdn, dz, gdn_nw_l)], axis=-1)
    return y, (kv_c, kv_s, win_new, s_gla.astype(s_gla0.dtype), s_gdn.astype(s_gdn0.dtype), x_ext[:, -(CONV_W - 1):])


def layer_forward(x, p, mixer, nw, w1, w3, w2, w_out_l, ple_proj, ple_gate):
    x = x + 0.5 * rms_norm(swiglu(rms_norm(x, nw[0]), w1[0], w3[0], w2[0]), nw[1])
    y, st = mixer(rms_norm(x, nw[2]))
    x = x + rms_norm(y @ w_out_l, nw[3])
    x = x + 0.5 * rms_norm(swiglu(rms_norm(x, nw[4]), w1[1], w3[1], w2[1]), nw[5])
    gate = jax.nn.sigmoid(rms_norm(x, nw[6]) @ ple_gate)
    x = x + rms_norm(gate * (p @ ple_proj), nw[7])
    return x, st


def stack_layers(states, j):
    return jnp.stack([s[j] for s in states])


def setup_inputs(seed: int = 0) -> dict:
    key = jax.random.key(seed)
    ks = jax.random.split(key, 28)
    f32 = jnp.float32
    n_pages = PAST_LEN // PAGE_SIZE
    n_phys = (DEC_BATCH * n_pages * 5) // 4
    win_len = min(WINDOW, PAST_LEN)
    nrm = lambda k, shape, scale: scale * jax.random.normal(k, shape, f32)
    page_table = jax.random.permutation(ks[0], n_phys)[: DEC_BATCH * n_pages].reshape(DEC_BATCH, n_pages).astype(jnp.int32)
    dt = jnp.exp(jax.random.uniform(ks[24], (DEPTH, GDN_HEADS), f32, math.log(1e-3), math.log(1e-1)))
    return {
        'x_prompt': nrm(ks[1], (BATCH, SEQ, D_MODEL), 1.0),
        'x_sample': nrm(ks[2], (DEC_BATCH, DEC_SEQ, D_MODEL), 1.0),
        'cache_cmp': nrm(ks[3], (DEPTH, n_phys, PAGE_SIZE, 2, NSA_KV_HEADS, HEAD_DIM), 1.0),
        'cache_sel': nrm(ks[4], (DEPTH, n_phys, PAGE_SIZE, 2, NSA_KV_HEADS, HEAD_DIM), 1.0),
        'cache_win': nrm(ks[5], (DEPTH, DEC_BATCH, win_len, 2, NSA_KV_HEADS, HEAD_DIM), 1.0),
        'state_gla': nrm(ks[6], (DEPTH, DEC_BATCH, GLA_HEADS, GLA_DK, GLA_DV), 0.1),
        'state_gdn': nrm(ks[7], (DEPTH, DEC_BATCH, GDN_HEADS, GDN_DK, GDN_DV), 0.1),
        'state_conv': nrm(ks[8], (DEPTH, DEC_BATCH, CONV_W - 1, 3 * GDN_WIDTH), 1.0),
        'page_table': page_table,
        'p_prompt': nrm(ks[9], (DEPTH, BATCH, SEQ, PLE_DIM), 1.0),
        'p_sample': nrm(ks[10], (DEPTH, DEC_BATCH, DEC_SEQ, PLE_DIM), 1.0),
        'norm_w': 1.0 + nrm(ks[11], (DEPTH, N_NORMS, D_MODEL), 0.05),
        'ffn_w1': nrm(ks[12], (DEPTH, 2, D_MODEL, D_FF), D_MODEL ** -0.5),
        'ffn_w3': nrm(ks[13], (DEPTH, 2, D_MODEL, D_FF), D_MODEL ** -0.5),
        'ffn_w2': nrm(ks[14], (DEPTH, 2, D_FF, D_MODEL), D_FF ** -0.5),
        'w_in': nrm(ks[15], (DEPTH, D_MODEL, IN_DIM), D_MODEL ** -0.5),
        'w_out': nrm(ks[16], (DEPTH, MIX_WIDTH, D_MODEL), MIX_WIDTH ** -0.5),
        'nsa_w_cmp': (1.0 + nrm(ks[17], (DEPTH, CMP_BLOCK, 2), 0.1)) / CMP_BLOCK,
        't5_bias': nrm(ks[18], (N_BUCKETS, NSA_HEADS), 0.5),
        'gla_w_gk2': nrm(ks[19], (DEPTH, GLA_GATE_RANK, GLA_HEADS * GLA_DK), GLA_GATE_RANK ** -0.5),
        'gla_b_gk': nrm(ks[20], (DEPTH, GLA_HEADS * GLA_DK), 0.1),
        'gla_norm_w': 1.0 + nrm(ks[21], (DEPTH, GLA_DV), 0.05),
        'gdn_conv_w': nrm(ks[22], (DEPTH, CONV_W, 3 * GDN_WIDTH), CONV_W ** -0.5),
        'gdn_a_log': jnp.log(jax.random.uniform(ks[23], (DEPTH, GDN_HEADS), f32, 1.0, 16.0)),
        'gdn_dt_bias': dt + jnp.log(-jnp.expm1(-dt)),
        'gdn_norm_w': 1.0 + nrm(ks[25], (DEPTH, GDN_DV), 0.05),
        'ple_w_proj': nrm(ks[26], (DEPTH, PLE_DIM, D_MODEL), PLE_DIM ** -0.5),
        'ple_w_gate': nrm(ks[27], (DEPTH, D_MODEL, D_MODEL), D_MODEL ** -0.5),
    }


def reference(x_prompt, x_sample, cache_cmp, cache_sel, cache_win, state_gla, state_gdn, state_conv, page_table,
              p_prompt, p_sample, norm_w, ffn_w1, ffn_w3, ffn_w2, w_in, w_out, nsa_w_cmp, t5_bias,
              gla_w_gk2, gla_b_gk, gla_norm_w, gdn_conv_w, gdn_a_log, gdn_dt_bias, gdn_norm_w,
              ple_w_proj, ple_w_gate):
    xp, xs = x_prompt, x_sample
    st_p, st_s = [], []
    for i in range(DEPTH):
        mix_w = (w_in[i], nsa_w_cmp[i], t5_bias, gla_w_gk2[i], gla_b_gk[i], gla_norm_w[i],
                 gdn_conv_w[i], gdn_a_log[i], gdn_dt_bias[i], gdn_norm_w[i])
        ff_w = (norm_w[i], ffn_w1[i], ffn_w3[i], ffn_w2[i], w_out[i], ple_w_proj[i], ple_w_gate[i])
        xp, sp = layer_forward(xp, p_prompt[i], lambda h: mixer_prompt(h, *mix_w), *ff_w)
        xs, ss = layer_forward(xs, p_sample[i], lambda h: mixer_sample(
            h, cache_cmp[i], cache_sel[i], cache_win[i], state_gla[i], state_gdn[i], state_conv[i], page_table,
            *mix_w), *ff_w)
        st_p.append(sp)
        st_s.append(ss)
    return (xp, xs,
            stack_layers(st_p, 0), stack_layers(st_s, 0),
            stack_layers(st_p, 1), stack_layers(st_s, 1),
            stack_layers(st_p, 2), stack_layers(st_s, 2),
            stack_layers(st_p, 3), stack_layers(st_s, 3),
            stack_layers(st_p, 4), stack_layers(st_s, 4),
            stack_layers(st_p, 5), stack_layers(st_s, 5))
```

```python
import functools
import math

import numpy as np
import jax
import jax.numpy as jnp
from jax import lax
from jax.experimental import pallas as pl
from jax.experimental.pallas import tpu as pltpu

F32 = jnp.float32
BF16 = jnp.bfloat16
I32 = jnp.int32

D_MODEL = 2048
HEAD_DIM = 128
NSA_HEADS = 8
NSA_KV_HEADS = 2
NSA_GROUP = NSA_HEADS // NSA_KV_HEADS
CMP_BLOCK = 64
N_SELECT = 8
WINDOW = 512
Q_BLOCK = 128
SEL_FORCE = 1.0e4
GLA_HEADS = 4
GLA_DK = 64
GLA_DV = 128
GLA_GATE_RANK = 16
GLA_GATE_NORM = 16.0
GDN_HEADS = 4
GDN_DK = 128
GDN_DV = 128
CHUNK = 64
CONV_W = 4
N_BUCKETS = 32
T5_MAX_DIST = 128
D_FF = 5632
PLE_DIM = 256
RMS_EPS = 1e-6
NEG = -1e30
PAGE_SIZE = 128

NSA_WIDTH = NSA_HEADS * HEAD_DIM
KVW = 2 * NSA_KV_HEADS * HEAD_DIM
GDN_WIDTH = GDN_HEADS * GDN_DV
GLA_WIDTH = GLA_HEADS * GLA_DV

C_NQ = 0
C_KVC = 1024
C_KVS = 1536
C_KVW = 2048
C_GQ = 2560
C_GK = 2816
C_DQKV = 3072
C_GV = 4608
C_GG = 5120
C_DZ = 5632
C_SM = 6144
PROJ_N = 6272
SM_GATE, SM_GLR, SM_DB, SM_DA = 0, 24, 40, 44

VMEM_LIMIT = 56 * 1024 * 1024
TM = 640
TF = 512
TN_PROJ = 896
TB = 256


def _sigmoid(x):
    return 1.0 / (1.0 + jnp.exp(-x))


def _silu(x):
    return x * _sigmoid(x)


def _softplus(x):
    return jnp.maximum(x, 0.0) + jnp.log(1.0 + jnp.exp(-jnp.abs(x)))


def _log_sigmoid(x):
    return jnp.minimum(x, 0.0) - jnp.log(1.0 + jnp.exp(-jnp.abs(x)))


def _rms(x, w):
    return x * lax.rsqrt(jnp.mean(x * x, axis=-1, keepdims=True) + RMS_EPS) * w


_NN = (((1,), (0,)), ((), ()))
_NT = (((1,), (1,)), ((), ()))
_TN = (((0,), (0,)), ((), ()))


def _bdot(a, b, dims=_NN):
    return lax.dot_general(a.astype(BF16), b.astype(BF16), dims, preferred_element_type=F32)


def _hdot(a, b, dims=_NN):
    return lax.dot_general(a, b, dims, preferred_element_type=F32, precision=lax.Precision.HIGHEST)


def _params(sem):
    return pltpu.CompilerParams(dimension_semantics=sem, vmem_limit_bytes=VMEM_LIMIT)


def _ffn_kernel(x_ref, nwa_ref, w1_ref, w3_ref, w2_ref, nwb_ref, o_ref, h_ref, acc_ref):
    f = pl.program_id(1)

    @pl.when(f == 0)
    def _():
        h_ref[...] = _rms(x_ref[...], nwa_ref[...]).astype(BF16)
        acc_ref[...] = jnp.zeros_like(acc_ref)

    h = h_ref[...]
    a = jnp.dot(h, w1_ref[...], preferred_element_type=F32)
    b = jnp.dot(h, w3_ref[...], preferred_element_type=F32)
    g = (_silu(a) * b).astype(BF16)
    acc_ref[...] += jnp.dot(g, w2_ref[...], preferred_element_type=F32)

    @pl.when(f == pl.num_programs(1) - 1)
    def _():
        o_ref[...] = x_ref[...] + 0.5 * _rms(acc_ref[...], nwb_ref[...])


def _ffn(x, nwa, w1, w3, w2, nwb):
    m, d = x.shape
    dff = w1.shape[1]
    row = lambda i, f: (i, 0)
    return pl.pallas_call(
        _ffn_kernel,
        out_shape=jax.ShapeDtypeStruct((m, d), F32),
        grid=(m // TM, dff // TF),
        in_specs=[pl.BlockSpec((TM, d), row),
                  pl.BlockSpec((1, d), lambda i, f: (0, 0)),
                  pl.BlockSpec((d, TF), lambda i, f: (0, f)),
                  pl.BlockSpec((d, TF), lambda i, f: (0, f)),
                  pl.BlockSpec((TF, d), lambda i, f: (f, 0)),
                  pl.BlockSpec((1, d), lambda i, f: (0, 0))],
        out_specs=pl.BlockSpec((TM, d), row),
        scratch_shapes=[pltpu.VMEM((TM, d), BF16), pltpu.VMEM((TM, d), F32)],
        compiler_params=_params(("parallel", "arbitrary")),
        name="ffn",
    )(x, nwa, w1, w3, w2, nwb)


def _proj_kernel(x_ref, nw_ref, w_ref, o_ref, h_ref):
    @pl.when(pl.program_id(1) == 0)
    def _():
        h_ref[...] = _rms(x_ref[...], nw_ref[...]).astype(BF16)

    o_ref[...] = jnp.dot(h_ref[...], w_ref[...], preferred_element_type=F32)


def _proj(x, nw, w):
    m, d = x.shape
    n = w.shape[1]
    return pl.pallas_call(
        _proj_kernel,
        out_shape=jax.ShapeDtypeStruct((m, n), F32),
        grid=(m // TM, n // TN_PROJ),
        in_specs=[pl.BlockSpec((TM, d), lambda i, j: (i, 0)),
                  pl.BlockSpec((1, d), lambda i, j: (0, 0)),
                  pl.BlockSpec((d, TN_PROJ), lambda i, j: (0, j))],
        out_specs=pl.BlockSpec((TM, TN_PROJ), lambda i, j: (i, j)),
        scratch_shapes=[pltpu.VMEM((TM, d), BF16)],
        compiler_params=_params(("parallel", "arbitrary")),
        name="proj",
    )(x, nw, w)


def _outproj_kernel(y_ref, x_ref, w_ref, nw_ref, o_ref):
    z = jnp.dot(y_ref[...].astype(BF16), w_ref[...], preferred_element_type=F32)
    o_ref[...] = x_ref[...] + _rms(z, nw_ref[...])


def _outproj(y, x, w, nw):
    m, d = x.shape
    k = y.shape[1]
    return pl.pallas_call(
        _outproj_kernel,
        out_shape=jax.ShapeDtypeStruct((m, d), F32),
        grid=(m // TM,),
        in_specs=[pl.BlockSpec((TM, k), lambda i: (i, 0)),
                  pl.BlockSpec((TM, d), lambda i: (i, 0)),
                  pl.BlockSpec((k, d), lambda i: (0, 0)),
                  pl.BlockSpec((1, d), lambda i: (0, 0))],
        out_specs=pl.BlockSpec((TM, d), lambda i: (i, 0)),
        compiler_params=_params(("parallel",)),
        name="outproj",
    )(y, x, w, nw)


def _ple_kernel(x_ref, p_ref, nwa_ref, wg_ref, wp_ref, nwb_ref, o_ref):
    x = x_ref[...]
    gate = _sigmoid(jnp.dot(_rms(x, nwa_ref[...]).astype(BF16), wg_ref[...], preferred_element_type=F32))
    pp = jnp.dot(p_ref[...].astype(BF16), wp_ref[...], preferred_element_type=F32)
    o_ref[...] = x + _rms(gate * pp, nwb_ref[...])


def _ple(x, p, nwa, wg, wp, nwb):
    m, d = x.shape
    pd = p.shape[1]
    return pl.pallas_call(
        _ple_kernel,
        out_shape=jax.ShapeDtypeStruct((m, d), F32),
        grid=(m // TM,),
        in_specs=[pl.BlockSpec((TM, d), lambda i: (i, 0)),
                  pl.BlockSpec((TM, pd), lambda i: (i, 0)),
                  pl.BlockSpec((1, d), lambda i: (0, 0)),
                  pl.BlockSpec((d, d), lambda i: (0, 0)),
                  pl.BlockSpec((pd, d), lambda i: (0, 0)),
                  pl.BlockSpec((1, d), lambda i: (0, 0))],
        out_specs=pl.BlockSpec((TM, d), lambda i: (i, 0)),
        compiler_params=_params(("parallel",)),
        name="ple",
    )(x, p, nwa, wg, wp, nwb)


def _t5_bucket_np(dist):
    n = np.maximum(dist, 0)
    exact = N_BUCKETS // 2
    val = (np.log(np.maximum(n, 1).astype(np.float32) / np.float32(exact))
           / np.float32(math.log(T5_MAX_DIST / exact)) * np.float32(N_BUCKETS - exact))
    large = exact + val.astype(np.int32)
    return np.where(n < exact, n, np.minimum(large, N_BUCKETS - 1)).astype(np.int32)


def _bias_table(t5_bias, dist, valid):
    tbl = jnp.take(t5_bias, jnp.asarray(_t5_bucket_np(dist).reshape(-1)), axis=0)
    tbl = tbl.T.reshape((t5_bias.shape[1],) + dist.shape)
    return jnp.where(jnp.asarray(valid)[None], tbl, NEG).astype(F32)


def _prompt_bias_tables(t5_bias, seq):
    nb = seq // CMP_BLOCK
    qpos = np.arange(seq)[:, None]
    d_c = qpos - (np.arange(nb) * CMP_BLOCK + CMP_BLOCK - 1)[None, :]
    bias_c = _bias_table(t5_bias, d_c, d_c >= 0)
    i = np.arange(Q_BLOCK)[:, None]
    jj = np.arange(Q_BLOCK)[None, :]
    d_s = np.stack([Q_BLOCK * dl + i - jj for dl in range(3)])
    bias_s = jnp.transpose(_bias_table(t5_bias, d_s, d_s >= 0), (1, 0, 2, 3))
    nwc = WINDOW // Q_BLOCK + 1
    d_w = np.stack([i + WINDOW - Q_BLOCK * cw - jj for cw in range(nwc)])
    bias_w = jnp.transpose(_bias_table(t5_bias, d_w, (d_w >= 0) & (d_w < WINDOW)), (1, 0, 2, 3))
    return bias_c, bias_s, bias_w


def _sample_bias_tables(t5_bias, past, win_len):
    nb_past = past // CMP_BLOCK
    nbp = 40
    n = np.arange(nbp)
    d_c = past - (n * CMP_BLOCK + CMP_BLOCK - 1)
    bias_c = _bias_table(t5_bias, d_c, (d_c >= 0) & (n <= nb_past))
    blk = np.arange(nb_past + 1)[:, None]
    d_s = past - (blk * CMP_BLOCK + np.arange(CMP_BLOCK)[None, :])
    bias_s = jnp.transpose(_bias_table(t5_bias, d_s, d_s >= 0), (1, 0, 2))
    wpos = past - win_len + 1 + np.arange(win_len)
    d_w = past - wpos
    bias_w = _bias_table(t5_bias, d_w, (d_w >= 0) & (d_w < WINDOW) & (wpos >= 0))
    return bias_c, bias_s, bias_w


def _compress_kernel(x_ref, w_ref, o_ref):
    r = x_ref.shape[0] // CMP_BLOCK
    x = x_ref[...].reshape(r, CMP_BLOCK, KVW)
    o_ref[...] = jnp.sum(x * w_ref[...][None], axis=1)


def _compress_prompt(proj, wtile, rows):
    rb = 512
    return pl.pallas_call(
        _compress_kernel,
        out_shape=jax.ShapeDtypeStruct((rows // CMP_BLOCK, KVW), F32),
        grid=(rows // rb,),
        in_specs=[pl.BlockSpec((rb, KVW), lambda i: (i, C_KVC // KVW)),
                  pl.BlockSpec((CMP_BLOCK, KVW), lambda i: (0, 0))],
        out_specs=pl.BlockSpec((rb // CMP_BLOCK, KVW), lambda i: (i, 0)),
        compiler_params=_params(("parallel",)),
        name="compress_prompt",
    )(proj, wtile)


def _online_softmax_step(s, v, m_ref, l_ref, acc_ref):
    valid = s > -1e29
    m_old = m_ref[...]
    m_new = jnp.maximum(m_old, jnp.max(s, axis=-1, keepdims=True))
    p = jnp.where(valid, jnp.exp(s - m_new), 0.0)
    alpha = jnp.exp(m_old - m_new)
    l_ref[...] = alpha * l_ref[...] + jnp.sum(p, axis=-1, keepdims=True)
    acc_ref[...] = alpha * acc_ref[...] + _bdot(p, v)
    m_ref[...] = m_new


def _nsa_prompt_kernel(q_ref, sm_ref, cmp_ref, kvs_ref, kvw_ref, bc_ref, bs_ref, bw_ref, e_ref, o_ref,
                       mask_ref, m_ref, l_ref, acc_ref):
    j = pl.program_id(1)
    nb = cmp_ref.shape[0]
    nqb = mask_ref.shape[0]
    hg, qb, dh = NSA_GROUP, Q_BLOCK, HEAD_DIM
    rows = hg * qb
    nwc = bw_ref.shape[0]
    gates = _sigmoid(sm_ref[:, SM_GATE:SM_GATE + 3 * NSA_HEADS])
    qi = lax.broadcasted_iota(I32, (qb, nb), 0)
    ni = lax.broadcasted_iota(I32, (qb, nb), 1)
    cur = (qb // CMP_BLOCK) * j + qi // CMP_BLOCK
    forced = (ni == cur) | (ni == cur - 1) | (ni == 0)
    started = ni <= cur

    def reset():
        m_ref[...] = jnp.full(m_ref.shape, NEG, F32)
        l_ref[...] = jnp.zeros(l_ref.shape, F32)
        acc_ref[...] = jnp.zeros(acc_ref.shape, F32)

    def result():
        return acc_ref[...] / jnp.maximum(l_ref[...], 1e-30)

    for g in range(NSA_KV_HEADS):
        q = jnp.concatenate([q_ref[:, (g * hg + h) * dh:(g * hg + h + 1) * dh] for h in range(hg)], axis=0)
        qb16 = (q * (dh ** -0.5)).astype(BF16)
        kcol = slice(g * dh, (g + 1) * dh)
        vcol = slice((NSA_KV_HEADS + g) * dh, (NSA_KV_HEADS + g + 1) * dh)

        bias = bc_ref[g * hg:(g + 1) * hg].reshape(rows, nb)
        valid = bias > -1e29
        s = jnp.where(valid, _bdot(qb16, cmp_ref[:, kcol], _NT) + bias, NEG)
        p = jnp.where(valid, jnp.exp(s - jnp.max(s, axis=-1, keepdims=True)), 0.0)
        p = p / jnp.maximum(jnp.sum(p, axis=-1, keepdims=True), 1e-30)
        o_c = _bdot(p, cmp_ref[:, vcol])

        imp = p[0:qb]
        for h in range(1, hg):
            imp = imp + p[h * qb:(h + 1) * qb]
        score = jnp.where(started, jnp.where(forced, SEL_FORCE, imp), -1.0)
        sel = jnp.zeros((qb, nb), F32)
        for _ in range(min(N_SELECT, nb)):
            mval = jnp.max(score, axis=-1, keepdims=True)
            idx = jnp.min(jnp.where(score == mval, ni, nb), axis=-1, keepdims=True)
            hit = ni == idx
            sel = jnp.where(hit, jnp.where(mval >= 0.0, 1.0, 0.0), sel)
            score = jnp.where(hit, -3e38, score)
        selexp = jnp.dot(sel.astype(BF16), e_ref[...], preferred_element_type=F32)
        for c in range(nqb):
            mask_ref[c] = selexp[:, c * qb:(c + 1) * qb]

        reset()

        def sel_step(c, carry):
            r0 = pl.multiple_of(c * qb, qb)
            k = kvs_ref[pl.ds(r0, qb), kcol]
            v = kvs_ref[pl.ds(r0, qb), vcol]
            bt = bs_ref[jnp.minimum(j - c, 2), g * hg:(g + 1) * hg].reshape(rows, qb)
            mk = mask_ref[c]
            mk = jnp.concatenate([mk] * hg, axis=0)
            s_ = jnp.where(mk > 0.0, _bdot(qb16, k, _NT) + bt, NEG)
            _online_softmax_step(s_, v, m_ref, l_ref, acc_ref)
            return carry

        lax.fori_loop(0, j + 1, sel_step, 0)
        o_s = result()

        reset()

        def win_step(cw, carry):
            r0 = pl.multiple_of((j - (nwc - 1) + cw) * qb, qb)
            k = kvw_ref[pl.ds(r0, qb), kcol]
            v = kvw_ref[pl.ds(r0, qb), vcol]
            bt = bw_ref[cw, g * hg:(g + 1) * hg].reshape(rows, qb)
            s_ = _bdot(qb16, k, _NT) + bt
            _online_softmax_step(s_, v, m_ref, l_ref, acc_ref)
            return carry

        lax.fori_loop(jnp.maximum(0, nwc - 1 - j), nwc, win_step, 0)
        o_w = result()

        for h in range(hg):
            c0 = SM_GATE + (g * hg + h) * 3
            rs = slice(h * qb, (h + 1) * qb)
            o_ref[:, (g * hg + h) * dh:(g * hg + h + 1) * dh] = (
                o_c[rs] * gates[:, c0:c0 + 1] + o_s[rs] * gates[:, c0 + 1:c0 + 2] + o_w[rs] * gates[:, c0 + 2:c0 + 3])


def _nsa_prompt(proj, cmpkv, kvs16, kvw16, bias_c, bias_s, bias_w, expand, batch, seq):
    nqb = seq // Q_BLOCK
    nb = seq // CMP_BLOCK
    rows = NSA_GROUP * Q_BLOCK
    full = lambda shape: pl.BlockSpec(shape, lambda b, j: (0,) * len(shape))
    return pl.pallas_call(
        _nsa_prompt_kernel,
        out_shape=jax.ShapeDtypeStruct((batch * seq, NSA_WIDTH), F32),
        grid=(batch, nqb),
        in_specs=[pl.BlockSpec((Q_BLOCK, NSA_WIDTH), lambda b, j: (b * nqb + j, 0)),
                  pl.BlockSpec((Q_BLOCK, 128), lambda b, j: (b * nqb + j, C_SM // 128)),
                  pl.BlockSpec((nb, KVW), lambda b, j: (b, 0)),
                  pl.BlockSpec((seq, KVW), lambda b, j: (b, 0)),
                  pl.BlockSpec((seq, KVW), lambda b, j: (b, 0)),
                  pl.BlockSpec((NSA_HEADS, Q_BLOCK, nb), lambda b, j: (0, j, 0)),
                  full(bias_s.shape), full(bias_w.shape), full(expand.shape)],
        out_specs=pl.BlockSpec((Q_BLOCK, NSA_WIDTH), lambda b, j: (b * nqb + j, 0)),
        scratch_shapes=[pltpu.VMEM((nqb, Q_BLOCK, Q_BLOCK), F32),
                        pltpu.VMEM((rows, 1), F32), pltpu.VMEM((rows, 1), F32),
                        pltpu.VMEM((rows, HEAD_DIM), F32)],
        compiler_params=_params(("parallel", "arbitrary")),
        name="nsa_prompt",
    )(proj, proj, cmpkv, kvs16, kvw16, bias_c, bias_s, bias_w, expand)


def _tri_masks():
    r = lax.broadcasted_iota(I32, (CHUNK, CHUNK), 0)
    c = lax.broadcasted_iota(I32, (CHUNK, CHUNK), 1)
    return r, c


def _gla_prompt_kernel(q_ref, k_ref, v_ref, gg_ref, sm_ref, w2_ref, b2_ref, nw_ref, o_ref, so_ref, s_ref):
    t = pl.program_id(1)

    @pl.when(t == 0)
    def _():
        s_ref[...] = jnp.zeros_like(s_ref)

    r, c = _tri_masks()
    lower = r >= c
    tril = jnp.where(lower, 1.0, 0.0).astype(F32)
    log_a = _log_sigmoid(_bdot(sm_ref[:, SM_GLR:SM_GLR + GLA_GATE_RANK], w2_ref[...]) + b2_ref[...]) / GLA_GATE_NORM
    dk, dv = GLA_DK, GLA_DV
    for ch in range(q_ref.shape[0] // CHUNK):
        rs = slice(ch * CHUNK, (ch + 1) * CHUNK)
        for h in range(GLA_HEADS):
            a = log_a[rs, h * dk:(h + 1) * dk]
            b = _hdot(tril, a)
            q = q_ref[rs, h * dk:(h + 1) * dk] * (dk ** -0.5)
            k = k_ref[rs, h * dk:(h + 1) * dk]
            v = v_ref[rs, h * dv:(h + 1) * dv]
            qe = q * jnp.exp(b)
            att = jnp.where(lower, _bdot(qe, k * jnp.exp(-b), _NT), 0.0)
            s = s_ref[h]
            o = _bdot(qe, s) + _bdot(att, v)
            b_last = b[CHUNK - 1:CHUNK, :]
            b_last_col = b.T[:, CHUNK - 1:CHUNK]
            s_ref[h] = jnp.exp(b_last_col) * s + _bdot(k * jnp.exp(b_last - b), v, _TN)
            y = _rms(o, nw_ref[...]) * _silu(gg_ref[rs, h * dv:(h + 1) * dv])
            o_ref[rs, h * dv:(h + 1) * dv] = y
    so_ref[0] = s_ref[...]


def _gla_prompt(proj, w2, b2, nw, batch, seq):
    nt = seq // TB
    row = lambda w, col: pl.BlockSpec((TB, w), lambda b, t: (b * nt + t, col // w))
    full = lambda shape: pl.BlockSpec(shape, lambda b, t: (0,) * len(shape))
    return pl.pallas_call(
        _gla_prompt_kernel,
        out_shape=(jax.ShapeDtypeStruct((batch * seq, GLA_WIDTH), F32),
                   jax.ShapeDtypeStruct((batch, GLA_HEADS, GLA_DK, GLA_DV), F32)),
        grid=(batch, nt),
        in_specs=[row(GLA_HEADS * GLA_DK, C_GQ), row(GLA_HEADS * GLA_DK, C_GK), row(GLA_WIDTH, C_GV),
                  row(GLA_WIDTH, C_GG), row(128, C_SM), full(w2.shape), full(b2.shape), full(nw.shape)],
        out_specs=(pl.BlockSpec((TB, GLA_WIDTH), lambda b, t: (b * nt + t, 0)),
                   pl.BlockSpec((1, GLA_HEADS, GLA_DK, GLA_DV), lambda b, t: (b, 0, 0, 0))),
        scratch_shapes=[pltpu.VMEM((GLA_HEADS, GLA_DK, GLA_DV), F32)],
        compiler_params=_params(("parallel", "arbitrary")),
        name="gla_prompt",
    )(proj, proj, proj, proj, proj, w2, b2, nw)


def _unit_lower_inverse(m, r, c):
    eye = jnp.where(r == c, 1.0, 0.0).astype(F32)
    base = 8
    m8 = jnp.where((r // base) == (c // base), m, 0.0)
    m2 = _hdot(m8, m8)
    m4 = _hdot(m2, m2)
    t = _hdot(_hdot(eye - m8, eye + m2), eye + m4)
    s = base
    while s < CHUNK:
        off = ((r // (2 * s)) == (c // (2 * s))) & ((r // s) != (c // s))
        t = t - _hdot(t, _hdot(jnp.where(off, m, 0.0), t))
        s *= 2
    return t


def _gdn_prompt_kernel(x_ref, dz_ref, sm_ref, cw_ref, al_ref, dt_ref, nw_ref, o_ref, so_ref, s_ref, tail_ref):
    t = pl.program_id(1)

    @pl.when(t == 0)
    def _():
        s_ref[...] = jnp.zeros_like(s_ref)
        tail_ref[...] = jnp.zeros_like(tail_ref)

    tb = x_ref.shape[0]
    x = x_ref[...]
    xc = jnp.concatenate([tail_ref[...], x], axis=0)
    off = 8 - (CONV_W - 1)
    y = xc[off:off + tb] * cw_ref[0:1, :]
    for jw in range(1, CONV_W):
        y = y + xc[off + jw:off + jw + tb] * cw_ref[jw:jw + 1, :]
    y = _silu(y)
    tail_ref[...] = x[tb - 8:tb]

    beta_all = _sigmoid(sm_ref[:, SM_DB:SM_DB + GDN_HEADS])
    g_all = -jnp.exp(al_ref[...]) * _softplus(sm_ref[:, SM_DA:SM_DA + GDN_HEADS] + dt_ref[...])
    r, c = _tri_masks()
    lower = r >= c
    strict = r > c
    tril = jnp.where(lower, 1.0, 0.0).astype(F32)
    dk, dv = GDN_DK, GDN_DV
    for ch in range(tb // CHUNK):
        rs = slice(ch * CHUNK, (ch + 1) * CHUNK)
        for h in range(GDN_HEADS):
            cq = y[rs, h * dk:(h + 1) * dk]
            ck = y[rs, GDN_WIDTH + h * dk:GDN_WIDTH + (h + 1) * dk]
            v = y[rs, 2 * GDN_WIDTH + h * dv:2 * GDN_WIDTH + (h + 1) * dv]
            q = cq * lax.rsqrt(jnp.sum(cq * cq, axis=-1, keepdims=True) + 1e-6) * (dk ** -0.5)
            k = ck * lax.rsqrt(jnp.sum(ck * ck, axis=-1, keepdims=True) + 1e-6)
            beta = beta_all[rs, h:h + 1]
            gam = _hdot(tril, jnp.broadcast_to(g_all[rs, h:h + 1], (CHUNK, dk)))
            gam_row = gam.T[0:CHUNK, :]
            decay = jnp.exp(jnp.where(lower, gam[:, 0:CHUNK] - gam_row, NEG))
            kb = k * beta
            m = jnp.where(strict, _hdot(kb, k, _NT) * decay, 0.0)
            tinv = _unit_lower_inverse(m, r, c)
            eg = jnp.exp(gam)
            u = _hdot(tinv, v * beta)
            w = _hdot(tinv, kb * eg)
            s = s_ref[h]
            v_new = u - _bdot(w, s)
            att = _bdot(q, k, _NT) * decay
            o = _bdot(q * eg, s) + _bdot(att, v_new)
            g_last = gam[CHUNK - 1:CHUNK, :]
            s_ref[h] = jnp.exp(g_last) * s + _bdot(k * jnp.exp(g_last - gam), v_new, _TN)
            o_ref[rs, h * dv:(h + 1) * dv] = _rms(o, nw_ref[...]) * _silu(dz_ref[rs, h * dv:(h + 1) * dv])
    so_ref[0] = s_ref[...]


def _gdn_prompt(proj, cw, a_log, dt_bias, nw, batch, seq):
    nt = seq // TB
    row = lambda w, col: pl.BlockSpec((TB, w), lambda b, t: (b * nt + t, col // w))
    full = lambda shape: pl.BlockSpec(shape, lambda b, t: (0,) * len(shape))
    return pl.pallas_call(
        _gdn_prompt_kernel,
        out_shape=(jax.ShapeDtypeStruct((batch * seq, GDN_WIDTH), F32),
                   jax.ShapeDtypeStruct((batch, GDN_HEADS, GDN_DK, GDN_DV), F32)),
        grid=(batch, nt),
        in_specs=[row(3 * GDN_WIDTH, C_DQKV), row(GDN_WIDTH, C_DZ), row(128, C_SM),
                  full(cw.shape), full(a_log.shape), full(dt_bias.shape), full(nw.shape)],
        out_specs=(pl.BlockSpec((TB, GDN_WIDTH), lambda b, t: (b * nt + t, 0)),
                   pl.BlockSpec((1, GDN_HEADS, GDN_DK, GDN_DV), lambda b, t: (b, 0, 0, 0))),
        scratch_shapes=[pltpu.VMEM((GDN_HEADS, GDN_DK, GDN_DV), F32), pltpu.VMEM((8, 3 * GDN_WIDTH), F32)],
        compiler_params=_params(("parallel", "arbitrary")),
        name="gdn_prompt",
    )(proj, proj, proj, cw, a_log, dt_bias, nw)


def _masked_softmax_rows(s):
    valid = s > -1e29
    p = jnp.where(valid, jnp.exp(s - jnp.max(s, axis=-1, keepdims=True)), 0.0)
    return p / jnp.maximum(jnp.sum(p, axis=-1, keepdims=True), 1e-30)


def _group_query(q_ref, g):
    hg, dh = NSA_GROUP, HEAD_DIM
    q = jnp.concatenate([q_ref[0, :, (g * hg + h) * dh:(g * hg + h + 1) * dh] for h in range(hg)], axis=0)
    return (q * (dh ** -0.5)).astype(BF16)


def _nsa_sample_a_kernel(n_pages, nb_past, pt_ref, q_ref, kvc_ref, kvw_ref, sm_ref, *rest):
    page_refs = rest[:n_pages]
    win_ref, wt_ref, bc_ref, bw_ref, part_ref, idx_ref, wo_ref, cmp_ref = rest[n_pages:]
    hg, dh = NSA_GROUP, HEAD_DIM
    bpp = PAGE_SIZE // CMP_BLOCK
    nbp = cmp_ref.shape[0]
    wt = wt_ref[...]
    for p in range(n_pages):
        blk = page_refs[p][0].reshape(bpp, CMP_BLOCK, KVW)
        cmp_ref[bpp * p:bpp * (p + 1), :] = jnp.sum(blk * wt[None], axis=1)
    cmp_ref[nb_past:nb_past + 1, :] = wt[0:1, :] * kvc_ref[0]
    cmp_ref[nb_past + 1:nbp, :] = jnp.zeros((nbp - nb_past - 1, KVW), F32)
    cm = cmp_ref[...]

    buf = win_ref[0]
    wl = buf.shape[0]
    ri = lax.broadcasted_iota(I32, buf.shape, 0)
    wout = jnp.where(ri == wl - 1, kvw_ref[0], pltpu.roll(buf, wl - 1, 0))
    wo_ref[0] = wout

    gates = _sigmoid(sm_ref[0, :, SM_GATE:SM_GATE + 3 * NSA_HEADS])
    ni = lax.broadcasted_iota(I32, (1, nbp), 1)
    li = lax.broadcasted_iota(I32, (1, 2 * N_SELECT), 1)
    cur = nb_past
    forced = (ni == cur) | (ni == cur - 1) | (ni == 0)
    idx_row = jnp.zeros((1, 2 * N_SELECT), I32)
    for g in range(NSA_KV_HEADS):
        q = _group_query(q_ref, g)
        kcol = slice(g * dh, (g + 1) * dh)
        vcol = slice((NSA_KV_HEADS + g) * dh, (NSA_KV_HEADS + g + 1) * dh)
        p = _masked_softmax_rows(_bdot(q, cm[:, kcol], _NT) + bc_ref[g * hg:(g + 1) * hg, :])
        o_c = _bdot(p, cm[:, vcol])
        imp = jnp.sum(p, axis=0, keepdims=True)
        score = jnp.where(ni <= cur, jnp.where(forced, SEL_FORCE, imp), -3e38)
        for r in range(N_SELECT):
            mval = jnp.max(score, axis=-1, keepdims=True)
            idx = jnp.min(jnp.where(score == mval, ni, nbp), axis=-1, keepdims=True)
            idx_row = jnp.where(li == g * N_SELECT + r, jnp.where(mval >= 0.0, idx, -1), idx_row)
            score = jnp.where(ni == idx, -3e38, score)
        pw = _masked_softmax_rows(_bdot(q, wout[:, kcol], _NT) + bw_ref[g * hg:(g + 1) * hg, :])
        o_w = _bdot(pw, wout[:, vcol])
        for h in range(hg):
            c0 = (g * hg + h) * 3
            part_ref[0, :, (g * hg + h) * dh:(g * hg + h + 1) * dh] = (
                o_c[h:h + 1] * gates[:, c0:c0 + 1] + o_w[h:h + 1] * gates[:, c0 + 2:c0 + 3])
    idx_ref[0] = idx_row


def _nsa_sample_a(page_table, proj_s3, pool_cmp, cache_win, wtile, bias_c, bias_w, layer, n_phys):
    nbatch, n_pages = page_table.shape
    nb_past = n_pages * PAGE_SIZE // CMP_BLOCK
    wl = cache_win.shape[1]
    nbp = bias_c.shape[1]
    col = lambda w, c: pl.BlockSpec((1, 1, w), lambda b, pt: (b, 0, c // w))
    full = lambda shape: pl.BlockSpec(shape, lambda b, pt: (0,) * len(shape))
    page = lambda p: pl.BlockSpec((1, PAGE_SIZE, KVW), lambda b, pt: (layer * n_phys + pt[b, p], 0, 0))
    gs = pltpu.PrefetchScalarGridSpec(
        num_scalar_prefetch=1, grid=(nbatch,),
        in_specs=[col(NSA_WIDTH, C_NQ), col(KVW, C_KVC), col(KVW, C_KVW), col(128, C_SM)]
        + [page(p) for p in range(n_pages)]
        + [pl.BlockSpec((1, wl, KVW), lambda b, pt: (layer * nbatch + b, 0, 0)),
           full(wtile.shape), full(bias_c.shape), full(bias_w.shape)],
        out_specs=(pl.BlockSpec((1, 1, NSA_WIDTH), lambda b, pt: (b, 0, 0)),
                   pl.BlockSpec((1, 1, 2 * N_SELECT), lambda b, pt: (b, 0, 0)),
                   pl.BlockSpec((1, wl, KVW), lambda b, pt: (b, 0, 0))),
        scratch_shapes=[pltpu.VMEM((nbp, KVW), F32)])
    return pl.pallas_call(
        functools.partial(_nsa_sample_a_kernel, n_pages, nb_past),
        out_shape=(jax.ShapeDtypeStruct((nbatch, 1, NSA_WIDTH), F32),
                   jax.ShapeDtypeStruct((nbatch, 1, 2 * N_SELECT), I32),
                   jax.ShapeDtypeStruct((nbatch, wl, KVW), F32)),
        grid_spec=gs,
        compiler_params=_params(("arbitrary",)),
        name="nsa_sample_a",
    )(page_table, proj_s3, proj_s3, proj_s3, proj_s3, *([pool_cmp] * n_pages), cache_win, wtile, bias_c, bias_w)


def _nsa_sample_b_kernel(nb_past, pt_ref, ix_ref, q_ref, kvs_ref, sm_ref, part_ref, *rest):
    nsel = NSA_KV_HEADS * N_SELECT
    blk_refs = rest[:nsel]
    bs_ref, o_ref = rest[nsel:]
    b = pl.program_id(0)
    hg, dh = NSA_GROUP, HEAD_DIM
    gates = _sigmoid(sm_ref[0, :, SM_GATE:SM_GATE + 3 * NSA_HEADS])
    ri = lax.broadcasted_iota(I32, (CMP_BLOCK, dh), 0)
    for g in range(NSA_KV_HEADS):
        q = _group_query(q_ref, g)
        kcol = slice(g * dh, (g + 1) * dh)
        vcol = slice((NSA_KV_HEADS + g) * dh, (NSA_KV_HEADS + g + 1) * dh)
        new_k = jnp.where(ri == 0, kvs_ref[0, :, kcol], 0.0)
        new_v = jnp.where(ri == 0, kvs_ref[0, :, vcol], 0.0)
        scores, vals = [], []
        for i in range(N_SELECT):
            idx = ix_ref[b, g * N_SELECT + i]
            is_new = idx >= nb_past
            k = jnp.where(is_new, new_k, blk_refs[g * N_SELECT + i][0, :, kcol])
            v = jnp.where(is_new, new_v, blk_refs[g * N_SELECT + i][0, :, vcol])
            bt = bs_ref[jnp.clip(idx, 0, nb_past), g * hg:(g + 1) * hg, :]
            s = jnp.where(idx >= 0, _bdot(q, k, _NT) + bt, NEG)
            scores.append(s)
            vals.append(v)
        mx = scores[0].max(axis=-1, keepdims=True)
        for s in scores[1:]:
            mx = jnp.maximum(mx, s.max(axis=-1, keepdims=True))
        den = jnp.zeros((hg, 1), F32)
        acc = jnp.zeros((hg, dh), F32)
        for s, v in zip(scores, vals):
            p = jnp.where(s > -1e29, jnp.exp(s - mx), 0.0)
            den = den + jnp.sum(p, axis=-1, keepdims=True)
            acc = acc + _bdot(p, v)
        o_s = acc / jnp.maximum(den, 1e-30)
        for h in range(hg):
            c0 = (g * hg + h) * 3
            cs = slice((g * hg + h) * dh, (g * hg + h + 1) * dh)
            o_ref[0, :, cs] = part_ref[0, :, cs] + o_s[h:h + 1] * gates[:, c0 + 1:c0 + 2]


def _nsa_sample_b(page_table, top_idx, proj_s3, part, pool_sel, bias_s, layer, n_phys):
    nbatch, n_pages = page_table.shape
    nb_past = n_pages * PAGE_SIZE // CMP_BLOCK
    bpp = PAGE_SIZE // CMP_BLOCK
    nsel = NSA_KV_HEADS * N_SELECT
    col = lambda w, c: pl.BlockSpec((1, 1, w), lambda b, pt, ix: (b, 0, c // w))

    def blk(gi):
        def imap(b, pt, ix):
            ip = jnp.clip(ix[b, gi], 0, nb_past - 1)
            return ((layer * n_phys + pt[b, ip // bpp]) * bpp + ip % bpp, 0, 0)
        return pl.BlockSpec((1, CMP_BLOCK, KVW), imap)

    gs = pltpu.PrefetchScalarGridSpec(
        num_scalar_prefetch=2, grid=(nbatch,),
        in_specs=[col(NSA_WIDTH, C_NQ), col(KVW, C_KVS), col(128, C_SM),
                  pl.BlockSpec((1, 1, NSA_WIDTH), lambda b, pt, ix: (b, 0, 0))]
        + [blk(gi) for gi in range(nsel)]
        + [pl.BlockSpec(bias_s.shape, lambda b, pt, ix: (0, 0, 0))],
        out_specs=pl.BlockSpec((1, 1, NSA_WIDTH), lambda b, pt, ix: (b, 0, 0)))
    return pl.pallas_call(
        functools.partial(_nsa_sample_b_kernel, nb_past),
        out_shape=jax.ShapeDtypeStruct((nbatch, 1, NSA_WIDTH), F32),
        grid_spec=gs,
        compiler_params=_params(("arbitrary",)),
        name="nsa_sample_b",
    )(page_table, top_idx, proj_s3, proj_s3, proj_s3, part, *([pool_sel] * nsel), bias_s)


SB = 8


def _rec_sample_kernel(gq_ref, gk_ref, gv_ref, gg_ref, x_ref, dz_ref, sm_ref, sg_ref, sd_ref, cb_ref,
                       w2_ref, b2_ref, gnw_ref, cw_ref, al_ref, dt_ref, dnw_ref,
                       o_ref, sgo_ref, sdo_ref):
    sm = sm_ref[...]
    ri = lax.broadcasted_iota(I32, (SB, 128), 0)
    log_a = _log_sigmoid(_bdot(sm[:, SM_GLR:SM_GLR + GLA_GATE_RANK], w2_ref[...]) + b2_ref[...]) / GLA_GATE_NORM
    ea_t = jnp.exp(log_a).T
    k_t = gk_ref[...].T
    q_t = (gq_ref[...] * (GLA_DK ** -0.5)).T
    gv = gv_ref[...]
    for h in range(GLA_HEADS):
        hs = slice(h * GLA_DK, (h + 1) * GLA_DK)
        vs = slice(h * GLA_DV, (h + 1) * GLA_DV)
        o_h = jnp.zeros((SB, GLA_DV), F32)
        for i in range(SB):
            s = ea_t[hs, i:i + 1] * sg_ref[i, h] + k_t[hs, i:i + 1] * gv[i:i + 1, vs]
            sgo_ref[i, h] = s
            o = jnp.sum(q_t[hs, i:i + 1] * s, axis=0, keepdims=True)
            o_h = jnp.where(ri == i, o, o_h)
        o_ref[:, vs] = _rms(o_h, gnw_ref[...]) * _silu(gg_ref[:, vs])
    y = x_ref[...] * cw_ref[CONV_W - 1:CONV_W, :]
    for jw in range(CONV_W - 1):
        y = y + cb_ref[jw] * cw_ref[jw:jw + 1, :]
    y = _silu(y)
    beta = _sigmoid(sm[:, SM_DB:SM_DB + GDN_HEADS])
    eg = jnp.exp(-jnp.exp(al_ref[...]) * _softplus(sm[:, SM_DA:SM_DA + GDN_HEADS] + dt_ref[...]))
    dk, dv = GDN_DK, GDN_DV
    for h in range(GDN_HEADS):
        cq = y[:, h * dk:(h + 1) * dk]
        ck = y[:, GDN_WIDTH + h * dk:GDN_WIDTH + (h + 1) * dk]
        v = y[:, 2 * GDN_WIDTH + h * dv:2 * GDN_WIDTH + (h + 1) * dv]
        q_t = (cq * lax.rsqrt(jnp.sum(cq * cq, axis=-1, keepdims=True) + 1e-6) * (dk ** -0.5)).T
        k_t = (ck * lax.rsqrt(jnp.sum(ck * ck, axis=-1, keepdims=True) + 1e-6)).T
        o_h = jnp.zeros((SB, dv), F32)
        for i in range(SB):
            s = eg[i:i + 1, h:h + 1] * sd_ref[i, h]
            kc = k_t[:, i:i + 1]
            delta = (v[i:i + 1] - jnp.sum(kc * s, axis=0, keepdims=True)) * beta[i:i + 1, h:h + 1]
            s = s + kc * delta
            sdo_ref[i, h] = s
            o = jnp.sum(q_t[:, i:i + 1] * s, axis=0, keepdims=True)
            o_h = jnp.where(ri == i, o, o_h)
        vs = slice(GLA_WIDTH + h * dv, GLA_WIDTH + (h + 1) * dv)
        o_ref[:, vs] = _rms(o_h, dnw_ref[...]) * _silu(dz_ref[:, h * dv:(h + 1) * dv])


def _rec_sample(proj_s, state_gla, state_gdn, conv_t, w2, b2, gnw, cw, a_log, dt_bias, dnw, layer):
    nbatch = proj_s.shape[0]
    nblk = nbatch // SB
    row = lambda w, col: pl.BlockSpec((SB, w), lambda i: (i, col // w))
    full = lambda shape: pl.BlockSpec(shape, lambda i: (0,) * len(shape))
    return pl.pallas_call(
        _rec_sample_kernel,
        out_shape=(jax.ShapeDtypeStruct((nbatch, GLA_WIDTH + GDN_WIDTH), F32),
                   jax.ShapeDtypeStruct((nbatch, GLA_HEADS, GLA_DK, GLA_DV), F32),
                   jax.ShapeDtypeStruct((nbatch, GDN_HEADS, GDN_DK, GDN_DV), F32)),
        grid=(nblk,),
        in_specs=[row(GLA_HEADS * GLA_DK, C_GQ), row(GLA_HEADS * GLA_DK, C_GK), row(GLA_WIDTH, C_GV),
                  row(GLA_WIDTH, C_GG), row(3 * GDN_WIDTH, C_DQKV), row(GDN_WIDTH, C_DZ), row(128, C_SM),
                  pl.BlockSpec((SB, GLA_HEADS, GLA_DK, GLA_DV), lambda i: (layer * nblk + i, 0, 0, 0)),
                  pl.BlockSpec((SB, GDN_HEADS, GDN_DK, GDN_DV), lambda i: (layer * nblk + i, 0, 0, 0)),
                  pl.BlockSpec((CONV_W - 1, SB, 3 * GDN_WIDTH), lambda i: (0, i, 0)),
                  full(w2.shape), full(b2.shape), full(gnw.shape), full(cw.shape), full(a_log.shape),
                  full(dt_bias.shape), full(dnw.shape)],
        out_specs=(pl.BlockSpec((SB, GLA_WIDTH + GDN_WIDTH), lambda i: (i, 0)),
                   pl.BlockSpec((SB, GLA_HEADS, GLA_DK, GLA_DV), lambda i: (i, 0, 0, 0)),
                   pl.BlockSpec((SB, GDN_HEADS, GDN_DK, GDN_DV), lambda i: (i, 0, 0, 0))),
        compiler_params=_params(("parallel",)),
        name="rec_sample",
    )(proj_s, proj_s, proj_s, proj_s, proj_s, proj_s, proj_s, state_gla, state_gdn, conv_t,
      w2, b2, gnw, cw, a_log, dt_bias, dnw)


def _reorder_w_in(w):
    d = w.shape[0]
    return jnp.concatenate([
        w[:, 0:2560],
        w[:, 2584:3096],
        w[:, 4136:5672],
        w[:, 3096:3608],
        w[:, 3624:4136],
        w[:, 5672:6184],
        w[:, 2560:2584],
        w[:, 3608:3624],
        w[:, 6184:6192],
        jnp.zeros((d, PROJ_N - 6192), w.dtype)], axis=1)


def _cmp_weight_tile(w_cmp):
    half = NSA_KV_HEADS * HEAD_DIM
    return jnp.concatenate([jnp.broadcast_to(w_cmp[:, 0:1], (CMP_BLOCK, half)),
                            jnp.broadcast_to(w_cmp[:, 1:2], (CMP_BLOCK, half))], axis=1).astype(F32)


def kernel(x_prompt, x_sample, cache_cmp, cache_sel, cache_win, state_gla, state_gdn, state_conv, page_table,
           p_prompt, p_sample, norm_w, ffn_w1, ffn_w3, ffn_w2, w_in, w_out, nsa_w_cmp, t5_bias,
           gla_w_gk2, gla_b_gk, gla_norm_w, gdn_conv_w, gdn_a_log, gdn_dt_bias, gdn_norm_w,
           ple_w_proj, ple_w_gate):
    depth = w_in.shape[0]
    batch, seq, d = x_prompt.shape
    nbatch = x_sample.shape[0]
    n_phys = cache_cmp.shape[1]
    n_pages = page_table.shape[1]
    past = n_pages * PAGE_SIZE
    win_len = cache_win.shape[2]
    rows_p = batch * seq
    kv_shape = (2, NSA_KV_HEADS, HEAD_DIM)

    x = jnp.concatenate([x_prompt.reshape(rows_p, d), x_sample.reshape(nbatch, d)], axis=0)
    pool_cmp = cache_cmp.reshape(depth * n_phys, PAGE_SIZE, KVW)
    pool_sel = cache_sel.reshape(depth * n_phys * (PAGE_SIZE // CMP_BLOCK), CMP_BLOCK, KVW)
    win_all = cache_win.reshape(depth * nbatch, win_len, KVW)
    sgla_all = state_gla.reshape((depth * nbatch,) + state_gla.shape[2:])
    sgdn_all = state_gdn.reshape((depth * nbatch,) + state_gdn.shape[2:])

    pb_c, pb_s, pb_w = _prompt_bias_tables(t5_bias, seq)
    sb_c, sb_s, sb_w = _sample_bias_tables(t5_bias, past, win_len)
    expand = jnp.asarray(np.repeat(np.eye(seq // CMP_BLOCK, dtype=np.float32), CMP_BLOCK, axis=1), BF16)

    outs = {k: [] for k in ("cmp_p", "cmp_s", "sel_p", "sel_s", "win_p", "win_s",
                            "gla_p", "gla_s", "gdn_p", "gdn_s", "conv_p", "conv_s")}
    for l in range(depth):
        nw = norm_w[l].reshape(-1, 1, d)
        w1 = ffn_w1[l].astype(BF16)
        w3 = ffn_w3[l].astype(BF16)
        w2 = ffn_w2[l].astype(BF16)
        x = _ffn(x, nw[0], w1[0], w3[0], w2[0], nw[1])

        proj = _proj(x, nw[2], _reorder_w_in(w_in[l]).astype(BF16))
        proj_s = proj[rows_p:]
        proj_s3 = proj_s.reshape(nbatch, 1, PROJ_N)
        wtile = _cmp_weight_tile(nsa_w_cmp[l])
        w_gk2 = gla_w_gk2[l]
        b_gk = gla_b_gk[l].reshape(1, -1)
        gla_nw = gla_norm_w[l].reshape(1, -1)
        gdn_nw = gdn_norm_w[l].reshape(1, -1)
        conv_w = gdn_conv_w[l]
        a_log = gdn_a_log[l].reshape(1, -1)
        dt_bias = gdn_dt_bias[l].reshape(1, -1)

        cmpkv = _compress_prompt(proj, wtile, rows_p)
        kvs16 = proj[:rows_p, C_KVS:C_KVS + KVW].astype(BF16)
        kvw16 = proj[:rows_p, C_KVW:C_KVW + KVW].astype(BF16)
        o_nsa = _nsa_prompt(proj, cmpkv, kvs16, kvw16, pb_c, pb_s, pb_w, expand, batch, seq)
        o_gla, s_gla_p = _gla_prompt(proj, w_gk2, b_gk, gla_nw, batch, seq)
        o_gdn, s_gdn_p = _gdn_prompt(proj, conv_w, a_log, dt_bias, gdn_nw, batch, seq)

        part, top_idx, win_new = _nsa_sample_a(page_table, proj_s3, pool_cmp, win_all, wtile, sb_c, sb_w, l, n_phys)
        o_nsa_s = _nsa_sample_b(page_table, top_idx.reshape(nbatch, 2 * N_SELECT), proj_s3, part, pool_sel,
                                sb_s, l, n_phys)
        conv_t = jnp.swapaxes(state_conv[l], 0, 1)
        o_rec_s, s_gla_s, s_gdn_s = _rec_sample(proj_s, sgla_all, sgdn_all, conv_t, w_gk2, b_gk, gla_nw,
                                                conv_w, a_log, dt_bias, gdn_nw, l)

        y = jnp.concatenate([jnp.concatenate([o_nsa, o_gla, o_gdn], axis=1),
                             jnp.concatenate([o_nsa_s.reshape(nbatch, NSA_WIDTH), o_rec_s], axis=1)], axis=0)
        x = _outproj(y, x, w_out[l].astype(BF16), nw[3])
        x = _ffn(x, nw[4], w1[1], w3[1], w2[1], nw[5])
        p = jnp.concatenate([p_prompt[l].reshape(rows_p, -1), p_sample[l].reshape(nbatch, -1)], axis=0)
        x = _ple(x, p, nw[6], ple_w_gate[l].astype(BF16), ple_w_proj[l].astype(BF16), nw[7])

        pp = proj[:rows_p]
        outs["cmp_p"].append(pp[:, C_KVC:C_KVC + KVW].reshape((batch, seq) + kv_shape))
        outs["sel_p"].append(pp[:, C_KVS:C_KVS + KVW].reshape((batch, seq) + kv_shape))
        wp = min(WINDOW, seq)
        outs["win_p"].append(pp[:, C_KVW:C_KVW + KVW].reshape((batch, seq) + kv_shape)[:, seq - wp:])
        outs["cmp_s"].append(proj_s[:, C_KVC:C_KVC + KVW].reshape((nbatch, 1) + kv_shape))
        outs["sel_s"].append(proj_s[:, C_KVS:C_KVS + KVW].reshape((nbatch, 1) + kv_shape))
        outs["win_s"].append(win_new.reshape((nbatch, win_len) + kv_shape))
        outs["gla_p"].append(s_gla_p)
        outs["gla_s"].append(s_gla_s)
        outs["gdn_p"].append(s_gdn_p)
        outs["gdn_s"].append(s_gdn_s)
        dqkv_p = pp[:, C_DQKV:C_DQKV + 3 * GDN_WIDTH].reshape(batch, seq, 3 * GDN_WIDTH)
        outs["conv_p"].append(dqkv_p[:, seq - (CONV_W - 1):])
        outs["conv_s"].append(jnp.concatenate(
            [state_conv[l][:, 1:], proj_s[:, None, C_DQKV:C_DQKV + 3 * GDN_WIDTH]], axis=1))

    st = lambda k: jnp.stack(outs[k])
    return (x[:rows_p].reshape(batch, seq, d), x[rows_p:].reshape(nbatch, 1, d),
            st("cmp_p"), st("cmp_s"), st("sel_p"), st("sel_s"), st("win_p"), st("win_s"),
            st("gla_p"), st("gla_s"), st("gdn_p"), st("gdn_s"), st("conv_p"), st("conv_s"))
```

```python
import functools
import math

import numpy as np
import jax
import jax.numpy as jnp
from jax import lax
from jax.experimental import pallas as pl
from jax.experimental.pallas import tpu as pltpu

F32 = jnp.float32
BF16 = jnp.bfloat16
I32 = jnp.int32

D_MODEL = 2048
HEAD_DIM = 128
NSA_HEADS = 8
NSA_KV_HEADS = 2
NSA_GROUP = NSA_HEADS // NSA_KV_HEADS
CMP_BLOCK = 64
N_SELECT = 8
WINDOW = 512
Q_BLOCK = 128
SEL_FORCE = 1.0e4
GLA_HEADS = 4
GLA_DK = 64
GLA_DV = 128
GLA_GATE_RANK = 16
GLA_GATE_NORM = 16.0
GDN_HEADS = 4
GDN_DK = 128
GDN_DV = 128
CHUNK = 64
CONV_W = 4
N_BUCKETS = 32
T5_MAX_DIST = 128
D_FF = 5632
PLE_DIM = 256
RMS_EPS = 1e-6
NEG = -1e30
PAGE_SIZE = 128

NSA_WIDTH = NSA_HEADS * HEAD_DIM
KVW = 2 * NSA_KV_HEADS * HEAD_DIM
KV_ROWS = 2 * NSA_KV_HEADS
CMP_PAD = 64
GDN_WIDTH = GDN_HEADS * GDN_DV
GLA_WIDTH = GLA_HEADS * GLA_DV

C_NQ = 0
C_KVC = 1024
C_KVS = 1536
C_KVW = 2048
C_GQ = 2560
C_GK = 2816
C_DQKV = 3072
C_GV = 4608
C_GG = 5120
C_DZ = 5632
C_SM = 6144
PROJ_N = 6272
SM_GATE, SM_GLR, SM_DB, SM_DA = 0, 24, 40, 44

VMEM_LIMIT = 56 * 1024 * 1024
TM = 640
TF = 512
TN_PROJ = 896
TB = 256


def _sigmoid(x):
    return 1.0 / (1.0 + jnp.exp(-x))


def _silu(x):
    return x * _sigmoid(x)


def _softplus(x):
    return jnp.maximum(x, 0.0) + jnp.log(1.0 + jnp.exp(-jnp.abs(x)))


def _log_sigmoid(x):
    return jnp.minimum(x, 0.0) - jnp.log(1.0 + jnp.exp(-jnp.abs(x)))


def _rms(x, w):
    return x * lax.rsqrt(jnp.mean(x * x, axis=-1, keepdims=True) + RMS_EPS) * w


_NN = (((1,), (0,)), ((), ()))
_NT = (((1,), (1,)), ((), ()))
_TN = (((0,), (0,)), ((), ()))


def _bdot(a, b, dims=_NN):
    return lax.dot_general(a.astype(BF16), b.astype(BF16), dims, preferred_element_type=F32)


def _hdot(a, b, dims=_NN):
    return lax.dot_general(a, b, dims, preferred_element_type=F32, precision=lax.Precision.HIGHEST)


def _params(sem):
    return pltpu.CompilerParams(dimension_semantics=sem, vmem_limit_bytes=VMEM_LIMIT)


def _ffn_kernel(x_ref, nwa_ref, w1_ref, w3_ref, w2_ref, nwb_ref, o_ref, h_ref, acc_ref):
    f = pl.program_id(1)

    @pl.when(f == 0)
    def _():
        h_ref[...] = _rms(x_ref[...], nwa_ref[...]).astype(BF16)
        acc_ref[...] = jnp.zeros_like(acc_ref)

    h = h_ref[...]
    a = jnp.dot(h, w1_ref[...], preferred_element_type=F32)
    b = jnp.dot(h, w3_ref[...], preferred_element_type=F32)
    g = (_silu(a) * b).astype(BF16)
    acc_ref[...] += jnp.dot(g, w2_ref[...], preferred_element_type=F32)

    @pl.when(f == pl.num_programs(1) - 1)
    def _():
        o_ref[...] = x_ref[...] + 0.5 * _rms(acc_ref[...], nwb_ref[...])


def _ffn(x, nwa, w1, w3, w2, nwb):
    m, d = x.shape
    dff = w1.shape[1]
    row = lambda i, f: (i, 0)
    return pl.pallas_call(
        _ffn_kernel,
        out_shape=jax.ShapeDtypeStruct((m, d), F32),
        grid=(m // TM, dff // TF),
        in_specs=[pl.BlockSpec((TM, d), row),
                  pl.BlockSpec((1, d), lambda i, f: (0, 0)),
                  pl.BlockSpec((d, TF), lambda i, f: (0, f)),
                  pl.BlockSpec((d, TF), lambda i, f: (0, f)),
                  pl.BlockSpec((TF, d), lambda i, f: (f, 0)),
                  pl.BlockSpec((1, d), lambda i, f: (0, 0))],
        out_specs=pl.BlockSpec((TM, d), row),
        scratch_shapes=[pltpu.VMEM((TM, d), BF16), pltpu.VMEM((TM, d), F32)],
        compiler_params=_params(("parallel", "arbitrary")),
        name="ffn",
    )(x, nwa, w1, w3, w2, nwb)


def _proj_kernel(x_ref, nw_ref, w_ref, o_ref, h_ref):
    @pl.when(pl.program_id(1) == 0)
    def _():
        h_ref[...] = _rms(x_ref[...], nw_ref[...]).astype(BF16)

    o_ref[...] = jnp.dot(h_ref[...], w_ref[...], preferred_element_type=F32)


def _proj(x, nw, w):
    m, d = x.shape
    n = w.shape[1]
    return pl.pallas_call(
        _proj_kernel,
        out_shape=jax.ShapeDtypeStruct((m, n), F32),
        grid=(m // TM, n // TN_PROJ),
        in_specs=[pl.BlockSpec((TM, d), lambda i, j: (i, 0)),
                  pl.BlockSpec((1, d), lambda i, j: (0, 0)),
                  pl.BlockSpec((d, TN_PROJ), lambda i, j: (0, j))],
        out_specs=pl.BlockSpec((TM, TN_PROJ), lambda i, j: (i, j)),
        scratch_shapes=[pltpu.VMEM((TM, d), BF16)],
        compiler_params=_params(("parallel", "arbitrary")),
        name="proj",
    )(x, nw, w)


def _outproj_kernel(y_ref, x_ref, w_ref, nw_ref, o_ref):
    z = jnp.dot(y_ref[...].astype(BF16), w_ref[...], preferred_element_type=F32)
    o_ref[...] = x_ref[...] + _rms(z, nw_ref[...])


def _outproj(y, x, w, nw):
    m, d = x.shape
    k = y.shape[1]
    return pl.pallas_call(
        _outproj_kernel,
        out_shape=jax.ShapeDtypeStruct((m, d), F32),
        grid=(m // TM,),
        in_specs=[pl.BlockSpec((TM, k), lambda i: (i, 0)),
                  pl.BlockSpec((TM, d), lambda i: (i, 0)),
                  pl.BlockSpec((k, d), lambda i: (0, 0)),
                  pl.BlockSpec((1, d), lambda i: (0, 0))],
        out_specs=pl.BlockSpec((TM, d), lambda i: (i, 0)),
        compiler_params=_params(("parallel",)),
        name="outproj",
    )(y, x, w, nw)


def _ple_kernel(x_ref, p_ref, nwa_ref, wg_ref, wp_ref, nwb_ref, o_ref):
    x = x_ref[...]
    gate = _sigmoid(jnp.dot(_rms(x, nwa_ref[...]).astype(BF16), wg_ref[...], preferred_element_type=F32))
    pp = jnp.dot(p_ref[...].astype(BF16), wp_ref[...], preferred_element_type=F32)
    o_ref[...] = x + _rms(gate * pp, nwb_ref[...])


def _ple(x, p, nwa, wg, wp, nwb):
    m, d = x.shape
    pd = p.shape[1]
    return pl.pallas_call(
        _ple_kernel,
        out_shape=jax.ShapeDtypeStruct((m, d), F32),
        grid=(m // TM,),
        in_specs=[pl.BlockSpec((TM, d), lambda i: (i, 0)),
                  pl.BlockSpec((TM, pd), lambda i: (i, 0)),
                  pl.BlockSpec((1, d), lambda i: (0, 0)),
                  pl.BlockSpec((d, d), lambda i: (0, 0)),
                  pl.BlockSpec((pd, d), lambda i: (0, 0)),
                  pl.BlockSpec((1, d), lambda i: (0, 0))],
        out_specs=pl.BlockSpec((TM, d), lambda i: (i, 0)),
        compiler_params=_params(("parallel",)),
        name="ple",
    )(x, p, nwa, wg, wp, nwb)


def _t5_bucket_np(dist):
    n = np.maximum(dist, 0)
    exact = N_BUCKETS // 2
    val = (np.log(np.maximum(n, 1).astype(np.float32) / np.float32(exact))
           / np.float32(math.log(T5_MAX_DIST / exact)) * np.float32(N_BUCKETS - exact))
    large = exact + val.astype(np.int32)
    return np.where(n < exact, n, np.minimum(large, N_BUCKETS - 1)).astype(np.int32)


def _bias_table(t5_bias, dist, valid):
    bucket = jnp.asarray(np.where(valid, _t5_bucket_np(dist), -1).astype(np.int32))[None]
    hshape = (t5_bias.shape[1],) + (1,) * dist.ndim
    out = jnp.full((t5_bias.shape[1],) + dist.shape, NEG, F32)
    for k in range(N_BUCKETS):
        out = jnp.where(bucket == k, t5_bias[k].reshape(hshape), out)
    return out


def _prompt_bias_tables(t5_bias, seq):
    nb = seq // CMP_BLOCK
    qpos = np.arange(seq)[:, None]
    d_c = qpos - (np.arange(nb) * CMP_BLOCK + CMP_BLOCK - 1)[None, :]
    bias_c = _bias_table(t5_bias, d_c, d_c >= 0)
    i = np.arange(Q_BLOCK)[:, None]
    jj = np.arange(Q_BLOCK)[None, :]
    d_s = np.stack([Q_BLOCK * dl + i - jj for dl in range(3)])
    bias_s = jnp.transpose(_bias_table(t5_bias, d_s, d_s >= 0), (1, 0, 2, 3))
    nwc = WINDOW // Q_BLOCK + 1
    d_w = np.stack([i + WINDOW - Q_BLOCK * cw - jj for cw in range(nwc)])
    bias_w = jnp.transpose(_bias_table(t5_bias, d_w, (d_w >= 0) & (d_w < WINDOW)), (1, 0, 2, 3))
    return bias_c, bias_s, bias_w


def _sample_bias_tables(t5_bias, past, win_len):
    nb_past = past // CMP_BLOCK
    head_g = (np.arange(NSA_HEADS) // NSA_GROUP)[:, None]

    def per_head(dist, valid):
        lane_cg = np.arange(dist.shape[-1]) % KV_ROWS
        tbl = _bias_table(t5_bias, dist, valid)
        own = jnp.asarray(lane_cg[None] == head_g)
        return tbl, own

    lane = np.arange(CMP_PAD * KV_ROWS)
    n = lane // KV_ROWS
    d_c = past - (n * CMP_BLOCK + CMP_BLOCK - 1)
    tbl, own = per_head(d_c, (d_c >= 0) & (n <= nb_past))
    bias_c = jnp.where(own, tbl, NEG)
    lane = np.arange(CMP_BLOCK * KV_ROWS)
    blk = np.arange(nb_past + 1)[:, None]
    d_s = past - (blk * CMP_BLOCK + (lane // KV_ROWS)[None, :])
    tbl, own = per_head(d_s, d_s >= 0)
    bias_s = jnp.transpose(jnp.where(own[:, None, :], tbl, NEG), (1, 0, 2))
    lane = np.arange(win_len * KV_ROWS)
    wpos = past - win_len + 1 + lane // KV_ROWS
    d_w = past - wpos
    tbl, own = per_head(d_w, (d_w >= 0) & (d_w < WINDOW) & (wpos >= 0))
    bias_w = jnp.where(own, tbl, NEG)
    return bias_c, bias_s, bias_w


def _compress_kernel(x_ref, w_ref, o_ref):
    r = x_ref.shape[0] // CMP_BLOCK
    x = x_ref[...].reshape(r, CMP_BLOCK, KVW)
    o_ref[...] = jnp.sum(x * w_ref[...][None], axis=1)


def _compress_prompt(proj, wtile, rows):
    rb = 512
    return pl.pallas_call(
        _compress_kernel,
        out_shape=jax.ShapeDtypeStruct((rows // CMP_BLOCK, KVW), F32),
        grid=(rows // rb,),
        in_specs=[pl.BlockSpec((rb, KVW), lambda i: (i, C_KVC // KVW)),
                  pl.BlockSpec((CMP_BLOCK, KVW), lambda i: (0, 0))],
        out_specs=pl.BlockSpec((rb // CMP_BLOCK, KVW), lambda i: (i, 0)),
        compiler_params=_params(("parallel",)),
        name="compress_prompt",
    )(proj, wtile)


def _nsa_prompt_kernel(q_ref, sm_ref, cmp_ref, kvs_ref, kvw_ref, bc_ref, bs_ref, bw_ref, e_ref, o_ref,
                       mask_ref, s_ref, mx_ref, l_ref, acc_ref):
    j = pl.program_id(1)
    nb = cmp_ref.shape[0]
    nqb = mask_ref.shape[1]
    hg, qb, dh = NSA_GROUP, Q_BLOCK, HEAD_DIM
    rows = hg * qb
    nwc = bw_ref.shape[0]
    gates = _sigmoid(sm_ref[:, SM_GATE:SM_GATE + 3 * NSA_HEADS])
    qi = lax.broadcasted_iota(I32, (qb, nb), 0)
    ni = lax.broadcasted_iota(I32, (qb, nb), 1)
    cur = (qb // CMP_BLOCK) * j + qi // CMP_BLOCK
    forced = (ni == cur) | (ni == cur - 1) | (ni == 0)
    started = ni <= cur
    kcols = [slice(g * dh, (g + 1) * dh) for g in range(NSA_KV_HEADS)]
    vcols = [slice((NSA_KV_HEADS + g) * dh, (NSA_KV_HEADS + g + 1) * dh) for g in range(NSA_KV_HEADS)]

    q16, o_c, score = [], [], []
    for g in range(NSA_KV_HEADS):
        q = jnp.concatenate([q_ref[:, (g * hg + h) * dh:(g * hg + h + 1) * dh] for h in range(hg)], axis=0)
        q16.append((q * (dh ** -0.5)).astype(BF16))
        bias = bc_ref[g * hg:(g + 1) * hg].reshape(rows, nb)
        valid = bias > -1e29
        s = jnp.where(valid, _bdot(q16[g], cmp_ref[:, kcols[g]], _NT) + bias, NEG)
        p = jnp.where(valid, jnp.exp(s - jnp.max(s, axis=-1, keepdims=True)), 0.0)
        p = p / jnp.maximum(jnp.sum(p, axis=-1, keepdims=True), 1e-30)
        o_c.append(_bdot(p, cmp_ref[:, vcols[g]]))
        imp = p[0:qb]
        for h in range(1, hg):
            imp = imp + p[h * qb:(h + 1) * qb]
        score.append(jnp.where(started, jnp.where(forced, SEL_FORCE, imp), -1.0))

    sel = [jnp.zeros((qb, nb), F32) for _ in range(NSA_KV_HEADS)]
    for _ in range(min(N_SELECT, nb)):
        for g in range(NSA_KV_HEADS):
            hit = ni == jnp.argmax(score[g], axis=-1, keepdims=True).astype(I32)
            sel[g] = jnp.where(hit & started, 1.0, sel[g])
            score[g] = jnp.where(hit, -3e38, score[g])
    for g in range(NSA_KV_HEADS):
        selexp = jnp.dot(sel[g].astype(BF16), e_ref[...], preferred_element_type=F32)
        for c in range(nqb):
            mask_ref[g, c] = (selexp[:, c * qb:(c + 1) * qb] - 1.0) * (-NEG)

    groups = range(NSA_KV_HEADS)

    def two_pass(lo, hi, scores, values):
        mx_ref[...] = jnp.full(mx_ref.shape, NEG, F32)
        l_ref[...] = jnp.zeros(l_ref.shape, F32)
        acc_ref[...] = jnp.zeros(acc_ref.shape, F32)

        def pass1(c, carry):
            for g in groups:
                s_ = scores(g, c)
                s_ref[g, c] = s_
                mx_ref[g] = jnp.maximum(mx_ref[g], s_)
            return carry

        lax.fori_loop(lo, hi, pass1, 0)
        for g in groups:
            mx_ref[g] = jnp.broadcast_to(jnp.max(mx_ref[g], axis=-1, keepdims=True), mx_ref.shape[1:])

        def pass2(c, carry):
            for g in groups:
                p_ = jnp.exp(s_ref[g, c] - mx_ref[g])
                l_ref[g] += p_
                acc_ref[g] += _bdot(p_, values(g, c))
            return carry

        lax.fori_loop(lo, hi, pass2, 0)
        return [acc_ref[g] / jnp.maximum(jnp.sum(l_ref[g], axis=-1, keepdims=True), 1e-30) for g in groups]

    def head_rows(g):
        return slice(g * hg, (g + 1) * hg)

    def sel_rows(c):
        return pl.ds(pl.multiple_of(c * qb, qb), qb)

    def sel_scores(g, c):
        bt = bs_ref[jnp.minimum(j - c, 2), head_rows(g)].reshape(rows, qb)
        return (_bdot(q16[g], kvs_ref[sel_rows(c), kcols[g]], _NT) + bt
                + jnp.concatenate([mask_ref[g, c]] * hg, axis=0))

    o_s = two_pass(0, j + 1, sel_scores, lambda g, c: kvs_ref[sel_rows(c), vcols[g]])

    def win_rows(cw):
        return pl.ds(pl.multiple_of((j - (nwc - 1) + cw) * qb, qb), qb)

    def win_scores(g, cw):
        return _bdot(q16[g], kvw_ref[win_rows(cw), kcols[g]], _NT) + bw_ref[cw, head_rows(g)].reshape(rows, qb)

    o_w = two_pass(jnp.maximum(0, nwc - 1 - j), nwc, win_scores, lambda g, cw: kvw_ref[win_rows(cw), vcols[g]])

    for g in groups:
        for h in range(hg):
            c0 = SM_GATE + (g * hg + h) * 3
            rs = slice(h * qb, (h + 1) * qb)
            o_ref[:, (g * hg + h) * dh:(g * hg + h + 1) * dh] = (
                o_c[g][rs] * gates[:, c0:c0 + 1] + o_s[g][rs] * gates[:, c0 + 1:c0 + 2]
                + o_w[g][rs] * gates[:, c0 + 2:c0 + 3])


def _nsa_prompt(proj, cmpkv, kvs16, kvw16, bias_c, bias_s, bias_w, expand, batch, seq):
    nqb = seq // Q_BLOCK
    nb = seq // CMP_BLOCK
    rows = NSA_GROUP * Q_BLOCK
    full = lambda shape: pl.BlockSpec(shape, lambda b, j: (0,) * len(shape))
    return pl.pallas_call(
        _nsa_prompt_kernel,
        out_shape=jax.ShapeDtypeStruct((batch * seq, NSA_WIDTH), F32),
        grid=(batch, nqb),
        in_specs=[pl.BlockSpec((Q_BLOCK, NSA_WIDTH), lambda b, j: (b * nqb + j, 0)),
                  pl.BlockSpec((Q_BLOCK, 128), lambda b, j: (b * nqb + j, C_SM // 128)),
                  pl.BlockSpec((nb, KVW), lambda b, j: (b, 0)),
                  pl.BlockSpec((seq, KVW), lambda b, j: (b, 0)),
                  pl.BlockSpec((seq, KVW), lambda b, j: (b, 0)),
                  pl.BlockSpec((NSA_HEADS, Q_BLOCK, nb), lambda b, j: (0, j, 0)),
                  full(bias_s.shape), full(bias_w.shape), full(expand.shape)],
        out_specs=pl.BlockSpec((Q_BLOCK, NSA_WIDTH), lambda b, j: (b * nqb + j, 0)),
        scratch_shapes=[pltpu.VMEM((NSA_KV_HEADS, nqb, Q_BLOCK, Q_BLOCK), F32),
                        pltpu.VMEM((NSA_KV_HEADS, nqb, rows, Q_BLOCK), F32),
                        pltpu.VMEM((NSA_KV_HEADS, rows, Q_BLOCK), F32), pltpu.VMEM((NSA_KV_HEADS, rows, Q_BLOCK), F32),
                        pltpu.VMEM((NSA_KV_HEADS, rows, HEAD_DIM), F32)],
        compiler_params=_params(("parallel", "arbitrary")),
        name="nsa_prompt",
    )(proj, proj, cmpkv, kvs16, kvw16, bias_c, bias_s, bias_w, expand)


def _tri_masks():
    r = lax.broadcasted_iota(I32, (CHUNK, CHUNK), 0)
    c = lax.broadcasted_iota(I32, (CHUNK, CHUNK), 1)
    return r, c


def _gla_prompt_kernel(q_ref, k_ref, v_ref, gg_ref, sm_ref, w2_ref, b2_ref, nw_ref, o_ref, so_ref, s_ref):
    t = pl.program_id(1)

    @pl.when(t == 0)
    def _():
        s_ref[...] = jnp.zeros_like(s_ref)

    r, c = _tri_masks()
    lower = r >= c
    tril = jnp.where(lower, 1.0, 0.0).astype(F32)
    log_a = _log_sigmoid(_bdot(sm_ref[:, SM_GLR:SM_GLR + GLA_GATE_RANK], w2_ref[...]) + b2_ref[...]) / GLA_GATE_NORM
    dk, dv = GLA_DK, GLA_DV
    for ch in range(q_ref.shape[0] // CHUNK):
        rs = slice(ch * CHUNK, (ch + 1) * CHUNK)
        for h in range(GLA_HEADS):
            a = log_a[rs, h * dk:(h + 1) * dk]
            b = _hdot(tril, a)
            q = q_ref[rs, h * dk:(h + 1) * dk] * (dk ** -0.5)
            k = k_ref[rs, h * dk:(h + 1) * dk]
            v = v_ref[rs, h * dv:(h + 1) * dv]
            qe = q * jnp.exp(b)
            att = jnp.where(lower, _bdot(qe, k * jnp.exp(-b), _NT), 0.0)
            s = s_ref[h]
            o = _bdot(qe, s) + _bdot(att, v)
            b_last = b[CHUNK - 1:CHUNK, :]
            b_last_col = b.T[:, CHUNK - 1:CHUNK]
            s_ref[h] = jnp.exp(b_last_col) * s + _bdot(k * jnp.exp(b_last - b), v, _TN)
            y = _rms(o, nw_ref[...]) * _silu(gg_ref[rs, h * dv:(h + 1) * dv])
            o_ref[rs, h * dv:(h + 1) * dv] = y
    so_ref[0] = s_ref[...]


def _gla_prompt(proj, w2, b2, nw, batch, seq):
    nt = seq // TB
    row = lambda w, col: pl.BlockSpec((TB, w), lambda b, t: (b * nt + t, col // w))
    full = lambda shape: pl.BlockSpec(shape, lambda b, t: (0,) * len(shape))
    return pl.pallas_call(
        _gla_prompt_kernel,
        out_shape=(jax.ShapeDtypeStruct((batch * seq, GLA_WIDTH), F32),
                   jax.ShapeDtypeStruct((batch, GLA_HEADS, GLA_DK, GLA_DV), F32)),
        grid=(batch, nt),
        in_specs=[row(GLA_HEADS * GLA_DK, C_GQ), row(GLA_HEADS * GLA_DK, C_GK), row(GLA_WIDTH, C_GV),
                  row(GLA_WIDTH, C_GG), row(128, C_SM), full(w2.shape), full(b2.shape), full(nw.shape)],
        out_specs=(pl.BlockSpec((TB, GLA_WIDTH), lambda b, t: (b * nt + t, 0)),
                   pl.BlockSpec((1, GLA_HEADS, GLA_DK, GLA_DV), lambda b, t: (b, 0, 0, 0))),
        scratch_shapes=[pltpu.VMEM((GLA_HEADS, GLA_DK, GLA_DV), F32)],
        compiler_params=_params(("parallel", "arbitrary")),
        name="gla_prompt",
    )(proj, proj, proj, proj, proj, w2, b2, nw)


_BNN = (((2,), (1,)), ((0,), (0,)))
_BNT = (((2,), (2,)), ((0,), (0,)))


def _hdot_b(a, b, dims=_BNN):
    return lax.dot_general(a, b, dims, preferred_element_type=F32, precision=lax.Precision.HIGHEST)


def _unit_lower_inverse(m, r, c):
    eye = jnp.where(r == c, 1.0, 0.0).astype(F32)[None]
    base = 8
    m8 = jnp.where(((r // base) == (c // base))[None], m, 0.0)
    m2 = _hdot_b(m8, m8)
    m4 = _hdot_b(m2, m2)
    t = _hdot_b(_hdot_b(eye - m8, eye + m2), eye + m4)
    s = base
    while s < CHUNK:
        off = ((r // (2 * s)) == (c // (2 * s))) & ((r // s) != (c // s))
        t = t - _hdot_b(t, _hdot_b(jnp.where(off[None], m, 0.0), t))
        s *= 2
    return t


def _gdn_prompt_kernel(x_ref, dz_ref, sm_ref, cw_ref, al_ref, dt_ref, nw_ref, o_ref, so_ref, s_ref, tail_ref):
    t = pl.program_id(1)

    @pl.when(t == 0)
    def _():
        s_ref[...] = jnp.zeros_like(s_ref)
        tail_ref[...] = jnp.zeros_like(tail_ref)

    tb = x_ref.shape[0]
    x = x_ref[...]
    xc = jnp.concatenate([tail_ref[...], x], axis=0)
    off = 8 - (CONV_W - 1)
    y = xc[off:off + tb] * cw_ref[0:1, :]
    for jw in range(1, CONV_W):
        y = y + xc[off + jw:off + jw + tb] * cw_ref[jw:jw + 1, :]
    y = _silu(y)
    tail_ref[...] = x[tb - 8:tb]

    beta_all = _sigmoid(sm_ref[:, SM_DB:SM_DB + GDN_HEADS])
    g_all = -jnp.exp(al_ref[...]) * _softplus(sm_ref[:, SM_DA:SM_DA + GDN_HEADS] + dt_ref[...])
    r, c = _tri_masks()
    lower = r >= c
    strict = r > c
    tril = jnp.where(lower, 1.0, 0.0).astype(F32)
    dk, dv = GDN_DK, GDN_DV
    nch = tb // CHUNK
    pairs = [(ch, h) for ch in range(nch) for h in range(GDN_HEADS)]
    rs = lambda ch: slice(ch * CHUNK, (ch + 1) * CHUNK)
    stack = lambda fn: jnp.stack([fn(ch, h) for ch, h in pairs], axis=0)

    cq = stack(lambda ch, h: y[rs(ch), h * dk:(h + 1) * dk])
    ck = stack(lambda ch, h: y[rs(ch), GDN_WIDTH + h * dk:GDN_WIDTH + (h + 1) * dk])
    v = stack(lambda ch, h: y[rs(ch), 2 * GDN_WIDTH + h * dv:2 * GDN_WIDTH + (h + 1) * dv])
    q = cq * lax.rsqrt(jnp.sum(cq * cq, axis=-1, keepdims=True) + 1e-6) * (dk ** -0.5)
    k = ck * lax.rsqrt(jnp.sum(ck * ck, axis=-1, keepdims=True) + 1e-6)
    beta = stack(lambda ch, h: jnp.broadcast_to(beta_all[rs(ch), h:h + 1], (CHUNK, dk)))
    g_cols = jnp.concatenate([g_all[rs(ch)] for ch in range(nch)], axis=1)
    gam_cols = _hdot(tril, g_cols)
    gam_rows = gam_cols.T
    gam = jnp.stack([jnp.broadcast_to(gam_cols[:, p:p + 1], (CHUNK, dk)) for p in range(len(pairs))], axis=0)
    decay = jnp.exp(jnp.where(lower[None], gam[:, :, 0:CHUNK] - gam_rows[:, None, :], NEG))
    kb = k * beta
    m = jnp.where(strict[None], _hdot_b(kb, k, _BNT) * decay, 0.0)
    tinv = _unit_lower_inverse(m, r, c)
    eg = jnp.exp(gam)
    u = _hdot_b(tinv, v * beta)
    w = _hdot_b(tinv, kb * eg)
    att = lax.dot_general(q.astype(BF16), k.astype(BF16), _BNT, preferred_element_type=F32) * decay
    qe = q * eg
    g_last = gam[:, CHUNK - 1:CHUNK, :]
    kd = k * jnp.exp(g_last - gam)
    eg_last = jnp.exp(g_last)

    for p, (ch, h) in enumerate(pairs):
        s = s_ref[h]
        v_new = u[p] - _bdot(w[p], s)
        o = _bdot(qe[p], s) + _bdot(att[p], v_new)
        s_ref[h] = eg_last[p] * s + _bdot(kd[p], v_new, _TN)
        o_ref[rs(ch), h * dv:(h + 1) * dv] = _rms(o, nw_ref[...]) * _silu(dz_ref[rs(ch), h * dv:(h + 1) * dv])
    so_ref[0] = s_ref[...]


def _gdn_prompt(proj, cw, a_log, dt_bias, nw, batch, seq):
    nt = seq // TB
    row = lambda w, col: pl.BlockSpec((TB, w), lambda b, t: (b * nt + t, col // w))
    full = lambda shape: pl.BlockSpec(shape, lambda b, t: (0,) * len(shape))
    return pl.pallas_call(
        _gdn_prompt_kernel,
        out_shape=(jax.ShapeDtypeStruct((batch * seq, GDN_WIDTH), F32),
                   jax.ShapeDtypeStruct((batch, GDN_HEADS, GDN_DK, GDN_DV), F32)),
        grid=(batch, nt),
        in_specs=[row(3 * GDN_WIDTH, C_DQKV), row(GDN_WIDTH, C_DZ), row(128, C_SM),
                  full(cw.shape), full(a_log.shape), full(dt_bias.shape), full(nw.shape)],
        out_specs=(pl.BlockSpec((TB, GDN_WIDTH), lambda b, t: (b * nt + t, 0)),
                   pl.BlockSpec((1, GDN_HEADS, GDN_DK, GDN_DV), lambda b, t: (b, 0, 0, 0))),
        scratch_shapes=[pltpu.VMEM((GDN_HEADS, GDN_DK, GDN_DV), F32), pltpu.VMEM((8, 3 * GDN_WIDTH), F32)],
        compiler_params=_params(("parallel", "arbitrary")),
        name="gdn_prompt",
    )(proj, proj, proj, cw, a_log, dt_bias, nw)


def _masked_softmax_rows(s):
    valid = s > -1e29
    p = jnp.where(valid, jnp.exp(s - jnp.max(s, axis=-1, keepdims=True)), 0.0)
    return p / jnp.maximum(jnp.sum(p, axis=-1, keepdims=True), 1e-30)


def _head_rows(q_ref, lo, hi):
    dh = HEAD_DIM
    q = jnp.concatenate([q_ref[0, :, h * dh:(h + 1) * dh] for h in range(lo, hi)], axis=0)
    return (q * (dh ** -0.5)).astype(BF16)


def _kv_rows(kv_ref):
    dh = HEAD_DIM
    return jnp.concatenate([kv_ref[0, :, r * dh:(r + 1) * dh] for r in range(KV_ROWS)], axis=0)


def _value_weights(p):
    return pltpu.roll(p, NSA_KV_HEADS, 1)


def _nsa_sample_a_kernel(n_pages, nb_past, pt_ref, q_ref, kvc_ref, kvw_ref, sm_ref, *rest):
    page_refs = rest[:n_pages]
    win_ref, w4_ref, bc_ref, bw_ref, part_ref, idx_ref, wo_ref, cmp_ref = rest[n_pages:]
    dh = HEAD_DIM
    blk_rows = CMP_BLOCK * KV_ROWS
    bpp = PAGE_SIZE // CMP_BLOCK
    w4 = w4_ref[...]
    for p in range(n_pages):
        x = page_refs[p][...]
        sums = []
        for half in range(bpp):
            pr = x[half * blk_rows:(half + 1) * blk_rows] * w4
            s8 = jnp.sum(pr.reshape(blk_rows // 8, 8, dh), axis=0)
            sums.append(s8[0:KV_ROWS] + s8[KV_ROWS:2 * KV_ROWS])
        cmp_ref[p * bpp * KV_ROWS:(p + 1) * bpp * KV_ROWS, :] = jnp.concatenate(sums, axis=0)
    r0 = nb_past * KV_ROWS
    cmp_ref[r0:r0 + KV_ROWS, :] = _kv_rows(kvc_ref) * w4[0:KV_ROWS]
    cmp_ref[r0 + KV_ROWS:, :] = jnp.zeros((cmp_ref.shape[0] - r0 - KV_ROWS, dh), F32)
    cm16 = cmp_ref[...].astype(BF16)

    q16 = _head_rows(q_ref, 0, NSA_HEADS)
    p = _masked_softmax_rows(_bdot(q16, cm16, _NT) + bc_ref[...])
    o_c = _bdot(_value_weights(p), cm16)

    lanes = cmp_ref.shape[0]
    imp = jnp.concatenate([jnp.sum(p[g * NSA_GROUP:(g + 1) * NSA_GROUP], axis=0, keepdims=True)
                           for g in range(NSA_KV_HEADS)], axis=0)
    lane = lax.broadcasted_iota(I32, (NSA_KV_HEADS, lanes), 1)
    gi = lax.broadcasted_iota(I32, (NSA_KV_HEADS, lanes), 0)
    n = lane // KV_ROWS
    cur = nb_past
    cand = ((lane % KV_ROWS) == gi) & (n <= cur)
    forced = (n == cur) | (n == cur - 1) | (n == 0)
    score = jnp.where(cand, jnp.where(forced, SEL_FORCE, imp), -3e38)
    li = lax.broadcasted_iota(I32, (NSA_KV_HEADS, N_SELECT), 1)
    top = jnp.zeros((NSA_KV_HEADS, N_SELECT), I32)
    for r in range(N_SELECT):
        a = jnp.argmax(score, axis=-1, keepdims=True).astype(I32)
        top = jnp.where(li == r, a // KV_ROWS, top)
        score = jnp.where(lane == a, -3e38, score)
    idx_ref[0] = top

    wl = win_ref.shape[0]
    wo_ref[...] = pltpu.roll(win_ref[...], wl - KV_ROWS, 0)
    wo_ref[wl - KV_ROWS:wl, :] = _kv_rows(kvw_ref)
    w16 = wo_ref[...].astype(BF16)
    pw = _masked_softmax_rows(_bdot(q16, w16, _NT) + bw_ref[...])
    o_w = _bdot(_value_weights(pw), w16)

    gates = _sigmoid(sm_ref[0, :, SM_GATE:SM_GATE + 3 * NSA_HEADS])
    for h in range(NSA_HEADS):
        c0 = 3 * h
        part_ref[0, :, h * dh:(h + 1) * dh] = (
            o_c[h:h + 1] * gates[:, c0:c0 + 1] + o_w[h:h + 1] * gates[:, c0 + 2:c0 + 3])


def _nsa_sample_a(page_table, proj_s3, pool_cmp, win_rows, w4, bias_c, bias_w, layer, n_phys):
    nbatch, n_pages = page_table.shape
    nb_past = n_pages * PAGE_SIZE // CMP_BLOCK
    wl = bias_w.shape[1]
    page_rows = PAGE_SIZE * KV_ROWS
    col = lambda w, c: pl.BlockSpec((1, 1, w), lambda b, pt: (b, 0, c // w))
    full = lambda shape: pl.BlockSpec(shape, lambda b, pt: (0,) * len(shape))
    page = lambda p: pl.BlockSpec((page_rows, HEAD_DIM), lambda b, pt: (layer * n_phys + pt[b, p], 0))
    gs = pltpu.PrefetchScalarGridSpec(
        num_scalar_prefetch=1, grid=(nbatch,),
        in_specs=[col(NSA_WIDTH, C_NQ), col(KVW, C_KVC), col(KVW, C_KVW), col(128, C_SM)]
        + [page(p) for p in range(n_pages)]
        + [pl.BlockSpec((wl, HEAD_DIM), lambda b, pt: (layer * nbatch + b, 0)),
           full(w4.shape), full(bias_c.shape), full(bias_w.shape)],
        out_specs=(pl.BlockSpec((1, 1, NSA_WIDTH), lambda b, pt: (b, 0, 0)),
                   pl.BlockSpec((1, NSA_KV_HEADS, N_SELECT), lambda b, pt: (b, 0, 0)),
                   pl.BlockSpec((wl, HEAD_DIM), lambda b, pt: (b, 0))),
        scratch_shapes=[pltpu.VMEM((CMP_PAD * KV_ROWS, HEAD_DIM), F32)])
    return pl.pallas_call(
        functools.partial(_nsa_sample_a_kernel, n_pages, nb_past),
        out_shape=(jax.ShapeDtypeStruct((nbatch, 1, NSA_WIDTH), F32),
                   jax.ShapeDtypeStruct((nbatch, NSA_KV_HEADS, N_SELECT), I32),
                   jax.ShapeDtypeStruct((nbatch * wl, HEAD_DIM), F32)),
        grid_spec=gs,
        compiler_params=_params(("arbitrary",)),
        name="nsa_sample_a",
    )(page_table, proj_s3, proj_s3, proj_s3, proj_s3, *([pool_cmp] * n_pages), win_rows, w4, bias_c, bias_w)


def _nsa_sample_b_kernel(nb_past, pt_ref, ix_ref, q_ref, kvs_ref, sm_ref, part_ref, *rest):
    nsel = NSA_KV_HEADS * N_SELECT
    blk_refs = rest[:nsel]
    bs_ref, o_ref = rest[nsel:]
    b = pl.program_id(0)
    hg, dh = NSA_GROUP, HEAD_DIM
    blk_rows = CMP_BLOCK * KV_ROWS
    gates = _sigmoid(sm_ref[0, :, SM_GATE:SM_GATE + 3 * NSA_HEADS])
    new_blk = jnp.concatenate([_kv_rows(kvs_ref), jnp.zeros((blk_rows - KV_ROWS, dh), F32)], axis=0)
    for g in range(NSA_KV_HEADS):
        q16 = _head_rows(q_ref, g * hg, (g + 1) * hg)
        scores, keys = [], []
        for i in range(N_SELECT):
            idx = ix_ref[b, g * N_SELECT + i]
            k16 = jnp.where(idx >= nb_past, new_blk, blk_refs[g * N_SELECT + i][...]).astype(BF16)
            bt = bs_ref[jnp.clip(idx, 0, nb_past), g * hg:(g + 1) * hg, :]
            scores.append(_bdot(q16, k16, _NT) + bt)
            keys.append(k16)
        p = _masked_softmax_rows(jnp.concatenate(scores, axis=1))
        o_s = _bdot(_value_weights(p), jnp.concatenate(keys, axis=0))
        for h in range(hg):
            c0 = (g * hg + h) * 3
            cs = slice((g * hg + h) * dh, (g * hg + h + 1) * dh)
            o_ref[0, :, cs] = part_ref[0, :, cs] + o_s[h:h + 1] * gates[:, c0 + 1:c0 + 2]


def _nsa_sample_b(page_table, top_idx, proj_s3, part, pool_sel, bias_s, layer, n_phys):
    nbatch, n_pages = page_table.shape
    nb_past = n_pages * PAGE_SIZE // CMP_BLOCK
    bpp = PAGE_SIZE // CMP_BLOCK
    nsel = NSA_KV_HEADS * N_SELECT
    col = lambda w, c: pl.BlockSpec((1, 1, w), lambda b, pt, ix: (b, 0, c // w))

    def blk(gi):
        def imap(b, pt, ix):
            ip = jnp.clip(ix[b, gi], 0, nb_past - 1)
            return ((layer * n_phys + pt[b, ip // bpp]) * bpp + ip % bpp, 0)
        return pl.BlockSpec((CMP_BLOCK * KV_ROWS, HEAD_DIM), imap)

    gs = pltpu.PrefetchScalarGridSpec(
        num_scalar_prefetch=2, grid=(nbatch,),
        in_specs=[col(NSA_WIDTH, C_NQ), col(KVW, C_KVS), col(128, C_SM),
                  pl.BlockSpec((1, 1, NSA_WIDTH), lambda b, pt, ix: (b, 0, 0))]
        + [blk(gi) for gi in range(nsel)]
        + [pl.BlockSpec(bias_s.shape, lambda b, pt, ix: (0, 0, 0))],
        out_specs=pl.BlockSpec((1, 1, NSA_WIDTH), lambda b, pt, ix: (b, 0, 0)))
    return pl.pallas_call(
        functools.partial(_nsa_sample_b_kernel, nb_past),
        out_shape=jax.ShapeDtypeStruct((nbatch, 1, NSA_WIDTH), F32),
        grid_spec=gs,
        compiler_params=_params(("arbitrary",)),
        name="nsa_sample_b",
    )(page_table, top_idx, proj_s3, proj_s3, proj_s3, part, *([pool_sel] * nsel), bias_s)


SB = 8


def _rec_sample_kernel(gq_ref, gk_ref, gv_ref, gg_ref, x_ref, dz_ref, sm_ref, sg_ref, sd_ref, cb_ref,
                       w2_ref, b2_ref, gnw_ref, cw_ref, al_ref, dt_ref, dnw_ref,
                       o_ref, sgo_ref, sdo_ref):
    sm = sm_ref[...]
    ri = lax.broadcasted_iota(I32, (SB, 128), 0)
    log_a = _log_sigmoid(_bdot(sm[:, SM_GLR:SM_GLR + GLA_GATE_RANK], w2_ref[...]) + b2_ref[...]) / GLA_GATE_NORM
    ea_t = jnp.exp(log_a).T
    k_t = gk_ref[...].T
    q_t = (gq_ref[...] * (GLA_DK ** -0.5)).T
    gv = gv_ref[...]
    for h in range(GLA_HEADS):
        hs = slice(h * GLA_DK, (h + 1) * GLA_DK)
        vs = slice(h * GLA_DV, (h + 1) * GLA_DV)
        o_h = jnp.zeros((SB, GLA_DV), F32)
        for i in range(SB):
            s = ea_t[hs, i:i + 1] * sg_ref[i, h] + k_t[hs, i:i + 1] * gv[i:i + 1, vs]
            sgo_ref[i, h] = s
            o = jnp.sum(q_t[hs, i:i + 1] * s, axis=0, keepdims=True)
            o_h = jnp.where(ri == i, o, o_h)
        o_ref[:, vs] = _rms(o_h, gnw_ref[...]) * _silu(gg_ref[:, vs])
    y = x_ref[...] * cw_ref[CONV_W - 1:CONV_W, :]
    for jw in range(CONV_W - 1):
        y = y + cb_ref[jw] * cw_ref[jw:jw + 1, :]
    y = _silu(y)
    beta = _sigmoid(sm[:, SM_DB:SM_DB + GDN_HEADS])
    eg = jnp.exp(-jnp.exp(al_ref[...]) * _softplus(sm[:, SM_DA:SM_DA + GDN_HEADS] + dt_ref[...]))
    dk, dv = GDN_DK, GDN_DV
    for h in range(GDN_HEADS):
        cq = y[:, h * dk:(h + 1) * dk]
        ck = y[:, GDN_WIDTH + h * dk:GDN_WIDTH + (h + 1) * dk]
        v = y[:, 2 * GDN_WIDTH + h * dv:2 * GDN_WIDTH + (h + 1) * dv]
        q_t = (cq * lax.rsqrt(jnp.sum(cq * cq, axis=-1, keepdims=True) + 1e-6) * (dk ** -0.5)).T
        k_t = (ck * lax.rsqrt(jnp.sum(ck * ck, axis=-1, keepdims=True) + 1e-6)).T
        o_h = jnp.zeros((SB, dv), F32)
        for i in range(SB):
            s = eg[i:i + 1, h:h + 1] * sd_ref[i, h]
            kc = k_t[:, i:i + 1]
            delta = (v[i:i + 1] - jnp.sum(kc * s, axis=0, keepdims=True)) * beta[i:i + 1, h:h + 1]
            s = s + kc * delta
            sdo_ref[i, h] = s
            o = jnp.sum(q_t[:, i:i + 1] * s, axis=0, keepdims=True)
            o_h = jnp.where(ri == i, o, o_h)
        vs = slice(GLA_WIDTH + h * dv, GLA_WIDTH + (h + 1) * dv)
        o_ref[:, vs] = _rms(o_h, dnw_ref[...]) * _silu(dz_ref[:, h * dv:(h + 1) * dv])


def _rec_sample(proj_s, state_gla, state_gdn, conv_t, w2, b2, gnw, cw, a_log, dt_bias, dnw, layer):
    nbatch = proj_s.shape[0]
    nblk = nbatch // SB
    row = lambda w, col: pl.BlockSpec((SB, w), lambda i: (i, col // w))
    full = lambda shape: pl.BlockSpec(shape, lambda i: (0,) * len(shape))
    return pl.pallas_call(
        _rec_sample_kernel,
        out_shape=(jax.ShapeDtypeStruct((nbatch, GLA_WIDTH + GDN_WIDTH), F32),
                   jax.ShapeDtypeStruct((nbatch, GLA_HEADS, GLA_DK, GLA_DV), F32),
                   jax.ShapeDtypeStruct((nbatch, GDN_HEADS, GDN_DK, GDN_DV), F32)),
        grid=(nblk,),
        in_specs=[row(GLA_HEADS * GLA_DK, C_GQ), row(GLA_HEADS * GLA_DK, C_GK), row(GLA_WIDTH, C_GV),
                  row(GLA_WIDTH, C_GG), row(3 * GDN_WIDTH, C_DQKV), row(GDN_WIDTH, C_DZ), row(128, C_SM),
                  pl.BlockSpec((SB, GLA_HEADS, GLA_DK, GLA_DV), lambda i: (layer * nblk + i, 0, 0, 0)),
                  pl.BlockSpec((SB, GDN_HEADS, GDN_DK, GDN_DV), lambda i: (layer * nblk + i, 0, 0, 0)),
                  pl.BlockSpec((CONV_W - 1, SB, 3 * GDN_WIDTH), lambda i: (0, i, 0)),
                  full(w2.shape), full(b2.shape), full(gnw.shape), full(cw.shape), full(a_log.shape),
                  full(dt_bias.shape), full(dnw.shape)],
        out_specs=(pl.BlockSpec((SB, GLA_WIDTH + GDN_WIDTH), lambda i: (i, 0)),
                   pl.BlockSpec((SB, GLA_HEADS, GLA_DK, GLA_DV), lambda i: (i, 0, 0, 0)),
                   pl.BlockSpec((SB, GDN_HEADS, GDN_DK, GDN_DV), lambda i: (i, 0, 0, 0))),
        compiler_params=_params(("parallel",)),
        name="rec_sample",
    )(proj_s, proj_s, proj_s, proj_s, proj_s, proj_s, proj_s, state_gla, state_gdn, conv_t,
      w2, b2, gnw, cw, a_log, dt_bias, dnw)


def _reorder_w_in(w):
    d = w.shape[0]
    return jnp.concatenate([
        w[:, 0:2560],
        w[:, 2584:3096],
        w[:, 4136:5672],
        w[:, 3096:3608],
        w[:, 3624:4136],
        w[:, 5672:6184],
        w[:, 2560:2584],
        w[:, 3608:3624],
        w[:, 6184:6192],
        jnp.zeros((d, PROJ_N - 6192), w.dtype)], axis=1)


def _cmp_weight_tile(w_cmp):
    half = NSA_KV_HEADS * HEAD_DIM
    return jnp.concatenate([jnp.broadcast_to(w_cmp[:, 0:1], (CMP_BLOCK, half)),
                            jnp.broadcast_to(w_cmp[:, 1:2], (CMP_BLOCK, half))], axis=1).astype(F32)


def _cmp_weight_rows(w_cmp):
    w = jnp.repeat(w_cmp, NSA_KV_HEADS, axis=1).reshape(CMP_BLOCK * KV_ROWS, 1)
    return jnp.broadcast_to(w, (CMP_BLOCK * KV_ROWS, HEAD_DIM)).astype(F32)


def kernel(x_prompt, x_sample, cache_cmp, cache_sel, cache_win, state_gla, state_gdn, state_conv, page_table,
           p_prompt, p_sample, norm_w, ffn_w1, ffn_w3, ffn_w2, w_in, w_out, nsa_w_cmp, t5_bias,
           gla_w_gk2, gla_b_gk, gla_norm_w, gdn_conv_w, gdn_a_log, gdn_dt_bias, gdn_norm_w,
           ple_w_proj, ple_w_gate):
    depth = w_in.shape[0]
    batch, seq, d = x_prompt.shape
    nbatch = x_sample.shape[0]
    n_phys = cache_cmp.shape[1]
    n_pages = page_table.shape[1]
    past = n_pages * PAGE_SIZE
    win_len = cache_win.shape[2]
    rows_p = batch * seq
    kv_shape = (2, NSA_KV_HEADS, HEAD_DIM)

    x = jnp.concatenate([x_prompt.reshape(rows_p, d), x_sample.reshape(nbatch, d)], axis=0)
    assert win_len == WINDOW and past // CMP_BLOCK + 1 >= N_SELECT and past // CMP_BLOCK < CMP_PAD
    pool_cmp = cache_cmp.reshape(-1, HEAD_DIM)
    pool_sel = cache_sel.reshape(-1, HEAD_DIM)
    win_all = cache_win.reshape(-1, HEAD_DIM)
    sgla_all = state_gla.reshape((depth * nbatch,) + state_gla.shape[2:])
    sgdn_all = state_gdn.reshape((depth * nbatch,) + state_gdn.shape[2:])

    pb_c, pb_s, pb_w = _prompt_bias_tables(t5_bias, seq)
    sb_c, sb_s, sb_w = _sample_bias_tables(t5_bias, past, win_len)
    expand = jnp.asarray(np.repeat(np.eye(seq // CMP_BLOCK, dtype=np.float32), CMP_BLOCK, axis=1), BF16)

    outs = {k: [] for k in ("cmp_p", "cmp_s", "sel_p", "sel_s", "win_p", "win_s",
                            "gla_p", "gla_s", "gdn_p", "gdn_s", "conv_p", "conv_s")}
    for l in range(depth):
        nw = norm_w[l].reshape(-1, 1, d)
        w1 = ffn_w1[l].astype(BF16)
        w3 = ffn_w3[l].astype(BF16)
        w2 = ffn_w2[l].astype(BF16)
        x = _ffn(x, nw[0], w1[0], w3[0], w2[0], nw[1])

        proj = _proj(x, nw[2], _reorder_w_in(w_in[l]).astype(BF16))
        proj_s = proj[rows_p:]
        proj_s3 = proj_s.reshape(nbatch, 1, PROJ_N)
        wtile = _cmp_weight_tile(nsa_w_cmp[l])
        w_gk2 = gla_w_gk2[l]
        b_gk = gla_b_gk[l].reshape(1, -1)
        gla_nw = gla_norm_w[l].reshape(1, -1)
        gdn_nw = gdn_norm_w[l].reshape(1, -1)
        conv_w = gdn_conv_w[l]
        a_log = gdn_a_log[l].reshape(1, -1)
        dt_bias = gdn_dt_bias[l].reshape(1, -1)

        cmpkv = _compress_prompt(proj, wtile, rows_p)
        kvs16 = proj[:rows_p, C_KVS:C_KVS + KVW].astype(BF16)
        kvw16 = proj[:rows_p, C_KVW:C_KVW + KVW].astype(BF16)
        o_nsa = _nsa_prompt(proj, cmpkv, kvs16, kvw16, pb_c, pb_s, pb_w, expand, batch, seq)
        o_gla, s_gla_p = _gla_prompt(proj, w_gk2, b_gk, gla_nw, batch, seq)
        o_gdn, s_gdn_p = _gdn_prompt(proj, conv_w, a_log, dt_bias, gdn_nw, batch, seq)

        part, top_idx, win_new = _nsa_sample_a(page_table, proj_s3, pool_cmp, win_all, _cmp_weight_rows(nsa_w_cmp[l]),
                                               sb_c, sb_w, l, n_phys)
        o_nsa_s = _nsa_sample_b(page_table, top_idx.reshape(nbatch, NSA_KV_HEADS * N_SELECT), proj_s3, part,
                                pool_sel, sb_s, l, n_phys)
        conv_t = jnp.swapaxes(state_conv[l], 0, 1)
        o_rec_s, s_gla_s, s_gdn_s = _rec_sample(proj_s, sgla_all, sgdn_all, conv_t, w_gk2, b_gk, gla_nw,
                                                conv_w, a_log, dt_bias, gdn_nw, l)

        y = jnp.concatenate([jnp.concatenate([o_nsa, o_gla, o_gdn], axis=1),
                             jnp.concatenate([o_nsa_s.reshape(nbatch, NSA_WIDTH), o_rec_s], axis=1)], axis=0)
        x = _outproj(y, x, w_out[l].astype(BF16), nw[3])
        x = _ffn(x, nw[4], w1[1], w3[1], w2[1], nw[5])
        p = jnp.concatenate([p_prompt[l].reshape(rows_p, -1), p_sample[l].reshape(nbatch, -1)], axis=0)
        x = _ple(x, p, nw[6], ple_w_gate[l].astype(BF16), ple_w_proj[l].astype(BF16), nw[7])

        pp = proj[:rows_p]
        outs["cmp_p"].append(pp[:, C_KVC:C_KVC + KVW].reshape((batch, seq) + kv_shape))
        outs["sel_p"].append(pp[:, C_KVS:C_KVS + KVW].reshape((batch, seq) + kv_shape))
        wp = min(WINDOW, seq)
        outs["win_p"].append(pp[:, C_KVW:C_KVW + KVW].reshape((batch, seq) + kv_shape)[:, seq - wp:])
        outs["cmp_s"].append(proj_s[:, C_KVC:C_KVC + KVW].reshape((nbatch, 1) + kv_shape))
        outs["sel_s"].append(proj_s[:, C_KVS:C_KVS + KVW].reshape((nbatch, 1) + kv_shape))
        outs["win_s"].append(win_new.reshape((nbatch, win_len) + kv_shape))
        outs["gla_p"].append(s_gla_p)
        outs["gla_s"].append(s_gla_s)
        outs["gdn_p"].append(s_gdn_p)
        outs["gdn_s"].append(s_gdn_s)
        dqkv_p = pp[:, C_DQKV:C_DQKV + 3 * GDN_WIDTH].reshape(batch, seq, 3 * GDN_WIDTH)
        outs["conv_p"].append(dqkv_p[:, seq - (CONV_W - 1):])
        outs["conv_s"].append(jnp.concatenate(
            [state_conv[l][:, 1:], proj_s[:, None, C_DQKV:C_DQKV + 3 * GDN_WIDTH]], axis=1))

    st = lambda k: jnp.stack(outs[k])
    return (x[:rows_p].reshape(batch, seq, d), x[rows_p:].reshape(nbatch, 1, d),
            st("cmp_p"), st("cmp_s"), st("sel_p"), st("sel_s"), st("win_p"), st("win_s"),
            st("gla_p"), st("gla_s"), st("gdn_p"), st("gdn_s"), st("conv_p"), st("conv_s"))
```

```python
import functools
import math

import numpy as np
import jax
import jax.numpy as jnp
from jax import lax
from jax.experimental import pallas as pl
from jax.experimental.pallas import tpu as pltpu

F32 = jnp.float32
BF16 = jnp.bfloat16
I32 = jnp.int32

D_MODEL = 2048
HEAD_DIM = 128
NSA_HEADS = 8
NSA_KV_HEADS = 2
NSA_GROUP = NSA_HEADS // NSA_KV_HEADS
CMP_BLOCK = 64
N_SELECT = 8
WINDOW = 512
Q_BLOCK = 128
SEL_FORCE = 1.0e4
GLA_HEADS = 4
GLA_DK = 64
GLA_DV = 128
GLA_GATE_RANK = 16
GLA_GATE_NORM = 16.0
GDN_HEADS = 4
GDN_DK = 128
GDN_DV = 128
CHUNK = 64
CONV_W = 4
N_BUCKETS = 32
T5_MAX_DIST = 128
D_FF = 5632
PLE_DIM = 256
RMS_EPS = 1e-6
NEG = -1e30
PAGE_SIZE = 128

NSA_WIDTH = NSA_HEADS * HEAD_DIM
KVW = 2 * NSA_KV_HEADS * HEAD_DIM
KV_ROWS = 2 * NSA_KV_HEADS
CMP_PAD = 64
GDN_WIDTH = GDN_HEADS * GDN_DV
GLA_WIDTH = GLA_HEADS * GLA_DV

C_NQ = 0
C_KVC = 1024
C_KVS = 1536
C_KVW = 2048
C_GQ = 2560
C_GK = 2816
C_DQKV = 3072
C_GV = 4608
C_GG = 5120
C_DZ = 5632
C_SM = 6144
PROJ_N = 6272
SM_GATE, SM_GLR, SM_DB, SM_DA = 0, 24, 40, 44

VMEM_LIMIT = 56 * 1024 * 1024
TM = 640
TF = 512
TN_PROJ = 896
TB = 256


def _sigmoid(x):
    return 1.0 / (1.0 + jnp.exp(-x))


def _silu(x):
    return x * _sigmoid(x)


def _softplus(x):
    return jnp.maximum(x, 0.0) + jnp.log(1.0 + jnp.exp(-jnp.abs(x)))


def _log_sigmoid(x):
    return jnp.minimum(x, 0.0) - jnp.log(1.0 + jnp.exp(-jnp.abs(x)))


def _rms(x, w):
    return x * lax.rsqrt(jnp.mean(x * x, axis=-1, keepdims=True) + RMS_EPS) * w


_NN = (((1,), (0,)), ((), ()))
_NT = (((1,), (1,)), ((), ()))
_TN = (((0,), (0,)), ((), ()))


def _bdot(a, b, dims=_NN):
    return lax.dot_general(a.astype(BF16), b.astype(BF16), dims, preferred_element_type=F32)


def _hdot(a, b, dims=_NN):
    return lax.dot_general(a, b, dims, preferred_element_type=F32, precision=lax.Precision.HIGHEST)


def _params(sem):
    return pltpu.CompilerParams(dimension_semantics=sem, vmem_limit_bytes=VMEM_LIMIT)


def _ffn_kernel(x_ref, nwa_ref, w1_ref, w3_ref, w2_ref, nwb_ref, o_ref, h_ref, acc_ref):
    f = pl.program_id(1)

    @pl.when(f == 0)
    def _():
        h_ref[...] = _rms(x_ref[...], nwa_ref[...]).astype(BF16)
        acc_ref[...] = jnp.zeros_like(acc_ref)

    h = h_ref[...]
    a = jnp.dot(h, w1_ref[...], preferred_element_type=F32)
    b = jnp.dot(h, w3_ref[...], preferred_element_type=F32)
    g = (_silu(a) * b).astype(BF16)
    acc_ref[...] += jnp.dot(g, w2_ref[...], preferred_element_type=F32)

    @pl.when(f == pl.num_programs(1) - 1)
    def _():
        o_ref[...] = x_ref[...] + 0.5 * _rms(acc_ref[...], nwb_ref[...])


def _ffn(x, nwa, w1, w3, w2, nwb):
    m, d = x.shape
    dff = w1.shape[1]
    row = lambda i, f: (i, 0)
    return pl.pallas_call(
        _ffn_kernel,
        out_shape=jax.ShapeDtypeStruct((m, d), F32),
        grid=(m // TM, dff // TF),
        in_specs=[pl.BlockSpec((TM, d), row),
                  pl.BlockSpec((1, d), lambda i, f: (0, 0)),
                  pl.BlockSpec((d, TF), lambda i, f: (0, f)),
                  pl.BlockSpec((d, TF), lambda i, f: (0, f)),
                  pl.BlockSpec((TF, d), lambda i, f: (f, 0)),
                  pl.BlockSpec((1, d), lambda i, f: (0, 0))],
        out_specs=pl.BlockSpec((TM, d), row),
        scratch_shapes=[pltpu.VMEM((TM, d), BF16), pltpu.VMEM((TM, d), F32)],
        compiler_params=_params(("parallel", "arbitrary")),
        name="ffn",
    )(x, nwa, w1, w3, w2, nwb)


def _proj_kernel(x_ref, nw_ref, w_ref, o_ref, h_ref):
    @pl.when(pl.program_id(1) == 0)
    def _():
        h_ref[...] = _rms(x_ref[...], nw_ref[...]).astype(BF16)

    o_ref[...] = jnp.dot(h_ref[...], w_ref[...], preferred_element_type=F32)


def _proj(x, nw, w):
    m, d = x.shape
    n = w.shape[1]
    return pl.pallas_call(
        _proj_kernel,
        out_shape=jax.ShapeDtypeStruct((m, n), F32),
        grid=(m // TM, n // TN_PROJ),
        in_specs=[pl.BlockSpec((TM, d), lambda i, j: (i, 0)),
                  pl.BlockSpec((1, d), lambda i, j: (0, 0)),
                  pl.BlockSpec((d, TN_PROJ), lambda i, j: (0, j))],
        out_specs=pl.BlockSpec((TM, TN_PROJ), lambda i, j: (i, j)),
        scratch_shapes=[pltpu.VMEM((TM, d), BF16)],
        compiler_params=_params(("parallel", "arbitrary")),
        name="proj",
    )(x, nw, w)


def _outproj_kernel(y_ref, x_ref, w_ref, nw_ref, o_ref):
    z = jnp.dot(y_ref[...].astype(BF16), w_ref[...], preferred_element_type=F32)
    o_ref[...] = x_ref[...] + _rms(z, nw_ref[...])


def _outproj(y, x, w, nw):
    m, d = x.shape
    k = y.shape[1]
    return pl.pallas_call(
        _outproj_kernel,
        out_shape=jax.ShapeDtypeStruct((m, d), F32),
        grid=(m // TM,),
        in_specs=[pl.BlockSpec((TM, k), lambda i: (i, 0)),
                  pl.BlockSpec((TM, d), lambda i: (i, 0)),
                  pl.BlockSpec((k, d), lambda i: (0, 0)),
                  pl.BlockSpec((1, d), lambda i: (0, 0))],
        out_specs=pl.BlockSpec((TM, d), lambda i: (i, 0)),
        compiler_params=_params(("parallel",)),
        name="outproj",
    )(y, x, w, nw)


def _ple_kernel(x_ref, p_ref, nwa_ref, wg_ref, wp_ref, nwb_ref, o_ref):
    x = x_ref[...]
    gate = _sigmoid(jnp.dot(_rms(x, nwa_ref[...]).astype(BF16), wg_ref[...], preferred_element_type=F32))
    pp = jnp.dot(p_ref[...].astype(BF16), wp_ref[...], preferred_element_type=F32)
    o_ref[...] = x + _rms(gate * pp, nwb_ref[...])


def _ple(x, p, nwa, wg, wp, nwb):
    m, d = x.shape
    pd = p.shape[1]
    return pl.pallas_call(
        _ple_kernel,
        out_shape=jax.ShapeDtypeStruct((m, d), F32),
        grid=(m // TM,),
        in_specs=[pl.BlockSpec((TM, d), lambda i: (i, 0)),
                  pl.BlockSpec((TM, pd), lambda i: (i, 0)),
                  pl.BlockSpec((1, d), lambda i: (0, 0)),
                  pl.BlockSpec((d, d), lambda i: (0, 0)),
                  pl.BlockSpec((pd, d), lambda i: (0, 0)),
                  pl.BlockSpec((1, d), lambda i: (0, 0))],
        out_specs=pl.BlockSpec((TM, d), lambda i: (i, 0)),
        compiler_params=_params(("parallel",)),
        name="ple",
    )(x, p, nwa, wg, wp, nwb)


def _t5_bucket_np(dist):
    n = np.maximum(dist, 0)
    exact = N_BUCKETS // 2
    val = (np.log(np.maximum(n, 1).astype(np.float32) / np.float32(exact))
           / np.float32(math.log(T5_MAX_DIST / exact)) * np.float32(N_BUCKETS - exact))
    large = exact + val.astype(np.int32)
    return np.where(n < exact, n, np.minimum(large, N_BUCKETS - 1)).astype(np.int32)


def _bias_table(t5_bias, dist, valid):
    bucket = jnp.asarray(np.where(valid, _t5_bucket_np(dist), -1).astype(np.int32))[None]
    hshape = (t5_bias.shape[1],) + (1,) * dist.ndim
    out = jnp.full((t5_bias.shape[1],) + dist.shape, NEG, F32)
    for k in range(N_BUCKETS):
        out = jnp.where(bucket == k, t5_bias[k].reshape(hshape), out)
    return out


def _prompt_bias_tables(t5_bias, seq):
    nb = seq // CMP_BLOCK
    qpos = np.arange(seq)[:, None]
    d_c = qpos - (np.arange(nb) * CMP_BLOCK + CMP_BLOCK - 1)[None, :]
    bias_c = _bias_table(t5_bias, d_c, d_c >= 0)
    i = np.arange(Q_BLOCK)[:, None]
    jj = np.arange(Q_BLOCK)[None, :]
    d_s = np.stack([Q_BLOCK * dl + i - jj for dl in range(3)] + [i - jj - Q_BLOCK])
    bias_s = jnp.transpose(_bias_table(t5_bias, d_s, d_s >= 0), (1, 0, 2, 3))
    nwc = WINDOW // Q_BLOCK + 1
    d_w = np.stack([i + WINDOW - Q_BLOCK * cw - jj for cw in range(nwc)] + [i - jj - Q_BLOCK])
    bias_w = jnp.transpose(_bias_table(t5_bias, d_w, (d_w >= 0) & (d_w < WINDOW)), (1, 0, 2, 3))
    return bias_c, bias_s, bias_w


def _sample_bias_tables(t5_bias, past, win_len):
    nb_past = past // CMP_BLOCK
    head_g = (np.arange(NSA_HEADS) // NSA_GROUP)[:, None]

    def per_head(dist, valid):
        lane_cg = np.arange(dist.shape[-1]) % KV_ROWS
        tbl = _bias_table(t5_bias, dist, valid)
        own = jnp.asarray(lane_cg[None] == head_g)
        return tbl, own

    lane = np.arange(CMP_PAD * KV_ROWS)
    n = lane // KV_ROWS
    d_c = past - (n * CMP_BLOCK + CMP_BLOCK - 1)
    tbl, own = per_head(d_c, (d_c >= 0) & (n <= nb_past))
    bias_c = jnp.where(own, tbl, NEG)
    lane = np.arange(CMP_BLOCK * KV_ROWS)
    blk = np.arange(nb_past + 1)[:, None]
    d_s = past - (blk * CMP_BLOCK + (lane // KV_ROWS)[None, :])
    tbl, own = per_head(d_s, d_s >= 0)
    bias_s = jnp.transpose(jnp.where(own[:, None, :], tbl, NEG), (1, 0, 2))
    lane = np.arange(win_len * KV_ROWS)
    wpos = past - win_len + 1 + lane // KV_ROWS
    d_w = past - wpos
    tbl, own = per_head(d_w, (d_w >= 0) & (d_w < WINDOW) & (wpos >= 0))
    bias_w = jnp.where(own, tbl, NEG)
    return bias_c, bias_s, bias_w


def _compress_kernel(x_ref, w_ref, o_ref):
    r = x_ref.shape[0] // CMP_BLOCK
    x = x_ref[...].reshape(r, CMP_BLOCK, KVW)
    o_ref[...] = jnp.sum(x * w_ref[...][None], axis=1)


def _compress_prompt(proj, wtile, rows):
    rb = 512
    return pl.pallas_call(
        _compress_kernel,
        out_shape=jax.ShapeDtypeStruct((rows // CMP_BLOCK, KVW), F32),
        grid=(rows // rb,),
        in_specs=[pl.BlockSpec((rb, KVW), lambda i: (i, C_KVC // KVW)),
                  pl.BlockSpec((CMP_BLOCK, KVW), lambda i: (0, 0))],
        out_specs=pl.BlockSpec((rb // CMP_BLOCK, KVW), lambda i: (i, 0)),
        compiler_params=_params(("parallel",)),
        name="compress_prompt",
    )(proj, wtile)


KEY_CHUNK = 2 * Q_BLOCK


def _nsa_prompt_kernel(q_ref, sm_ref, cmp_ref, kvs_ref, kvw_ref, bc_ref, bs_ref, bw_ref, e_ref, o_ref,
                       mask_ref, s_ref, mx_ref, l_ref, acc_ref):
    j = pl.program_id(1)
    nb = cmp_ref.shape[0]
    nkc = mask_ref.shape[1]
    hg, qb, dh, kc = NSA_GROUP, Q_BLOCK, HEAD_DIM, KEY_CHUNK
    rows = hg * qb
    nwt = bw_ref.shape[0] - 1
    gates = _sigmoid(sm_ref[:, SM_GATE:SM_GATE + 3 * NSA_HEADS])
    qi = lax.broadcasted_iota(I32, (qb, nb), 0)
    ni = lax.broadcasted_iota(I32, (qb, nb), 1)
    cur = (qb // CMP_BLOCK) * j + qi // CMP_BLOCK
    forced = (ni == cur) | (ni == cur - 1) | (ni == 0)
    started = ni <= cur
    groups = range(NSA_KV_HEADS)
    kcols = [slice(g * dh, (g + 1) * dh) for g in groups]
    vcols = [slice((NSA_KV_HEADS + g) * dh, (NSA_KV_HEADS + g + 1) * dh) for g in groups]
    head_rows = [slice(g * hg, (g + 1) * hg) for g in groups]

    def fold(x, op):
        out = x[:, 0:qb]
        for t in range(1, x.shape[1] // qb):
            out = op(out, x[:, t * qb:(t + 1) * qb])
        return out

    q16, o_c, score = [], [], []
    for g in groups:
        q = jnp.concatenate([q_ref[:, (g * hg + h) * dh:(g * hg + h + 1) * dh] for h in range(hg)], axis=0)
        q16.append((q * (dh ** -0.5)).astype(BF16))
        bias = bc_ref[head_rows[g]].reshape(rows, nb)
        valid = bias > -1e29
        s = jnp.where(valid, _bdot(q16[g], cmp_ref[:, kcols[g]], _NT) + bias, NEG)
        p = jnp.where(valid, jnp.exp(s - jnp.max(s, axis=-1, keepdims=True)), 0.0)
        p = p / jnp.maximum(jnp.sum(p, axis=-1, keepdims=True), 1e-30)
        o_c.append(_bdot(p, cmp_ref[:, vcols[g]]))
        imp = p[0:qb]
        for h in range(1, hg):
            imp = imp + p[h * qb:(h + 1) * qb]
        score.append(jnp.where(started, jnp.where(forced, SEL_FORCE, imp), -1.0))

    sel = [jnp.zeros((qb, nb), F32) for _ in groups]
    for _ in range(min(N_SELECT, nb)):
        for g in groups:
            hit = ni == jnp.argmax(score[g], axis=-1, keepdims=True).astype(I32)
            sel[g] = jnp.where(hit & started, 1.0, sel[g])
            score[g] = jnp.where(hit, -3e38, score[g])
    for g in groups:
        selexp = jnp.dot(sel[g].astype(BF16), e_ref[...], preferred_element_type=F32)
        for c in range(nkc):
            mask_ref[g, c] = (selexp[:, c * kc:(c + 1) * kc] - 1.0) * (-NEG)

    mx_ref[...] = jnp.full(mx_ref.shape, NEG, F32)
    l_ref[...] = jnp.zeros(l_ref.shape, F32)
    acc_ref[...] = jnp.zeros(acc_ref.shape, F32)
    n_chunks = j // (kc // qb) + 1

    def key_rows(c):
        return pl.ds(pl.multiple_of(c * kc, kc), kc)

    def sel_bias(g, c):
        tiles = []
        for t in range(kc // qb):
            back = j - (c * (kc // qb) + t)
            tiles.append(bs_ref[jnp.where(back < 0, 3, jnp.minimum(back, 2)), head_rows[g]].reshape(rows, qb))
        return jnp.concatenate(tiles, axis=1)

    def pass1(c, carry):
        for g in groups:
            s_ = (_bdot(q16[g], kvs_ref[key_rows(c), kcols[g]], _NT) + sel_bias(g, c)
                  + jnp.concatenate([mask_ref[g, c]] * hg, axis=0))
            s_ref[g, c] = s_
            mx_ref[g] = jnp.maximum(mx_ref[g], fold(s_, jnp.maximum))
        return carry

    lax.fori_loop(0, n_chunks, pass1, 0)
    for g in groups:
        mx_ref[g] = jnp.broadcast_to(jnp.max(mx_ref[g], axis=-1, keepdims=True), mx_ref.shape[1:])

    def pass2(c, carry):
        for g in groups:
            p_ = jnp.exp(s_ref[g, c] - jnp.concatenate([mx_ref[g]] * (kc // qb), axis=1))
            l_ref[g] += fold(p_, jnp.add)
            acc_ref[g] += _bdot(p_, kvs_ref[key_rows(c), vcols[g]])
        return carry

    lax.fori_loop(0, n_chunks, pass2, 0)
    o_s = [acc_ref[g] / jnp.maximum(jnp.sum(l_ref[g], axis=-1, keepdims=True), 1e-30) for g in groups]

    w_rows = pl.ds(pl.multiple_of(j * qb, qb), nwt * qb)
    o_w = []
    for g in groups:
        bias = jnp.concatenate(
            [bw_ref[jnp.where(j - (nwt - 1) + t >= 0, t, nwt), head_rows[g]].reshape(rows, qb) for t in range(nwt)],
            axis=1)
        s = _bdot(q16[g], kvw_ref[w_rows, kcols[g]], _NT) + bias
        p = jnp.exp(s - jnp.max(fold(s, jnp.maximum), axis=-1, keepdims=True))
        den = jnp.sum(fold(p, jnp.add), axis=-1, keepdims=True)
        o_w.append(_bdot(p, kvw_ref[w_rows, vcols[g]]) / jnp.maximum(den, 1e-30))

    for g in groups:
        for h in range(hg):
            c0 = SM_GATE + (g * hg + h) * 3
            rs = slice(h * qb, (h + 1) * qb)
            o_ref[:, (g * hg + h) * dh:(g * hg + h + 1) * dh] = (
                o_c[g][rs] * gates[:, c0:c0 + 1] + o_s[g][rs] * gates[:, c0 + 1:c0 + 2]
                + o_w[g][rs] * gates[:, c0 + 2:c0 + 3])


def _nsa_prompt(proj, cmpkv, kvs16, kvw16, bias_c, bias_s, bias_w, expand, batch, seq):
    nqb = seq // Q_BLOCK
    nb = seq // CMP_BLOCK
    rows = NSA_GROUP * Q_BLOCK
    full = lambda shape: pl.BlockSpec(shape, lambda b, j: (0,) * len(shape))
    return pl.pallas_call(
        _nsa_prompt_kernel,
        out_shape=jax.ShapeDtypeStruct((batch * seq, NSA_WIDTH), F32),
        grid=(batch, nqb),
        in_specs=[pl.BlockSpec((Q_BLOCK, NSA_WIDTH), lambda b, j: (b * nqb + j, 0)),
                  pl.BlockSpec((Q_BLOCK, 128), lambda b, j: (b * nqb + j, C_SM // 128)),
                  pl.BlockSpec((nb, KVW), lambda b, j: (b, 0)),
                  pl.BlockSpec((seq, KVW), lambda b, j: (b, 0)),
                  pl.BlockSpec((WINDOW + seq, KVW), lambda b, j: (b, 0)),
                  pl.BlockSpec((NSA_HEADS, Q_BLOCK, nb), lambda b, j: (0, j, 0)),
                  full(bias_s.shape), full(bias_w.shape), full(expand.shape)],
        out_specs=pl.BlockSpec((Q_BLOCK, NSA_WIDTH), lambda b, j: (b * nqb + j, 0)),
        scratch_shapes=[pltpu.VMEM((NSA_KV_HEADS, seq // KEY_CHUNK, Q_BLOCK, KEY_CHUNK), F32),
                        pltpu.VMEM((NSA_KV_HEADS, seq // KEY_CHUNK, rows, KEY_CHUNK), F32),
                        pltpu.VMEM((NSA_KV_HEADS, rows, Q_BLOCK), F32), pltpu.VMEM((NSA_KV_HEADS, rows, Q_BLOCK), F32),
                        pltpu.VMEM((NSA_KV_HEADS, rows, HEAD_DIM), F32)],
        compiler_params=_params(("parallel", "arbitrary")),
        name="nsa_prompt",
    )(proj, proj, cmpkv, kvs16, kvw16, bias_c, bias_s, bias_w, expand)


_BNN = (((2,), (1,)), ((0,), (0,)))
_BNT = (((2,), (2,)), ((0,), (0,)))
_BTN = (((1,), (1,)), ((0,), (0,)))


def _bdot_b(a, b, dims=_BNN):
    return lax.dot_general(a.astype(BF16), b.astype(BF16), dims, preferred_element_type=F32)


def _tri_masks():
    r = lax.broadcasted_iota(I32, (CHUNK, CHUNK), 0)
    c = lax.broadcasted_iota(I32, (CHUNK, CHUNK), 1)
    return r, c


def _gla_prompt_kernel(q_ref, k_ref, v_ref, gg_ref, sm_ref, w2_ref, b2_ref, nw_ref, o_ref, so_ref, s_ref):
    t = pl.program_id(1)

    @pl.when(t == 0)
    def _():
        s_ref[...] = jnp.zeros_like(s_ref)

    r, c = _tri_masks()
    lower = r >= c
    tril = jnp.where(lower, 1.0, 0.0).astype(F32)
    dk, dv = GLA_DK, GLA_DV
    nch = q_ref.shape[0] // CHUNK
    pairs = [(ch, h) for ch in range(nch) for h in range(GLA_HEADS)]
    rs = lambda ch: slice(ch * CHUNK, (ch + 1) * CHUNK)
    stack = lambda fn: jnp.stack([fn(ch, h) for ch, h in pairs], axis=0)

    log_a = _log_sigmoid(_bdot(sm_ref[:, SM_GLR:SM_GLR + GLA_GATE_RANK], w2_ref[...]) + b2_ref[...]) / GLA_GATE_NORM
    width = GLA_HEADS * dk
    b_all = _hdot(tril, jnp.concatenate([log_a[rs(ch)] for ch in range(nch)], axis=1))
    b_all_t = b_all.T
    b = stack(lambda ch, h: b_all[:, ch * width + h * dk:ch * width + (h + 1) * dk])
    b_last = b[:, CHUNK - 1:CHUNK, :]
    b_last_col = stack(lambda ch, h: b_all_t[ch * width + h * dk:ch * width + (h + 1) * dk, CHUNK - 1:CHUNK])
    q = stack(lambda ch, h: q_ref[rs(ch), h * dk:(h + 1) * dk]) * (dk ** -0.5)
    k = stack(lambda ch, h: k_ref[rs(ch), h * dk:(h + 1) * dk])
    v = stack(lambda ch, h: v_ref[rs(ch), h * dv:(h + 1) * dv])
    qe = q * jnp.exp(b)
    att = jnp.where(lower[None], _bdot_b(qe, k * jnp.exp(-b), _BNT), 0.0)
    o_intra = _bdot_b(att, v)
    kv = _bdot_b(k * jnp.exp(b_last - b), v, _BTN)
    s_decay = jnp.exp(b_last_col)

    for p, (ch, h) in enumerate(pairs):
        s = s_ref[h]
        o = _bdot(qe[p], s) + o_intra[p]
        s_ref[h] = s_decay[p] * s + kv[p]
        o_ref[rs(ch), h * dv:(h + 1) * dv] = _rms(o, nw_ref[...]) * _silu(gg_ref[rs(ch), h * dv:(h + 1) * dv])
    so_ref[0] = s_ref[...]


def _gla_prompt(proj, w2, b2, nw, batch, seq):
    nt = seq // TB
    row = lambda w, col: pl.BlockSpec((TB, w), lambda b, t: (b * nt + t, col // w))
    full = lambda shape: pl.BlockSpec(shape, lambda b, t: (0,) * len(shape))
    return pl.pallas_call(
        _gla_prompt_kernel,
        out_shape=(jax.ShapeDtypeStruct((batch * seq, GLA_WIDTH), F32),
                   jax.ShapeDtypeStruct((batch, GLA_HEADS, GLA_DK, GLA_DV), F32)),
        grid=(batch, nt),
        in_specs=[row(GLA_HEADS * GLA_DK, C_GQ), row(GLA_HEADS * GLA_DK, C_GK), row(GLA_WIDTH, C_GV),
                  row(GLA_WIDTH, C_GG), row(128, C_SM), full(w2.shape), full(b2.shape), full(nw.shape)],
        out_specs=(pl.BlockSpec((TB, GLA_WIDTH), lambda b, t: (b * nt + t, 0)),
                   pl.BlockSpec((1, GLA_HEADS, GLA_DK, GLA_DV), lambda b, t: (b, 0, 0, 0))),
        scratch_shapes=[pltpu.VMEM((GLA_HEADS, GLA_DK, GLA_DV), F32)],
        compiler_params=_params(("parallel", "arbitrary")),
        name="gla_prompt",
    )(proj, proj, proj, proj, proj, w2, b2, nw)


def _unit_lower_inverse(m, r, c):
    eye = jnp.where(r == c, 1.0, 0.0).astype(F32)[None]
    base = 8
    m8 = jnp.where(((r // base) == (c // base))[None], m, 0.0)
    m2 = _bdot_b(m8, m8)
    m4 = _bdot_b(m2, m2)
    t = _bdot_b(_bdot_b(eye - m8, eye + m2), eye + m4)
    s = base
    while s < CHUNK:
        off = ((r // (2 * s)) == (c // (2 * s))) & ((r // s) != (c // s))
        t = t - _bdot_b(t, _bdot_b(jnp.where(off[None], m, 0.0), t))
        s *= 2
    return t


def _gdn_prompt_kernel(x_ref, dz_ref, sm_ref, cw_ref, al_ref, dt_ref, nw_ref, o_ref, so_ref, s_ref, tail_ref):
    t = pl.program_id(1)

    @pl.when(t == 0)
    def _():
        s_ref[...] = jnp.zeros_like(s_ref)
        tail_ref[...] = jnp.zeros_like(tail_ref)

    tb = x_ref.shape[0]
    x = x_ref[...]
    xc = jnp.concatenate([tail_ref[...], x], axis=0)
    off = 8 - (CONV_W - 1)
    y = xc[off:off + tb] * cw_ref[0:1, :]
    for jw in range(1, CONV_W):
        y = y + xc[off + jw:off + jw + tb] * cw_ref[jw:jw + 1, :]
    y = _silu(y)
    tail_ref[...] = x[tb - 8:tb]

    beta_all = _sigmoid(sm_ref[:, SM_DB:SM_DB + GDN_HEADS])
    g_all = -jnp.exp(al_ref[...]) * _softplus(sm_ref[:, SM_DA:SM_DA + GDN_HEADS] + dt_ref[...])
    r, c = _tri_masks()
    lower = r >= c
    strict = r > c
    tril = jnp.where(lower, 1.0, 0.0).astype(F32)
    dk, dv = GDN_DK, GDN_DV
    nch = tb // CHUNK
    pairs = [(ch, h) for ch in range(nch) for h in range(GDN_HEADS)]
    rs = lambda ch: slice(ch * CHUNK, (ch + 1) * CHUNK)
    stack = lambda fn: jnp.stack([fn(ch, h) for ch, h in pairs], axis=0)

    cq = stack(lambda ch, h: y[rs(ch), h * dk:(h + 1) * dk])
    ck = stack(lambda ch, h: y[rs(ch), GDN_WIDTH + h * dk:GDN_WIDTH + (h + 1) * dk])
    v = stack(lambda ch, h: y[rs(ch), 2 * GDN_WIDTH + h * dv:2 * GDN_WIDTH + (h + 1) * dv])
    q = cq * lax.rsqrt(jnp.sum(cq * cq, axis=-1, keepdims=True) + 1e-6) * (dk ** -0.5)
    k = ck * lax.rsqrt(jnp.sum(ck * ck, axis=-1, keepdims=True) + 1e-6)
    beta = stack(lambda ch, h: jnp.broadcast_to(beta_all[rs(ch), h:h + 1], (CHUNK, dk)))
    g_cols = jnp.concatenate([g_all[rs(ch)] for ch in range(nch)], axis=1)
    gam_cols = _hdot(tril, g_cols)
    gam_rows = gam_cols.T
    gam = jnp.stack([jnp.broadcast_to(gam_cols[:, p:p + 1], (CHUNK, dk)) for p in range(len(pairs))], axis=0)
    decay = jnp.exp(jnp.where(lower[None], gam[:, :, 0:CHUNK] - gam_rows[:, None, :], NEG))
    kb = k * beta
    m = jnp.where(strict[None], _bdot_b(kb, k, _BNT) * decay, 0.0)
    tinv = _unit_lower_inverse(m, r, c)
    eg = jnp.exp(gam)
    u = _bdot_b(tinv, v * beta)
    w = _bdot_b(tinv, kb * eg)
    att = _bdot_b(q, k, _BNT) * decay
    qe = q * eg
    g_last = gam[:, CHUNK - 1:CHUNK, :]
    kd = k * jnp.exp(g_last - gam)
    eg_last = jnp.exp(g_last)

    for p, (ch, h) in enumerate(pairs):
        s = s_ref[h]
        v_new = u[p] - _bdot(w[p], s)
        o = _bdot(qe[p], s) + _bdot(att[p], v_new)
        s_ref[h] = eg_last[p] * s + _bdot(kd[p], v_new, _TN)
        o_ref[rs(ch), h * dv:(h + 1) * dv] = _rms(o, nw_ref[...]) * _silu(dz_ref[rs(ch), h * dv:(h + 1) * dv])
    so_ref[0] = s_ref[...]


def _gdn_prompt(proj, cw, a_log, dt_bias, nw, batch, seq):
    nt = seq // TB
    row = lambda w, col: pl.BlockSpec((TB, w), lambda b, t: (b * nt + t, col // w))
    full = lambda shape: pl.BlockSpec(shape, lambda b, t: (0,) * len(shape))
    return pl.pallas_call(
        _gdn_prompt_kernel,
        out_shape=(jax.ShapeDtypeStruct((batch * seq, GDN_WIDTH), F32),
                   jax.ShapeDtypeStruct((batch, GDN_HEADS, GDN_DK, GDN_DV), F32)),
        grid=(batch, nt),
        in_specs=[row(3 * GDN_WIDTH, C_DQKV), row(GDN_WIDTH, C_DZ), row(128, C_SM),
                  full(cw.shape), full(a_log.shape), full(dt_bias.shape), full(nw.shape)],
        out_specs=(pl.BlockSpec((TB, GDN_WIDTH), lambda b, t: (b * nt + t, 0)),
                   pl.BlockSpec((1, GDN_HEADS, GDN_DK, GDN_DV), lambda b, t: (b, 0, 0, 0))),
        scratch_shapes=[pltpu.VMEM((GDN_HEADS, GDN_DK, GDN_DV), F32), pltpu.VMEM((8, 3 * GDN_WIDTH), F32)],
        compiler_params=_params(("parallel", "arbitrary")),
        name="gdn_prompt",
    )(proj, proj, proj, cw, a_log, dt_bias, nw)


def _masked_softmax_rows(s):
    valid = s > -1e29
    p = jnp.where(valid, jnp.exp(s - jnp.max(s, axis=-1, keepdims=True)), 0.0)
    return p / jnp.maximum(jnp.sum(p, axis=-1, keepdims=True), 1e-30)


def _head_rows(q_ref, sb, lo, hi):
    dh = HEAD_DIM
    q = jnp.concatenate([q_ref[sb, :, h * dh:(h + 1) * dh] for h in range(lo, hi)], axis=0)
    return (q * (dh ** -0.5)).astype(BF16)


def _kv_rows(kv_ref, sb):
    dh = HEAD_DIM
    return jnp.concatenate([kv_ref[sb, :, r * dh:(r + 1) * dh] for r in range(KV_ROWS)], axis=0)


def _value_weights(p):
    return pltpu.roll(p, NSA_KV_HEADS, 1)


SN = 2


def _nsa_sample_a_kernel(n_pages, nb_past, pt_ref, q_ref, kvc_ref, kvw_ref, sm_ref, *rest):
    page_refs = rest[:SN * n_pages]
    win_ref, w4_ref, bc_ref, bw_ref, part_ref, idx_ref, wo_ref, cmp_ref = rest[SN * n_pages:]
    dh = HEAD_DIM
    blk_rows = CMP_BLOCK * KV_ROWS
    bpp = PAGE_SIZE // CMP_BLOCK
    wl = win_ref.shape[0] // SN
    w4 = w4_ref[...]
    lanes = cmp_ref.shape[1]
    lane = lax.broadcasted_iota(I32, (NSA_KV_HEADS, lanes), 1)
    gi = lax.broadcasted_iota(I32, (NSA_KV_HEADS, lanes), 0)
    n = lane // KV_ROWS
    cur = nb_past
    cand = ((lane % KV_ROWS) == gi) & (n <= cur)
    forced = (n == cur) | (n == cur - 1) | (n == 0)
    li = lax.broadcasted_iota(I32, (NSA_KV_HEADS, N_SELECT), 1)

    for sb in range(SN):
        for p in range(n_pages):
            x = page_refs[sb * n_pages + p][...]
            sums = []
            for half in range(bpp):
                pr = x[half * blk_rows:(half + 1) * blk_rows] * w4
                s8 = jnp.sum(pr.reshape(blk_rows // 8, 8, dh), axis=0)
                sums.append(s8[0:KV_ROWS] + s8[KV_ROWS:2 * KV_ROWS])
            cmp_ref[sb, p * bpp * KV_ROWS:(p + 1) * bpp * KV_ROWS, :] = jnp.concatenate(sums, axis=0)
        r0 = nb_past * KV_ROWS
        cmp_ref[sb, r0:r0 + KV_ROWS, :] = _kv_rows(kvc_ref, sb) * w4[0:KV_ROWS]
        cmp_ref[sb, r0 + KV_ROWS:, :] = jnp.zeros((lanes - r0 - KV_ROWS, dh), F32)
        cm16 = cmp_ref[sb].astype(BF16)

        q16 = _head_rows(q_ref, sb, 0, NSA_HEADS)
        p = _masked_softmax_rows(_bdot(q16, cm16, _NT) + bc_ref[...])
        o_c = _bdot(_value_weights(p), cm16)

        imp = jnp.concatenate([jnp.sum(p[g * NSA_GROUP:(g + 1) * NSA_GROUP], axis=0, keepdims=True)
                               for g in range(NSA_KV_HEADS)], axis=0)
        score = jnp.where(cand, jnp.where(forced, SEL_FORCE, imp), -3e38)
        top = jnp.zeros((NSA_KV_HEADS, N_SELECT), I32)
        for r in range(N_SELECT):
            a = jnp.argmax(score, axis=-1, keepdims=True).astype(I32)
            top = jnp.where(li == r, a // KV_ROWS, top)
            score = jnp.where(lane == a, -3e38, score)
        idx_ref[sb] = top

        ws = slice(sb * wl, (sb + 1) * wl)
        wo_ref[ws, :] = pltpu.roll(win_ref[ws, :], wl - KV_ROWS, 0)
        wo_ref[(sb + 1) * wl - KV_ROWS:(sb + 1) * wl, :] = _kv_rows(kvw_ref, sb)
        w16 = wo_ref[ws, :].astype(BF16)
        pw = _masked_softmax_rows(_bdot(q16, w16, _NT) + bw_ref[...])
        o_w = _bdot(_value_weights(pw), w16)

        gates = _sigmoid(sm_ref[sb, :, SM_GATE:SM_GATE + 3 * NSA_HEADS])
        for h in range(NSA_HEADS):
            c0 = 3 * h
            part_ref[sb, :, h * dh:(h + 1) * dh] = (
                o_c[h:h + 1] * gates[:, c0:c0 + 1] + o_w[h:h + 1] * gates[:, c0 + 2:c0 + 3])


def _nsa_sample_a(page_table, proj_s3, pool_cmp, win_rows, w4, bias_c, bias_w, layer, n_phys):
    nbatch, n_pages = page_table.shape
    nb_past = n_pages * PAGE_SIZE // CMP_BLOCK
    wl = bias_w.shape[1]
    page_rows = PAGE_SIZE * KV_ROWS
    nsteps = nbatch // SN
    col = lambda w, c: pl.BlockSpec((SN, 1, w), lambda i, pt: (i, 0, c // w))
    full = lambda shape: pl.BlockSpec(shape, lambda i, pt: (0,) * len(shape))
    page = lambda sb, p: pl.BlockSpec((page_rows, HEAD_DIM), lambda i, pt: (layer * n_phys + pt[SN * i + sb, p], 0))
    gs = pltpu.PrefetchScalarGridSpec(
        num_scalar_prefetch=1, grid=(nsteps,),
        in_specs=[col(NSA_WIDTH, C_NQ), col(KVW, C_KVC), col(KVW, C_KVW), col(128, C_SM)]
        + [page(sb, p) for sb in range(SN) for p in range(n_pages)]
        + [pl.BlockSpec((SN * wl, HEAD_DIM), lambda i, pt: (layer * nsteps + i, 0)),
           full(w4.shape), full(bias_c.shape), full(bias_w.shape)],
        out_specs=(pl.BlockSpec((SN, 1, NSA_WIDTH), lambda i, pt: (i, 0, 0)),
                   pl.BlockSpec((SN, NSA_KV_HEADS, N_SELECT), lambda i, pt: (i, 0, 0)),
                   pl.BlockSpec((SN * wl, HEAD_DIM), lambda i, pt: (i, 0))),
        scratch_shapes=[pltpu.VMEM((SN, CMP_PAD * KV_ROWS, HEAD_DIM), F32)])
    return pl.pallas_call(
        functools.partial(_nsa_sample_a_kernel, n_pages, nb_past),
        out_shape=(jax.ShapeDtypeStruct((nbatch, 1, NSA_WIDTH), F32),
                   jax.ShapeDtypeStruct((nbatch, NSA_KV_HEADS, N_SELECT), I32),
                   jax.ShapeDtypeStruct((nbatch * wl, HEAD_DIM), F32)),
        grid_spec=gs,
        compiler_params=_params(("arbitrary",)),
        name="nsa_sample_a",
    )(page_table, proj_s3, proj_s3, proj_s3, proj_s3, *([pool_cmp] * (SN * n_pages)), win_rows, w4, bias_c, bias_w)


def _nsa_sample_b_kernel(nb_past, pt_ref, ix_ref, q_ref, kvs_ref, sm_ref, part_ref, *rest):
    nsel = NSA_KV_HEADS * N_SELECT
    blk_refs = rest[:SN * nsel]
    bs_ref, o_ref = rest[SN * nsel:]
    i0 = pl.program_id(0) * SN
    hg, dh = NSA_GROUP, HEAD_DIM
    blk_rows = CMP_BLOCK * KV_ROWS
    for sb in range(SN):
        gates = _sigmoid(sm_ref[sb, :, SM_GATE:SM_GATE + 3 * NSA_HEADS])
        new_blk = jnp.concatenate([_kv_rows(kvs_ref, sb), jnp.zeros((blk_rows - KV_ROWS, dh), F32)], axis=0)
        for g in range(NSA_KV_HEADS):
            q16 = _head_rows(q_ref, sb, g * hg, (g + 1) * hg)
            scores, keys = [], []
            for i in range(N_SELECT):
                idx = ix_ref[i0 + sb, g * N_SELECT + i]
                past_blk = blk_refs[sb * nsel + g * N_SELECT + i][...]
                k16 = jnp.where(idx >= nb_past, new_blk, past_blk).astype(BF16)
                bt = bs_ref[jnp.clip(idx, 0, nb_past), g * hg:(g + 1) * hg, :]
                scores.append(_bdot(q16, k16, _NT) + bt)
                keys.append(k16)
            p = _masked_softmax_rows(jnp.concatenate(scores, axis=1))
            o_s = _bdot(_value_weights(p), jnp.concatenate(keys, axis=0))
            for h in range(hg):
                c0 = (g * hg + h) * 3
                cs = slice((g * hg + h) * dh, (g * hg + h + 1) * dh)
                o_ref[sb, :, cs] = part_ref[sb, :, cs] + o_s[h:h + 1] * gates[:, c0 + 1:c0 + 2]


def _nsa_sample_b(page_table, top_idx, proj_s3, part, pool_sel, bias_s, layer, n_phys):
    nbatch, n_pages = page_table.shape
    nb_past = n_pages * PAGE_SIZE // CMP_BLOCK
    bpp = PAGE_SIZE // CMP_BLOCK
    nsel = NSA_KV_HEADS * N_SELECT
    col = lambda w, c: pl.BlockSpec((SN, 1, w), lambda i, pt, ix: (i, 0, c // w))

    def blk(sb, gi):
        def imap(i, pt, ix):
            b = SN * i + sb
            ip = jnp.clip(ix[b, gi], 0, nb_past - 1)
            return ((layer * n_phys + pt[b, ip // bpp]) * bpp + ip % bpp, 0)
        return pl.BlockSpec((CMP_BLOCK * KV_ROWS, HEAD_DIM), imap)

    gs = pltpu.PrefetchScalarGridSpec(
        num_scalar_prefetch=2, grid=(nbatch // SN,),
        in_specs=[col(NSA_WIDTH, C_NQ), col(KVW, C_KVS), col(128, C_SM),
                  pl.BlockSpec((SN, 1, NSA_WIDTH), lambda i, pt, ix: (i, 0, 0))]
        + [blk(sb, gi) for sb in range(SN) for gi in range(nsel)]
        + [pl.BlockSpec(bias_s.shape, lambda i, pt, ix: (0, 0, 0))],
        out_specs=pl.BlockSpec((SN, 1, NSA_WIDTH), lambda i, pt, ix: (i, 0, 0)))
    return pl.pallas_call(
        functools.partial(_nsa_sample_b_kernel, nb_past),
        out_shape=jax.ShapeDtypeStruct((nbatch, 1, NSA_WIDTH), F32),
        grid_spec=gs,
        compiler_params=_params(("arbitrary",)),
        name="nsa_sample_b",
    )(page_table, top_idx, proj_s3, proj_s3, proj_s3, part, *([pool_sel] * (SN * nsel)), bias_s)


SB = 8


def _rec_sample_kernel(gq_ref, gk_ref, gv_ref, gg_ref, x_ref, dz_ref, sm_ref, sg_ref, sd_ref, cb_ref,
                       w2_ref, b2_ref, gnw_ref, cw_ref, al_ref, dt_ref, dnw_ref,
                       o_ref, sgo_ref, sdo_ref):
    sm = sm_ref[...]
    ri = lax.broadcasted_iota(I32, (SB, 128), 0)
    log_a = _log_sigmoid(_bdot(sm[:, SM_GLR:SM_GLR + GLA_GATE_RANK], w2_ref[...]) + b2_ref[...]) / GLA_GATE_NORM
    ea_t = jnp.exp(log_a).T
    k_t = gk_ref[...].T
    q_t = (gq_ref[...] * (GLA_DK ** -0.5)).T
    gv = gv_ref[...]
    for h in range(GLA_HEADS):
        hs = slice(h * GLA_DK, (h + 1) * GLA_DK)
        vs = slice(h * GLA_DV, (h + 1) * GLA_DV)
        o_h = jnp.zeros((SB, GLA_DV), F32)
        for i in range(SB):
            s = ea_t[hs, i:i + 1] * sg_ref[i, h] + k_t[hs, i:i + 1] * gv[i:i + 1, vs]
            sgo_ref[i, h] = s
            o = jnp.sum(q_t[hs, i:i + 1] * s, axis=0, keepdims=True)
            o_h = jnp.where(ri == i, o, o_h)
        o_ref[:, vs] = _rms(o_h, gnw_ref[...]) * _silu(gg_ref[:, vs])
    y = x_ref[...] * cw_ref[CONV_W - 1:CONV_W, :]
    for jw in range(CONV_W - 1):
        y = y + cb_ref[jw] * cw_ref[jw:jw + 1, :]
    y = _silu(y)
    beta = _sigmoid(sm[:, SM_DB:SM_DB + GDN_HEADS])
    eg = jnp.exp(-jnp.exp(al_ref[...]) * _softplus(sm[:, SM_DA:SM_DA + GDN_HEADS] + dt_ref[...]))
    dk, dv = GDN_DK, GDN_DV
    for h in range(GDN_HEADS):
        cq = y[:, h * dk:(h + 1) * dk]
        ck = y[:, GDN_WIDTH + h * dk:GDN_WIDTH + (h + 1) * dk]
        v = y[:, 2 * GDN_WIDTH + h * dv:2 * GDN_WIDTH + (h + 1) * dv]
        q_t = (cq * lax.rsqrt(jnp.sum(cq * cq, axis=-1, keepdims=True) + 1e-6) * (dk ** -0.5)).T
        k_t = (ck * lax.rsqrt(jnp.sum(ck * ck, axis=-1, keepdims=True) + 1e-6)).T
        o_h = jnp.zeros((SB, dv), F32)
        for i in range(SB):
            s = eg[i:i + 1, h:h + 1] * sd_ref[i, h]
            kc = k_t[:, i:i + 1]
            delta = (v[i:i + 1] - jnp.sum(kc * s, axis=0, keepdims=True)) * beta[i:i + 1, h:h + 1]
            s = s + kc * delta
            sdo_ref[i, h] = s
            o = jnp.sum(q_t[:, i:i + 1] * s, axis=0, keepdims=True)
            o_h = jnp.where(ri == i, o, o_h)
        vs = slice(GLA_WIDTH + h * dv, GLA_WIDTH + (h + 1) * dv)
        o_ref[:, vs] = _rms(o_h, dnw_ref[...]) * _silu(dz_ref[:, h * dv:(h + 1) * dv])


def _rec_sample(proj_s, state_gla, state_gdn, conv_t, w2, b2, gnw, cw, a_log, dt_bias, dnw, layer):
    nbatch = proj_s.shape[0]
    nblk = nbatch // SB
    row = lambda w, col: pl.BlockSpec((SB, w), lambda i: (i, col // w))
    full = lambda shape: pl.BlockSpec(shape, lambda i: (0,) * len(shape))
    return pl.pallas_call(
        _rec_sample_kernel,
        out_shape=(jax.ShapeDtypeStruct((nbatch, GLA_WIDTH + GDN_WIDTH), F32),
                   jax.ShapeDtypeStruct((nbatch, GLA_HEADS, GLA_DK, GLA_DV), F32),
                   jax.ShapeDtypeStruct((nbatch, GDN_HEADS, GDN_DK, GDN_DV), F32)),
        grid=(nblk,),
        in_specs=[row(GLA_HEADS * GLA_DK, C_GQ), row(GLA_HEADS * GLA_DK, C_GK), row(GLA_WIDTH, C_GV),
                  row(GLA_WIDTH, C_GG), row(3 * GDN_WIDTH, C_DQKV), row(GDN_WIDTH, C_DZ), row(128, C_SM),
                  pl.BlockSpec((SB, GLA_HEADS, GLA_DK, GLA_DV), lambda i: (layer * nblk + i, 0, 0, 0)),
                  pl.BlockSpec((SB, GDN_HEADS, GDN_DK, GDN_DV), lambda i: (layer * nblk + i, 0, 0, 0)),
                  pl.BlockSpec((CONV_W - 1, SB, 3 * GDN_WIDTH), lambda i: (0, i, 0)),
                  full(w2.shape), full(b2.shape), full(gnw.shape), full(cw.shape), full(a_log.shape),
                  full(dt_bias.shape), full(dnw.shape)],
        out_specs=(pl.BlockSpec((SB, GLA_WIDTH + GDN_WIDTH), lambda i: (i, 0)),
                   pl.BlockSpec((SB, GLA_HEADS, GLA_DK, GLA_DV), lambda i: (i, 0, 0, 0)),
                   pl.BlockSpec((SB, GDN_HEADS, GDN_DK, GDN_DV), lambda i: (i, 0, 0, 0))),
        compiler_params=_params(("parallel",)),
        name="rec_sample",
    )(proj_s, proj_s, proj_s, proj_s, proj_s, proj_s, proj_s, state_gla, state_gdn, conv_t,
      w2, b2, gnw, cw, a_log, dt_bias, dnw)


def _reorder_w_in(w):
    d = w.shape[0]
    return jnp.concatenate([
        w[:, 0:2560],
        w[:, 2584:3096],
        w[:, 4136:5672],
        w[:, 3096:3608],
        w[:, 3624:4136],
        w[:, 5672:6184],
        w[:, 2560:2584],
        w[:, 3608:3624],
        w[:, 6184:6192],
        jnp.zeros((d, PROJ_N - 6192), w.dtype)], axis=1)


def _cmp_weight_tile(w_cmp):
    half = NSA_KV_HEADS * HEAD_DIM
    return jnp.concatenate([jnp.broadcast_to(w_cmp[:, 0:1], (CMP_BLOCK, half)),
                            jnp.broadcast_to(w_cmp[:, 1:2], (CMP_BLOCK, half))], axis=1).astype(F32)


def _cmp_weight_rows(w_cmp):
    w = jnp.repeat(w_cmp, NSA_KV_HEADS, axis=1).reshape(CMP_BLOCK * KV_ROWS, 1)
    return jnp.broadcast_to(w, (CMP_BLOCK * KV_ROWS, HEAD_DIM)).astype(F32)


def kernel(x_prompt, x_sample, cache_cmp, cache_sel, cache_win, state_gla, state_gdn, state_conv, page_table,
           p_prompt, p_sample, norm_w, ffn_w1, ffn_w3, ffn_w2, w_in, w_out, nsa_w_cmp, t5_bias,
           gla_w_gk2, gla_b_gk, gla_norm_w, gdn_conv_w, gdn_a_log, gdn_dt_bias, gdn_norm_w,
           ple_w_proj, ple_w_gate):
    depth = w_in.shape[0]
    batch, seq, d = x_prompt.shape
    nbatch = x_sample.shape[0]
    n_phys = cache_cmp.shape[1]
    n_pages = page_table.shape[1]
    past = n_pages * PAGE_SIZE
    win_len = cache_win.shape[2]
    rows_p = batch * seq
    kv_shape = (2, NSA_KV_HEADS, HEAD_DIM)

    x = jnp.concatenate([x_prompt.reshape(rows_p, d), x_sample.reshape(nbatch, d)], axis=0)
    assert win_len == WINDOW and past // CMP_BLOCK + 1 >= N_SELECT and past // CMP_BLOCK < CMP_PAD
    pool_cmp = cache_cmp.reshape(-1, HEAD_DIM)
    pool_sel = cache_sel.reshape(-1, HEAD_DIM)
    win_all = cache_win.reshape(-1, HEAD_DIM)
    sgla_all = state_gla.reshape((depth * nbatch,) + state_gla.shape[2:])
    sgdn_all = state_gdn.reshape((depth * nbatch,) + state_gdn.shape[2:])

    pb_c, pb_s, pb_w = _prompt_bias_tables(t5_bias, seq)
    sb_c, sb_s, sb_w = _sample_bias_tables(t5_bias, past, win_len)
    expand = jnp.asarray(np.repeat(np.eye(seq // CMP_BLOCK, dtype=np.float32), CMP_BLOCK, axis=1), BF16)

    outs = {k: [] for k in ("cmp_p", "cmp_s", "sel_p", "sel_s", "win_p", "win_s",
                            "gla_p", "gla_s", "gdn_p", "gdn_s", "conv_p", "conv_s")}
    for l in range(depth):
        nw = norm_w[l].reshape(-1, 1, d)
        w1 = ffn_w1[l].astype(BF16)
        w3 = ffn_w3[l].astype(BF16)
        w2 = ffn_w2[l].astype(BF16)
        x = _ffn(x, nw[0], w1[0], w3[0], w2[0], nw[1])

        proj = _proj(x, nw[2], _reorder_w_in(w_in[l].astype(BF16)))
        proj_s = proj[rows_p:]
        proj_s3 = proj_s.reshape(nbatch, 1, PROJ_N)
        wtile = _cmp_weight_tile(nsa_w_cmp[l])
        w_gk2 = gla_w_gk2[l]
        b_gk = gla_b_gk[l].reshape(1, -1)
        gla_nw = gla_norm_w[l].reshape(1, -1)
        gdn_nw = gdn_norm_w[l].reshape(1, -1)
        conv_w = gdn_conv_w[l]
        a_log = gdn_a_log[l].reshape(1, -1)
        dt_bias = gdn_dt_bias[l].reshape(1, -1)

        cmpkv = _compress_prompt(proj, wtile, rows_p)
        kvs16 = proj[:rows_p, C_KVS:C_KVS + KVW].astype(BF16)
        kvw16 = jnp.pad(proj[:rows_p, C_KVW:C_KVW + KVW].astype(BF16).reshape(batch, seq, KVW),
                        ((0, 0), (WINDOW, 0), (0, 0))).reshape(batch * (WINDOW + seq), KVW)
        o_nsa = _nsa_prompt(proj, cmpkv, kvs16, kvw16, pb_c, pb_s, pb_w, expand, batch, seq)
        o_gla, s_gla_p = _gla_prompt(proj, w_gk2, b_gk, gla_nw, batch, seq)
        o_gdn, s_gdn_p = _gdn_prompt(proj, conv_w, a_log, dt_bias, gdn_nw, batch, seq)

        part, top_idx, win_new = _nsa_sample_a(page_table, proj_s3, pool_cmp, win_all, _cmp_weight_rows(nsa_w_cmp[l]),
                                               sb_c, sb_w, l, n_phys)
        o_nsa_s = _nsa_sample_b(page_table, top_idx.reshape(nbatch, NSA_KV_HEADS * N_SELECT), proj_s3, part,
                                pool_sel, sb_s, l, n_phys)
        conv_t = jnp.swapaxes(state_conv[l], 0, 1)
        o_rec_s, s_gla_s, s_gdn_s = _rec_sample(proj_s, sgla_all, sgdn_all, conv_t, w_gk2, b_gk, gla_nw,
                                                conv_w, a_log, dt_bias, gdn_nw, l)

        y = jnp.concatenate([jnp.concatenate([o_nsa, o_gla, o_gdn], axis=1),
                             jnp.concatenate([o_nsa_s.reshape(nbatch, NSA_WIDTH), o_rec_s], axis=1)], axis=0)
        x = _outproj(y, x, w_out[l].astype(BF16), nw[3])
        x = _ffn(x, nw[4], w1[1], w3[1], w2[1], nw[5])
        p = jnp.concatenate([p_prompt[l].reshape(rows_p, -1), p_sample[l].reshape(nbatch, -1)], axis=0)
        x = _ple(x, p, nw[6], ple_w_gate[l].astype(BF16), ple_w_proj[l].astype(BF16), nw[7])

        pp = proj[:rows_p]
        outs["cmp_p"].append(pp[:, C_KVC:C_KVC + KVW].reshape((batch, seq) + kv_shape))
        outs["sel_p"].append(pp[:, C_KVS:C_KVS + KVW].reshape((batch, seq) + kv_shape))
        wp = min(WINDOW, seq)
        outs["win_p"].append(pp[:, C_KVW:C_KVW + KVW].reshape((batch, seq) + kv_shape)[:, seq - wp:])
        outs["cmp_s"].append(proj_s[:, C_KVC:C_KVC + KVW].reshape((nbatch, 1) + kv_shape))
        outs["sel_s"].append(proj_s[:, C_KVS:C_KVS + KVW].reshape((nbatch, 1) + kv_shape))
        outs["win_s"].append(win_new.reshape((nbatch, win_len) + kv_shape))
        outs["gla_p"].append(s_gla_p)
        outs["gla_s"].append(s_gla_s)
        outs["gdn_p"].append(s_gdn_p)
        outs["gdn_s"].append(s_gdn_s)
        dqkv_p = pp[:, C_DQKV:C_DQKV + 3 * GDN_WIDTH].reshape(batch, seq, 3 * GDN_WIDTH)
        outs["conv_p"].append(dqkv_p[:, seq - (CONV_W - 1):])
        outs["conv_s"].append(jnp.concatenate(
            [state_conv[l][:, 1:], proj_s[:, None, C_DQKV:C_DQKV + 3 * GDN_WIDTH]], axis=1))

    st = lambda k: jnp.stack(outs[k])
    return (x[:rows_p].reshape(batch, seq, d), x[rows_p:].reshape(nbatch, 1, d),
            st("cmp_p"), st("cmp_s"), st("sel_p"), st("sel_s"), st("win_p"), st("win_s"),
            st("gla_p"), st("gla_s"), st("gdn_p"), st("gdn_s"), st("conv_p"), st("conv_s"))
```

```python
import functools
import math

import numpy as np
import jax
import jax.numpy as jnp
from jax import lax
from jax.experimental import pallas as pl
from jax.experimental.pallas import tpu as pltpu

F32 = jnp.float32
BF16 = jnp.bfloat16
I32 = jnp.int32

D_MODEL = 2048
HEAD_DIM = 128
NSA_HEADS = 8
NSA_KV_HEADS = 2
NSA_GROUP = NSA_HEADS // NSA_KV_HEADS
CMP_BLOCK = 64
N_SELECT = 8
WINDOW = 512
Q_BLOCK = 128
SEL_FORCE = 1.0e4
GLA_HEADS = 4
GLA_DK = 64
GLA_DV = 128
GLA_GATE_RANK = 16
GLA_GATE_NORM = 16.0
GDN_HEADS = 4
GDN_DK = 128
GDN_DV = 128
CHUNK = 64
CONV_W = 4
N_BUCKETS = 32
T5_MAX_DIST = 128
D_FF = 5632
PLE_DIM = 256
RMS_EPS = 1e-6
NEG = -1e30
MASK_BIG = 2.0 ** 100
PAGE_SIZE = 128

NSA_WIDTH = NSA_HEADS * HEAD_DIM
KVW = 2 * NSA_KV_HEADS * HEAD_DIM
KV_ROWS = 2 * NSA_KV_HEADS
CMP_PAD = 64
GDN_WIDTH = GDN_HEADS * GDN_DV
GLA_WIDTH = GLA_HEADS * GLA_DV

C_NQ = 0
C_KVC = 1024
C_KVS = 1536
C_KVW = 2048
C_GQ = 2560
C_GK = 2816
C_DQKV = 3072
C_GV = 4608
C_GG = 5120
C_DZ = 5632
C_SM = 6144
PROJ_N = 6272
SM_GATE, SM_GLR, SM_DB, SM_DA = 0, 24, 40, 44

VMEM_LIMIT = 56 * 1024 * 1024
TM = 640
TF = 512
TN_PROJ = 896
TB = 256


def _sigmoid(x):
    return 1.0 / (1.0 + jnp.exp(-x))


def _silu(x):
    return x * _sigmoid(x)


def _softplus(x):
    return jnp.maximum(x, 0.0) + jnp.log(1.0 + jnp.exp(-jnp.abs(x)))


def _log_sigmoid(x):
    return jnp.minimum(x, 0.0) - jnp.log(1.0 + jnp.exp(-jnp.abs(x)))


def _rms(x, w):
    return x * lax.rsqrt(jnp.mean(x * x, axis=-1, keepdims=True) + RMS_EPS) * w


_NN = (((1,), (0,)), ((), ()))
_NT = (((1,), (1,)), ((), ()))
_TN = (((0,), (0,)), ((), ()))


def _bdot(a, b, dims=_NN):
    return lax.dot_general(a.astype(BF16), b.astype(BF16), dims, preferred_element_type=F32)


def _hdot(a, b, dims=_NN):
    return lax.dot_general(a, b, dims, preferred_element_type=F32, precision=lax.Precision.HIGHEST)


def _params(sem):
    return pltpu.CompilerParams(dimension_semantics=sem, vmem_limit_bytes=VMEM_LIMIT)


def _ffn_kernel(x_ref, nwa_ref, w1_ref, w3_ref, w2_ref, nwb_ref, o_ref, h_ref, acc_ref):
    f = pl.program_id(1)

    @pl.when(f == 0)
    def _():
        h_ref[...] = _rms(x_ref[...], nwa_ref[...]).astype(BF16)
        acc_ref[...] = jnp.zeros_like(acc_ref)

    h = h_ref[...]
    a = jnp.dot(h, w1_ref[...], preferred_element_type=F32)
    b = jnp.dot(h, w3_ref[...], preferred_element_type=F32)
    g = (_silu(a) * b).astype(BF16)
    acc_ref[...] += jnp.dot(g, w2_ref[...], preferred_element_type=F32)

    @pl.when(f == pl.num_programs(1) - 1)
    def _():
        o_ref[...] = x_ref[...] + 0.5 * _rms(acc_ref[...], nwb_ref[...])


def _ffn(x, nwa, w1, w3, w2, nwb):
    m, d = x.shape
    dff = w1.shape[1]
    row = lambda i, f: (i, 0)
    return pl.pallas_call(
        _ffn_kernel,
        out_shape=jax.ShapeDtypeStruct((m, d), F32),
        grid=(m // TM, dff // TF),
        in_specs=[pl.BlockSpec((TM, d), row),
                  pl.BlockSpec((1, d), lambda i, f: (0, 0)),
                  pl.BlockSpec((d, TF), lambda i, f: (0, f)),
                  pl.BlockSpec((d, TF), lambda i, f: (0, f)),
                  pl.BlockSpec((TF, d), lambda i, f: (f, 0)),
                  pl.BlockSpec((1, d), lambda i, f: (0, 0))],
        out_specs=pl.BlockSpec((TM, d), row),
        scratch_shapes=[pltpu.VMEM((TM, d), BF16), pltpu.VMEM((TM, d), F32)],
        compiler_params=_params(("parallel", "arbitrary")),
        name="ffn",
    )(x, nwa, w1, w3, w2, nwb)


def _proj_kernel(x_ref, nw_ref, w_ref, o_ref, h_ref):
    @pl.when(pl.program_id(1) == 0)
    def _():
        h_ref[...] = _rms(x_ref[...], nw_ref[...]).astype(BF16)

    o_ref[...] = jnp.dot(h_ref[...], w_ref[...], preferred_element_type=F32)


def _proj(x, nw, w):
    m, d = x.shape
    n = w.shape[1]
    return pl.pallas_call(
        _proj_kernel,
        out_shape=jax.ShapeDtypeStruct((m, n), F32),
        grid=(m // TM, n // TN_PROJ),
        in_specs=[pl.BlockSpec((TM, d), lambda i, j: (i, 0)),
                  pl.BlockSpec((1, d), lambda i, j: (0, 0)),
                  pl.BlockSpec((d, TN_PROJ), lambda i, j: (0, j))],
        out_specs=pl.BlockSpec((TM, TN_PROJ), lambda i, j: (i, j)),
        scratch_shapes=[pltpu.VMEM((TM, d), BF16)],
        compiler_params=_params(("parallel", "arbitrary")),
        name="proj",
    )(x, nw, w)


def _outproj_kernel(y_ref, x_ref, w_ref, nw_ref, o_ref):
    z = jnp.dot(y_ref[...].astype(BF16), w_ref[...], preferred_element_type=F32)
    o_ref[...] = x_ref[...] + _rms(z, nw_ref[...])


def _outproj(y, x, w, nw):
    m, d = x.shape
    k = y.shape[1]
    return pl.pallas_call(
        _outproj_kernel,
        out_shape=jax.ShapeDtypeStruct((m, d), F32),
        grid=(m // TM,),
        in_specs=[pl.BlockSpec((TM, k), lambda i: (i, 0)),
                  pl.BlockSpec((TM, d), lambda i: (i, 0)),
                  pl.BlockSpec((k, d), lambda i: (0, 0)),
                  pl.BlockSpec((1, d), lambda i: (0, 0))],
        out_specs=pl.BlockSpec((TM, d), lambda i: (i, 0)),
        compiler_params=_params(("parallel",)),
        name="outproj",
    )(y, x, w, nw)


def _ple_kernel(split, x_ref, p_ref, nwa_ref, wg_ref, wp_ref, nwb_ref, *o_refs):
    x = x_ref[...]
    gate = _sigmoid(jnp.dot(_rms(x, nwa_ref[...]).astype(BF16), wg_ref[...], preferred_element_type=F32))
    pp = jnp.dot(p_ref[...].astype(BF16), wp_ref[...], preferred_element_type=F32)
    y = x + _rms(gate * pp, nwb_ref[...])
    o_refs[0][...] = y
    if split is not None:
        @pl.when(pl.program_id(0) == pl.num_programs(0) - 1)
        def _():
            o_refs[1][...] = y[split:]


def _ple(x, p, nwa, wg, wp, nwb, split_rows=None):
    m, d = x.shape
    pd = p.shape[1]
    nblk = m // TM
    out_shape = jax.ShapeDtypeStruct((m, d), F32)
    out_specs = pl.BlockSpec((TM, d), lambda i: (i, 0))
    split = None
    if split_rows is not None:
        split = split_rows - (nblk - 1) * TM
        assert 0 < split and m - split_rows == TM - split
        out_shape = (jax.ShapeDtypeStruct((split_rows, d), F32), jax.ShapeDtypeStruct((m - split_rows, d), F32))
        out_specs = (out_specs, pl.BlockSpec((m - split_rows, d), lambda i: (0, 0)))
    return pl.pallas_call(
        functools.partial(_ple_kernel, split),
        out_shape=out_shape,
        grid=(nblk,),
        in_specs=[pl.BlockSpec((TM, d), lambda i: (i, 0)),
                  pl.BlockSpec((TM, pd), lambda i: (i, 0)),
                  pl.BlockSpec((1, d), lambda i: (0, 0)),
                  pl.BlockSpec((d, d), lambda i: (0, 0)),
                  pl.BlockSpec((pd, d), lambda i: (0, 0)),
                  pl.BlockSpec((1, d), lambda i: (0, 0))],
        out_specs=out_specs,
        compiler_params=_params(("arbitrary",)),
        name="ple",
    )(x, p, nwa, wg, wp, nwb)


def _t5_bucket_np(dist):
    n = np.maximum(dist, 0)
    exact = N_BUCKETS // 2
    val = (np.log(np.maximum(n, 1).astype(np.float32) / np.float32(exact))
           / np.float32(math.log(T5_MAX_DIST / exact)) * np.float32(N_BUCKETS - exact))
    large = exact + val.astype(np.int32)
    return np.where(n < exact, n, np.minimum(large, N_BUCKETS - 1)).astype(np.int32)


def _bias_table(t5_bias, dist, valid):
    bucket = jnp.asarray(np.where(valid, _t5_bucket_np(dist), -1).astype(np.int32))[None]
    hshape = (t5_bias.shape[1],) + (1,) * dist.ndim
    out = jnp.full((t5_bias.shape[1],) + dist.shape, NEG, F32)
    for k in range(N_BUCKETS):
        out = jnp.where(bucket == k, t5_bias[k].reshape(hshape), out)
    return out


def _prompt_bias_tables(t5_bias, seq):
    nb = seq // CMP_BLOCK
    qpos = np.arange(seq)[:, None]
    d_c = qpos - (np.arange(nb) * CMP_BLOCK + CMP_BLOCK - 1)[None, :]
    bias_c = _bias_table(t5_bias, d_c, d_c >= 0)
    i = np.arange(Q_BLOCK)[:, None]
    jj = np.arange(Q_BLOCK)[None, :]
    d_s = np.stack([Q_BLOCK * dl + i - jj for dl in range(3)] + [i - jj - Q_BLOCK])
    bias_s = jnp.transpose(_bias_table(t5_bias, d_s, d_s >= 0), (1, 0, 2, 3))
    nwc = WINDOW // Q_BLOCK + 1
    d_w = np.stack([i + WINDOW - Q_BLOCK * cw - jj for cw in range(nwc)] + [i - jj - Q_BLOCK])
    bias_w = jnp.transpose(_bias_table(t5_bias, d_w, (d_w >= 0) & (d_w < WINDOW)), (1, 0, 2, 3))
    return bias_c, bias_s, bias_w


def _sample_bias_tables(t5_bias, past, win_len):
    nb_past = past // CMP_BLOCK
    head_g = (np.arange(NSA_HEADS) // NSA_GROUP)[:, None]

    def per_head(dist, valid):
        lane_cg = np.arange(dist.shape[-1]) % KV_ROWS
        tbl = _bias_table(t5_bias, dist, valid)
        own = jnp.asarray(lane_cg[None] == head_g)
        return tbl, own

    lane = np.arange(CMP_PAD * KV_ROWS)
    n = lane // KV_ROWS
    d_c = past - (n * CMP_BLOCK + CMP_BLOCK - 1)
    tbl, own = per_head(d_c, (d_c >= 0) & (n <= nb_past))
    bias_c = jnp.where(own, tbl, NEG)
    lane = np.arange(CMP_BLOCK * KV_ROWS)
    blk = np.arange(nb_past + 1)[:, None]
    d_s = past - (blk * CMP_BLOCK + (lane // KV_ROWS)[None, :])
    tbl, own = per_head(d_s, d_s >= 0)
    bias_s = jnp.transpose(jnp.where(own[:, None, :], tbl, NEG), (1, 0, 2))
    lane = np.arange(win_len * KV_ROWS)
    wpos = past - win_len + 1 + lane // KV_ROWS
    d_w = past - wpos
    tbl, own = per_head(d_w, (d_w >= 0) & (d_w < WINDOW) & (wpos >= 0))
    bias_w = jnp.where(own, tbl, NEG)
    return bias_c, bias_s, bias_w


def _compress_kernel(x_ref, w_ref, o_ref):
    r = x_ref.shape[0] // CMP_BLOCK
    x = x_ref[...].reshape(r, CMP_BLOCK, KVW)
    o_ref[...] = jnp.sum(x * w_ref[...][None], axis=1)


def _compress_prompt(proj, wtile, rows):
    rb = 512
    return pl.pallas_call(
        _compress_kernel,
        out_shape=jax.ShapeDtypeStruct((rows // CMP_BLOCK, KVW), F32),
        grid=(rows // rb,),
        in_specs=[pl.BlockSpec((rb, KVW), lambda i: (i, C_KVC // KVW)),
                  pl.BlockSpec((CMP_BLOCK, KVW), lambda i: (0, 0))],
        out_specs=pl.BlockSpec((rb // CMP_BLOCK, KVW), lambda i: (i, 0)),
        compiler_params=_params(("parallel",)),
        name="compress_prompt",
    )(proj, wtile)


KEY_CHUNK = 2 * Q_BLOCK


def _nsa_prompt_kernel(q_ref, sm_ref, cmp_ref, kvs_ref, kvw_ref, bc_ref, bs_ref, bw_ref, e_ref, o_ref,
                       s_ref, mx_ref, l_ref, acc_ref):
    j = pl.program_id(1)
    nb = cmp_ref.shape[0]
    hg, qb, dh, kc = NSA_GROUP, Q_BLOCK, HEAD_DIM, KEY_CHUNK
    rows = hg * qb
    nwt = bw_ref.shape[0] - 1
    gates = _sigmoid(sm_ref[:, SM_GATE:SM_GATE + 3 * NSA_HEADS])
    qi = lax.broadcasted_iota(I32, (qb, nb), 0)
    ni = lax.broadcasted_iota(I32, (qb, nb), 1)
    cur = (qb // CMP_BLOCK) * j + qi // CMP_BLOCK
    forced = (ni == cur) | (ni == cur - 1) | (ni == 0)
    started = ni <= cur
    groups = range(NSA_KV_HEADS)
    kcols = [slice(g * dh, (g + 1) * dh) for g in groups]
    vcols = [slice((NSA_KV_HEADS + g) * dh, (NSA_KV_HEADS + g + 1) * dh) for g in groups]
    head_rows = [slice(g * hg, (g + 1) * hg) for g in groups]

    def fold(x, op):
        out = x[:, 0:qb]
        for t in range(1, x.shape[1] // qb):
            out = op(out, x[:, t * qb:(t + 1) * qb])
        return out

    q16, o_c, score = [], [], []
    for g in groups:
        q = jnp.concatenate([q_ref[:, (g * hg + h) * dh:(g * hg + h + 1) * dh] for h in range(hg)], axis=0)
        q16.append((q * (dh ** -0.5)).astype(BF16))
        bias = bc_ref[head_rows[g]].reshape(rows, nb)
        valid = bias > -1e29
        s = jnp.where(valid, _bdot(q16[g], cmp_ref[:, kcols[g]], _NT) + bias, NEG)
        p = jnp.where(valid, jnp.exp(s - jnp.max(s, axis=-1, keepdims=True)), 0.0)
        p = p / jnp.maximum(jnp.sum(p, axis=-1, keepdims=True), 1e-30)
        o_c.append(_bdot(p, cmp_ref[:, vcols[g]]))
        imp = p[0:qb]
        for h in range(1, hg):
            imp = imp + p[h * qb:(h + 1) * qb]
        score.append(jnp.where(started, jnp.where(forced, SEL_FORCE, imp), -1.0))

    unsel = [jnp.ones((qb, nb), F32) for _ in groups]
    for _ in range(min(N_SELECT, nb)):
        for g in groups:
            hit = ni == jnp.argmax(score[g], axis=-1, keepdims=True).astype(I32)
            unsel[g] = jnp.where(hit & started, 0.0, unsel[g])
            score[g] = jnp.where(hit, -3e38, score[g])
    unsel = [u.astype(BF16) for u in unsel]

    mx_ref[...] = jnp.full(mx_ref.shape, NEG, F32)
    l_ref[...] = jnp.zeros(l_ref.shape, F32)
    acc_ref[...] = jnp.zeros(acc_ref.shape, F32)
    n_chunks = j // (kc // qb) + 1

    def key_rows(c):
        return pl.ds(pl.multiple_of(c * kc, kc), kc)

    def sel_bias(g, c):
        tiles = []
        for t in range(kc // qb):
            back = j - (c * (kc // qb) + t)
            tiles.append(bs_ref[jnp.where(back < 0, 3, jnp.minimum(back, 2)), head_rows[g]].reshape(rows, qb))
        return jnp.concatenate(tiles, axis=1)

    def pass1(c, carry):
        for g in groups:
            key_mask = jnp.dot(unsel[g], e_ref[c], preferred_element_type=F32)
            s_ = (_bdot(q16[g], kvs_ref[key_rows(c), kcols[g]], _NT) + sel_bias(g, c)
                  + jnp.concatenate([key_mask] * hg, axis=0))
            s_ref[g, c] = s_
            mx_ref[g] = jnp.maximum(mx_ref[g], fold(s_, jnp.maximum))
        return carry

    lax.fori_loop(0, n_chunks, pass1, 0)
    for g in groups:
        mx_ref[g] = jnp.broadcast_to(jnp.max(mx_ref[g], axis=-1, keepdims=True), mx_ref.shape[1:])

    def pass2(c, carry):
        for g in groups:
            p_ = jnp.exp(s_ref[g, c] - jnp.concatenate([mx_ref[g]] * (kc // qb), axis=1))
            l_ref[g] += fold(p_, jnp.add)
            acc_ref[g] += _bdot(p_, kvs_ref[key_rows(c), vcols[g]])
        return carry

    lax.fori_loop(0, n_chunks, pass2, 0)
    o_s = [acc_ref[g] / jnp.maximum(jnp.sum(l_ref[g], axis=-1, keepdims=True), 1e-30) for g in groups]

    w_rows = pl.ds(pl.multiple_of(j * qb, qb), nwt * qb)
    o_w = []
    for g in groups:
        bias = jnp.concatenate(
            [bw_ref[jnp.where(j - (nwt - 1) + t >= 0, t, nwt), head_rows[g]].reshape(rows, qb) for t in range(nwt)],
            axis=1)
        s = _bdot(q16[g], kvw_ref[w_rows, kcols[g]], _NT) + bias
        p = jnp.exp(s - jnp.max(fold(s, jnp.maximum), axis=-1, keepdims=True))
        den = jnp.sum(fold(p, jnp.add), axis=-1, keepdims=True)
        o_w.append(_bdot(p, kvw_ref[w_rows, vcols[g]]) / jnp.maximum(den, 1e-30))

    for g in groups:
        for h in range(hg):
            c0 = SM_GATE + (g * hg + h) * 3
            rs = slice(h * qb, (h + 1) * qb)
            o_ref[:, (g * hg + h) * dh:(g * hg + h + 1) * dh] = (
                o_c[g][rs] * gates[:, c0:c0 + 1] + o_s[g][rs] * gates[:, c0 + 1:c0 + 2]
                + o_w[g][rs] * gates[:, c0 + 2:c0 + 3])


def _nsa_prompt(proj, cmpkv, kvs16, kvw16, bias_c, bias_s, bias_w, expand, batch, seq):
    nqb = seq // Q_BLOCK
    nb = seq // CMP_BLOCK
    rows = NSA_GROUP * Q_BLOCK
    full = lambda shape: pl.BlockSpec(shape, lambda b, j: (0,) * len(shape))
    return pl.pallas_call(
        _nsa_prompt_kernel,
        out_shape=jax.ShapeDtypeStruct((batch * seq, NSA_WIDTH), F32),
        grid=(batch, nqb),
        in_specs=[pl.BlockSpec((Q_BLOCK, NSA_WIDTH), lambda b, j: (b * nqb + j, 0)),
                  pl.BlockSpec((Q_BLOCK, 128), lambda b, j: (b * nqb + j, C_SM // 128)),
                  pl.BlockSpec((nb, KVW), lambda b, j: (b, 0)),
                  pl.BlockSpec((seq, KVW), lambda b, j: (b, 0)),
                  pl.BlockSpec((WINDOW + seq, KVW), lambda b, j: (b, 0)),
                  pl.BlockSpec((NSA_HEADS, Q_BLOCK, nb), lambda b, j: (0, j, 0)),
                  full(bias_s.shape), full(bias_w.shape), full(expand.shape)],
        out_specs=pl.BlockSpec((Q_BLOCK, NSA_WIDTH), lambda b, j: (b * nqb + j, 0)),
        scratch_shapes=[pltpu.VMEM((NSA_KV_HEADS, seq // KEY_CHUNK, rows, KEY_CHUNK), F32),
                        pltpu.VMEM((NSA_KV_HEADS, rows, Q_BLOCK), F32), pltpu.VMEM((NSA_KV_HEADS, rows, Q_BLOCK), F32),
                        pltpu.VMEM((NSA_KV_HEADS, rows, HEAD_DIM), F32)],
        compiler_params=_params(("parallel", "arbitrary")),
        name="nsa_prompt",
    )(proj, proj, cmpkv, kvs16, kvw16, bias_c, bias_s, bias_w, expand)


_BNN = (((2,), (1,)), ((0,), (0,)))
_BNT = (((2,), (2,)), ((0,), (0,)))
_BTN = (((1,), (1,)), ((0,), (0,)))


def _bdot_b(a, b, dims=_BNN):
    return lax.dot_general(a.astype(BF16), b.astype(BF16), dims, preferred_element_type=F32)


def _tri_masks():
    r = lax.broadcasted_iota(I32, (CHUNK, CHUNK), 0)
    c = lax.broadcasted_iota(I32, (CHUNK, CHUNK), 1)
    return r, c


def _gla_prompt_kernel(q_ref, k_ref, v_ref, gg_ref, sm_ref, w2_ref, b2_ref, nw_ref, o_ref, so_ref, s_ref):
    t = pl.program_id(1)

    @pl.when(t == 0)
    def _():
        s_ref[...] = jnp.zeros_like(s_ref)

    r, c = _tri_masks()
    lower = r >= c
    tril = jnp.where(lower, 1.0, 0.0).astype(F32)
    dk, dv = GLA_DK, GLA_DV
    nch = q_ref.shape[0] // CHUNK
    pairs = [(ch, h) for ch in range(nch) for h in range(GLA_HEADS)]
    rs = lambda ch: slice(ch * CHUNK, (ch + 1) * CHUNK)
    stack = lambda fn: jnp.stack([fn(ch, h) for ch, h in pairs], axis=0)

    log_a = _log_sigmoid(_bdot(sm_ref[:, SM_GLR:SM_GLR + GLA_GATE_RANK], w2_ref[...]) + b2_ref[...]) / GLA_GATE_NORM
    width = GLA_HEADS * dk
    b_all = _hdot(tril, jnp.concatenate([log_a[rs(ch)] for ch in range(nch)], axis=1))
    b_all_t = b_all.T
    b = stack(lambda ch, h: b_all[:, ch * width + h * dk:ch * width + (h + 1) * dk])
    b_last = b[:, CHUNK - 1:CHUNK, :]
    b_last_col = stack(lambda ch, h: b_all_t[ch * width + h * dk:ch * width + (h + 1) * dk, CHUNK - 1:CHUNK])
    q = stack(lambda ch, h: q_ref[rs(ch), h * dk:(h + 1) * dk]) * (dk ** -0.5)
    k = stack(lambda ch, h: k_ref[rs(ch), h * dk:(h + 1) * dk])
    v = stack(lambda ch, h: v_ref[rs(ch), h * dv:(h + 1) * dv])
    qe = q * jnp.exp(b)
    att = jnp.where(lower[None], _bdot_b(qe, k * jnp.exp(-b), _BNT), 0.0)
    o_intra = _bdot_b(att, v)
    kv = _bdot_b(k * jnp.exp(b_last - b), v, _BTN)
    s_decay = jnp.exp(b_last_col)

    for p, (ch, h) in enumerate(pairs):
        s = s_ref[h]
        o = _bdot(qe[p], s) + o_intra[p]
        s_ref[h] = s_decay[p] * s + kv[p]
        o_ref[rs(ch), h * dv:(h + 1) * dv] = _rms(o, nw_ref[...]) * _silu(gg_ref[rs(ch), h * dv:(h + 1) * dv])
    so_ref[0] = s_ref[...]


def _gla_prompt(proj, w2, b2, nw, batch, seq):
    nt = seq // TB
    row = lambda w, col: pl.BlockSpec((TB, w), lambda b, t: (b * nt + t, col // w))
    full = lambda shape: pl.BlockSpec(shape, lambda b, t: (0,) * len(shape))
    return pl.pallas_call(
        _gla_prompt_kernel,
        out_shape=(jax.ShapeDtypeStruct((batch * seq, GLA_WIDTH), F32),
                   jax.ShapeDtypeStruct((batch, GLA_HEADS, GLA_DK, GLA_DV), F32)),
        grid=(batch, nt),
        in_specs=[row(GLA_HEADS * GLA_DK, C_GQ), row(GLA_HEADS * GLA_DK, C_GK), row(GLA_WIDTH, C_GV),
                  row(GLA_WIDTH, C_GG), row(128, C_SM), full(w2.shape), full(b2.shape), full(nw.shape)],
        out_specs=(pl.BlockSpec((TB, GLA_WIDTH), lambda b, t: (b * nt + t, 0)),
                   pl.BlockSpec((1, GLA_HEADS, GLA_DK, GLA_DV), lambda b, t: (b, 0, 0, 0))),
        scratch_shapes=[pltpu.VMEM((GLA_HEADS, GLA_DK, GLA_DV), F32)],
        compiler_params=_params(("parallel", "arbitrary")),
        name="gla_prompt",
    )(proj, proj, proj, proj, proj, w2, b2, nw)


def _unit_lower_inverse(m, r, c):
    eye = jnp.where(r == c, 1.0, 0.0).astype(F32)[None]
    base = 8
    m8 = jnp.where(((r // base) == (c // base))[None], m, 0.0)
    m2 = _bdot_b(m8, m8)
    m4 = _bdot_b(m2, m2)
    t = _bdot_b(_bdot_b(eye - m8, eye + m2), eye + m4)
    s = base
    while s < CHUNK:
        off = ((r // (2 * s)) == (c // (2 * s))) & ((r // s) != (c // s))
        t = t - _bdot_b(t, _bdot_b(jnp.where(off[None], m, 0.0), t))
        s *= 2
    return t


def _gdn_prompt_kernel(x_ref, dz_ref, sm_ref, cw_ref, al_ref, dt_ref, nw_ref, o_ref, so_ref, s_ref, tail_ref):
    t = pl.program_id(1)

    @pl.when(t == 0)
    def _():
        s_ref[...] = jnp.zeros_like(s_ref)
        tail_ref[...] = jnp.zeros_like(tail_ref)

    tb = x_ref.shape[0]
    x = x_ref[...]
    xc = jnp.concatenate([tail_ref[...], x], axis=0)
    off = 8 - (CONV_W - 1)
    y = xc[off:off + tb] * cw_ref[0:1, :]
    for jw in range(1, CONV_W):
        y = y + xc[off + jw:off + jw + tb] * cw_ref[jw:jw + 1, :]
    y = _silu(y)
    tail_ref[...] = x[tb - 8:tb]

    beta_all = _sigmoid(sm_ref[:, SM_DB:SM_DB + GDN_HEADS])
    g_all = -jnp.exp(al_ref[...]) * _softplus(sm_ref[:, SM_DA:SM_DA + GDN_HEADS] + dt_ref[...])
    r, c = _tri_masks()
    lower = r >= c
    strict = r > c
    tril = jnp.where(lower, 1.0, 0.0).astype(F32)
    dk, dv = GDN_DK, GDN_DV
    nch = tb // CHUNK
    pairs = [(ch, h) for ch in range(nch) for h in range(GDN_HEADS)]
    rs = lambda ch: slice(ch * CHUNK, (ch + 1) * CHUNK)
    stack = lambda fn: jnp.stack([fn(ch, h) for ch, h in pairs], axis=0)

    cq = stack(lambda ch, h: y[rs(ch), h * dk:(h + 1) * dk])
    ck = stack(lambda ch, h: y[rs(ch), GDN_WIDTH + h * dk:GDN_WIDTH + (h + 1) * dk])
    v = stack(lambda ch, h: y[rs(ch), 2 * GDN_WIDTH + h * dv:2 * GDN_WIDTH + (h + 1) * dv])
    q = cq * lax.rsqrt(jnp.sum(cq * cq, axis=-1, keepdims=True) + 1e-6) * (dk ** -0.5)
    k = ck * lax.rsqrt(jnp.sum(ck * ck, axis=-1, keepdims=True) + 1e-6)
    beta = stack(lambda ch, h: jnp.broadcast_to(beta_all[rs(ch), h:h + 1], (CHUNK, dk)))
    g_cols = jnp.concatenate([g_all[rs(ch)] for ch in range(nch)], axis=1)
    gam_cols = _hdot(tril, g_cols)
    gam_rows = gam_cols.T
    gam = jnp.stack([jnp.broadcast_to(gam_cols[:, p:p + 1], (CHUNK, dk)) for p in range(len(pairs))], axis=0)
    decay = jnp.exp(jnp.where(lower[None], gam[:, :, 0:CHUNK] - gam_rows[:, None, :], NEG))
    kb = k * beta
    m = jnp.where(strict[None], _bdot_b(kb, k, _BNT) * decay, 0.0)
    tinv = _unit_lower_inverse(m, r, c)
    eg = jnp.exp(gam)
    u = _bdot_b(tinv, v * beta)
    w = _bdot_b(tinv, kb * eg)
    att = _bdot_b(q, k, _BNT) * decay
    qe = q * eg
    g_last = gam[:, CHUNK - 1:CHUNK, :]
    kd = k * jnp.exp(g_last - gam)
    eg_last = jnp.exp(g_last)

    for p, (ch, h) in enumerate(pairs):
        s = s_ref[h]
        v_new = u[p] - _bdot(w[p], s)
        o = _bdot(qe[p], s) + _bdot(att[p], v_new)
        s_ref[h] = eg_last[p] * s + _bdot(kd[p], v_new, _TN)
        o_ref[rs(ch), h * dv:(h + 1) * dv] = _rms(o, nw_ref[...]) * _silu(dz_ref[rs(ch), h * dv:(h + 1) * dv])
    so_ref[0] = s_ref[...]


def _gdn_prompt(proj, cw, a_log, dt_bias, nw, batch, seq):
    nt = seq // TB
    row = lambda w, col: pl.BlockSpec((TB, w), lambda b, t: (b * nt + t, col // w))
    full = lambda shape: pl.BlockSpec(shape, lambda b, t: (0,) * len(shape))
    return pl.pallas_call(
        _gdn_prompt_kernel,
        out_shape=(jax.ShapeDtypeStruct((batch * seq, GDN_WIDTH), F32),
                   jax.ShapeDtypeStruct((batch, GDN_HEADS, GDN_DK, GDN_DV), F32)),
        grid=(batch, nt),
        in_specs=[row(3 * GDN_WIDTH, C_DQKV), row(GDN_WIDTH, C_DZ), row(128, C_SM),
                  full(cw.shape), full(a_log.shape), full(dt_bias.shape), full(nw.shape)],
        out_specs=(pl.BlockSpec((TB, GDN_WIDTH), lambda b, t: (b * nt + t, 0)),
                   pl.BlockSpec((1, GDN_HEADS, GDN_DK, GDN_DV), lambda b, t: (b, 0, 0, 0))),
        scratch_shapes=[pltpu.VMEM((GDN_HEADS, GDN_DK, GDN_DV), F32), pltpu.VMEM((8, 3 * GDN_WIDTH), F32)],
        compiler_params=_params(("parallel", "arbitrary")),
        name="gdn_prompt",
    )(proj, proj, proj, cw, a_log, dt_bias, nw)


def _masked_softmax_rows(s):
    valid = s > -1e29
    p = jnp.where(valid, jnp.exp(s - jnp.max(s, axis=-1, keepdims=True)), 0.0)
    return p / jnp.maximum(jnp.sum(p, axis=-1, keepdims=True), 1e-30)


def _head_rows(q_ref, sb, lo, hi):
    dh = HEAD_DIM
    q = jnp.concatenate([q_ref[sb, :, h * dh:(h + 1) * dh] for h in range(lo, hi)], axis=0)
    return (q * (dh ** -0.5)).astype(BF16)


def _kv_rows(kv_ref, sb):
    dh = HEAD_DIM
    return jnp.concatenate([kv_ref[sb, :, r * dh:(r + 1) * dh] for r in range(KV_ROWS)], axis=0)


def _value_weights(p):
    return pltpu.roll(p, NSA_KV_HEADS, 1)


SN = 2


def _nsa_sample_a_kernel(n_pages, nb_past, pt_ref, q_ref, kvc_ref, kvw_ref, sm_ref, *rest):
    page_refs = rest[:SN * n_pages]
    win_ref, w4_ref, bc_ref, bw_ref = rest[SN * n_pages:SN * n_pages + 4]
    part_ref, idx_ref, wo_ref, cmp_ref = rest[-4:]
    dh = HEAD_DIM
    blk_rows = CMP_BLOCK * KV_ROWS
    bpp = PAGE_SIZE // CMP_BLOCK
    wl = win_ref.shape[0] // SN
    w4 = w4_ref[...]
    lanes = cmp_ref.shape[1]
    lane = lax.broadcasted_iota(I32, (NSA_KV_HEADS, lanes), 1)
    gi = lax.broadcasted_iota(I32, (NSA_KV_HEADS, lanes), 0)
    n = lane // KV_ROWS
    cur = nb_past
    cand = ((lane % KV_ROWS) == gi) & (n <= cur)
    forced = (n == cur) | (n == cur - 1) | (n == 0)
    li = lax.broadcasted_iota(I32, (NSA_KV_HEADS, N_SELECT), 1)

    for sb in range(SN):
        for p in range(n_pages):
            x = page_refs[sb * n_pages + p][...]
            sums = []
            for half in range(bpp):
                pr = x[half * blk_rows:(half + 1) * blk_rows] * w4
                s8 = jnp.sum(pr.reshape(blk_rows // 8, 8, dh), axis=0)
                sums.append(s8[0:KV_ROWS] + s8[KV_ROWS:2 * KV_ROWS])
            cmp_ref[sb, p * bpp * KV_ROWS:(p + 1) * bpp * KV_ROWS, :] = jnp.concatenate(sums, axis=0)
        r0 = nb_past * KV_ROWS
        cmp_ref[sb, r0:r0 + KV_ROWS, :] = _kv_rows(kvc_ref, sb) * w4[0:KV_ROWS]
        cmp_ref[sb, r0 + KV_ROWS:, :] = jnp.zeros((lanes - r0 - KV_ROWS, dh), F32)
        cm16 = cmp_ref[sb].astype(BF16)

        q16 = _head_rows(q_ref, sb, 0, NSA_HEADS)
        p = _masked_softmax_rows(_bdot(q16, cm16, _NT) + bc_ref[...])
        o_c = _bdot(_value_weights(p), cm16)

        imp = jnp.concatenate([jnp.sum(p[g * NSA_GROUP:(g + 1) * NSA_GROUP], axis=0, keepdims=True)
                               for g in range(NSA_KV_HEADS)], axis=0)
        score = jnp.where(cand, jnp.where(forced, SEL_FORCE, imp), -3e38)
        top = jnp.zeros((NSA_KV_HEADS, N_SELECT), I32)
        for r in range(N_SELECT):
            a = jnp.argmax(score, axis=-1, keepdims=True).astype(I32)
            top = jnp.where(li == r, a // KV_ROWS, top)
            score = jnp.where(lane == a, -3e38, score)
        idx_ref[sb] = top

        ws = slice(sb * wl, (sb + 1) * wl)
        wo_ref[ws, :] = pltpu.roll(win_ref[ws, :], wl - KV_ROWS, 0)
        wo_ref[(sb + 1) * wl - KV_ROWS:(sb + 1) * wl, :] = _kv_rows(kvw_ref, sb)
        w16 = wo_ref[ws, :].astype(BF16)
        pw = _masked_softmax_rows(_bdot(q16, w16, _NT) + bw_ref[...])
        o_w = _bdot(_value_weights(pw), w16)

        gates = _sigmoid(sm_ref[sb, :, SM_GATE:SM_GATE + 3 * NSA_HEADS])
        for h in range(NSA_HEADS):
            c0 = 3 * h
            part_ref[sb, :, h * dh:(h + 1) * dh] = (
                o_c[h:h + 1] * gates[:, c0:c0 + 1] + o_w[h:h + 1] * gates[:, c0 + 2:c0 + 3])


def _nsa_sample_a(page_table, proj_s3, pool_cmp, win_rows, w4, bias_c, bias_w, layer, n_phys, prev_win):
    nbatch, n_pages = page_table.shape
    nb_past = n_pages * PAGE_SIZE // CMP_BLOCK
    wl = bias_w.shape[1]
    page_rows = PAGE_SIZE * KV_ROWS
    nsteps = nbatch // SN
    col = lambda w, c: pl.BlockSpec((SN, 1, w), lambda i, pt: (i, 0, c // w))
    full = lambda shape: pl.BlockSpec(shape, lambda i, pt: (0,) * len(shape))
    page = lambda sb, p: pl.BlockSpec((page_rows, HEAD_DIM), lambda i, pt: (layer * n_phys + pt[SN * i + sb, p], 0))
    gs = pltpu.PrefetchScalarGridSpec(
        num_scalar_prefetch=1, grid=(nsteps,),
        in_specs=[col(NSA_WIDTH, C_NQ), col(KVW, C_KVC), col(KVW, C_KVW), col(128, C_SM)]
        + [page(sb, p) for sb in range(SN) for p in range(n_pages)]
        + [pl.BlockSpec((SN * wl, HEAD_DIM), lambda i, pt: (layer * nsteps + i, 0)),
           full(w4.shape), full(bias_c.shape), full(bias_w.shape)]
        + [pl.BlockSpec(memory_space=pl.ANY)] * len(prev_win),
        out_specs=(pl.BlockSpec((SN, 1, NSA_WIDTH), lambda i, pt: (i, 0, 0)),
                   pl.BlockSpec((SN, NSA_KV_HEADS, N_SELECT), lambda i, pt: (i, 0, 0)),
                   pl.BlockSpec((SN * wl, HEAD_DIM), lambda i, pt: (layer * nsteps + i, 0))),
        scratch_shapes=[pltpu.VMEM((SN, CMP_PAD * KV_ROWS, HEAD_DIM), F32)])
    return pl.pallas_call(
        functools.partial(_nsa_sample_a_kernel, n_pages, nb_past),
        out_shape=(jax.ShapeDtypeStruct((nbatch, 1, NSA_WIDTH), F32),
                   jax.ShapeDtypeStruct((nbatch, NSA_KV_HEADS, N_SELECT), I32),
                   jax.ShapeDtypeStruct(win_rows.shape, F32)),
        grid_spec=gs,
        input_output_aliases={1 + 4 + SN * n_pages + 4 + a: 2 for a in range(len(prev_win))},
        compiler_params=_params(("arbitrary",)),
        name="nsa_sample_a",
    )(page_table, proj_s3, proj_s3, proj_s3, proj_s3, *([pool_cmp] * (SN * n_pages)), win_rows, w4, bias_c, bias_w,
      *prev_win)


def _nsa_sample_b_kernel(nb_past, pt_ref, ix_ref, q_ref, kvs_ref, sm_ref, part_ref, *rest):
    nsel = NSA_KV_HEADS * N_SELECT
    blk_refs = rest[:SN * nsel]
    bs_ref, o_ref = rest[SN * nsel:]
    i0 = pl.program_id(0) * SN
    hg, dh = NSA_GROUP, HEAD_DIM
    blk_rows = CMP_BLOCK * KV_ROWS
    q16 = jnp.concatenate([_head_rows(q_ref, sb, 0, NSA_HEADS) for sb in range(SN)], axis=0)
    masked = jnp.full((hg, blk_rows), NEG, F32)
    keys, bias = [], []
    for sb in range(SN):
        new_blk = jnp.concatenate([_kv_rows(kvs_ref, sb), jnp.zeros((blk_rows - KV_ROWS, dh), F32)], axis=0)
        for g in range(NSA_KV_HEADS):
            owner = sb * NSA_KV_HEADS + g
            for i in range(N_SELECT):
                idx = ix_ref[i0 + sb, g * N_SELECT + i]
                past_blk = blk_refs[sb * nsel + g * N_SELECT + i][...]
                keys.append(jnp.where(idx >= nb_past, new_blk, past_blk).astype(BF16))
                bt = bs_ref[jnp.clip(idx, 0, nb_past), g * hg:(g + 1) * hg, :]
                bias.append(jnp.concatenate([bt if rg == owner else masked
                                             for rg in range(SN * NSA_KV_HEADS)], axis=0))
    k_all = jnp.concatenate(keys, axis=0)
    p = _masked_softmax_rows(_bdot(q16, k_all, _NT) + jnp.concatenate(bias, axis=1))
    o_s = _bdot(_value_weights(p), k_all)
    for sb in range(SN):
        gates = _sigmoid(sm_ref[sb, :, SM_GATE:SM_GATE + 3 * NSA_HEADS])
        for h in range(NSA_HEADS):
            cs = slice(h * dh, (h + 1) * dh)
            r = sb * NSA_HEADS + h
            o_ref[sb, :, cs] = part_ref[sb, :, cs] + o_s[r:r + 1] * gates[:, 3 * h + 1:3 * h + 2]


def _nsa_sample_b(page_table, top_idx, proj_s3, part, pool_sel, bias_s, layer, n_phys):
    nbatch, n_pages = page_table.shape
    nb_past = n_pages * PAGE_SIZE // CMP_BLOCK
    bpp = PAGE_SIZE // CMP_BLOCK
    nsel = NSA_KV_HEADS * N_SELECT
    col = lambda w, c: pl.BlockSpec((SN, 1, w), lambda i, pt, ix: (i, 0, c // w))

    def blk(sb, gi):
        def imap(i, pt, ix):
            b = SN * i + sb
            ip = jnp.clip(ix[b, gi], 0, nb_past - 1)
            return ((layer * n_phys + pt[b, ip // bpp]) * bpp + ip % bpp, 0)
        return pl.BlockSpec((CMP_BLOCK * KV_ROWS, HEAD_DIM), imap)

    gs = pltpu.PrefetchScalarGridSpec(
        num_scalar_prefetch=2, grid=(nbatch // SN,),
        in_specs=[col(NSA_WIDTH, C_NQ), col(KVW, C_KVS), col(128, C_SM),
                  pl.BlockSpec((SN, 1, NSA_WIDTH), lambda i, pt, ix: (i, 0, 0))]
        + [blk(sb, gi) for sb in range(SN) for gi in range(nsel)]
        + [pl.BlockSpec(bias_s.shape, lambda i, pt, ix: (0, 0, 0))],
        out_specs=pl.BlockSpec((SN, 1, NSA_WIDTH), lambda i, pt, ix: (i, 0, 0)))
    return pl.pallas_call(
        functools.partial(_nsa_sample_b_kernel, nb_past),
        out_shape=jax.ShapeDtypeStruct((nbatch, 1, NSA_WIDTH), F32),
        grid_spec=gs,
        compiler_params=_params(("arbitrary",)),
        name="nsa_sample_b",
    )(page_table, top_idx, proj_s3, proj_s3, proj_s3, part, *([pool_sel] * (SN * nsel)), bias_s)


SB = 8


def _rec_sample_kernel(gq_ref, gk_ref, gv_ref, gg_ref, x_ref, dz_ref, sm_ref, sg_ref, sd_ref, cb_ref,
                       w2_ref, b2_ref, gnw_ref, cw_ref, al_ref, dt_ref, dnw_ref, *rest):
    o_ref, sgo_ref, sdo_ref = rest[-3:]
    sm = sm_ref[...]
    ri = lax.broadcasted_iota(I32, (SB, 128), 0)
    log_a = _log_sigmoid(_bdot(sm[:, SM_GLR:SM_GLR + GLA_GATE_RANK], w2_ref[...]) + b2_ref[...]) / GLA_GATE_NORM
    ea_t = jnp.exp(log_a).T
    k_t = gk_ref[...].T
    q_t = (gq_ref[...] * (GLA_DK ** -0.5)).T
    gv = gv_ref[...]
    for h in range(GLA_HEADS):
        hs = slice(h * GLA_DK, (h + 1) * GLA_DK)
        vs = slice(h * GLA_DV, (h + 1) * GLA_DV)
        o_h = jnp.zeros((SB, GLA_DV), F32)
        for i in range(SB):
            s = ea_t[hs, i:i + 1] * sg_ref[i, h] + k_t[hs, i:i + 1] * gv[i:i + 1, vs]
            sgo_ref[i, h] = s
            o = jnp.sum(q_t[hs, i:i + 1] * s, axis=0, keepdims=True)
            o_h = jnp.where(ri == i, o, o_h)
        o_ref[:, vs] = _rms(o_h, gnw_ref[...]) * _silu(gg_ref[:, vs])
    y = x_ref[...] * cw_ref[CONV_W - 1:CONV_W, :]
    for jw in range(CONV_W - 1):
        y = y + cb_ref[jw] * cw_ref[jw:jw + 1, :]
    y = _silu(y)
    beta = _sigmoid(sm[:, SM_DB:SM_DB + GDN_HEADS])
    eg = jnp.exp(-jnp.exp(al_ref[...]) * _softplus(sm[:, SM_DA:SM_DA + GDN_HEADS] + dt_ref[...]))
    dk, dv = GDN_DK, GDN_DV
    for h in range(GDN_HEADS):
        cq = y[:, h * dk:(h + 1) * dk]
        ck = y[:, GDN_WIDTH + h * dk:GDN_WIDTH + (h + 1) * dk]
        v = y[:, 2 * GDN_WIDTH + h * dv:2 * GDN_WIDTH + (h + 1) * dv]
        q_t = (cq * lax.rsqrt(jnp.sum(cq * cq, axis=-1, keepdims=True) + 1e-6) * (dk ** -0.5)).T
        k_t = (ck * lax.rsqrt(jnp.sum(ck * ck, axis=-1, keepdims=True) + 1e-6)).T
        o_h = jnp.zeros((SB, dv), F32)
        for i in range(SB):
            s = eg[i:i + 1, h:h + 1] * sd_ref[i, h]
            kc = k_t[:, i:i + 1]
            delta = (v[i:i + 1] - jnp.sum(kc * s, axis=0, keepdims=True)) * beta[i:i + 1, h:h + 1]
            s = s + kc * delta
            sdo_ref[i, h] = s
            o = jnp.sum(q_t[:, i:i + 1] * s, axis=0, keepdims=True)
            o_h = jnp.where(ri == i, o, o_h)
        vs = slice(GLA_WIDTH + h * dv, GLA_WIDTH + (h + 1) * dv)
        o_ref[:, vs] = _rms(o_h, dnw_ref[...]) * _silu(dz_ref[:, h * dv:(h + 1) * dv])


def _rec_sample(proj_s, state_gla, state_gdn, conv_t, w2, b2, gnw, cw, a_log, dt_bias, dnw, layer, prev_states):
    nbatch = proj_s.shape[0]
    nblk = nbatch // SB
    depth = state_gla.shape[0] // nbatch
    alias_specs = [pl.BlockSpec(memory_space=pl.ANY)] * len(prev_states)
    n_in = 17
    row = lambda w, col: pl.BlockSpec((SB, w), lambda i: (i, col // w))
    full = lambda shape: pl.BlockSpec(shape, lambda i: (0,) * len(shape))
    return pl.pallas_call(
        _rec_sample_kernel,
        out_shape=(jax.ShapeDtypeStruct((nbatch, GLA_WIDTH + GDN_WIDTH), F32),
                   jax.ShapeDtypeStruct(state_gla.shape, F32),
                   jax.ShapeDtypeStruct(state_gdn.shape, F32)),
        grid=(nblk,),
        in_specs=[row(GLA_HEADS * GLA_DK, C_GQ), row(GLA_HEADS * GLA_DK, C_GK), row(GLA_WIDTH, C_GV),
                  row(GLA_WIDTH, C_GG), row(3 * GDN_WIDTH, C_DQKV), row(GDN_WIDTH, C_DZ), row(128, C_SM),
                  pl.BlockSpec((SB, GLA_HEADS, GLA_DK, GLA_DV), lambda i: (layer * nblk + i, 0, 0, 0)),
                  pl.BlockSpec((SB, GDN_HEADS, GDN_DK, GDN_DV), lambda i: (layer * nblk + i, 0, 0, 0)),
                  pl.BlockSpec((CONV_W - 1, SB, 3 * GDN_WIDTH), lambda i: (0, i, 0)),
                  full(w2.shape), full(b2.shape), full(gnw.shape), full(cw.shape), full(a_log.shape),
                  full(dt_bias.shape), full(dnw.shape)] + alias_specs,
        out_specs=(pl.BlockSpec((SB, GLA_WIDTH + GDN_WIDTH), lambda i: (i, 0)),
                   pl.BlockSpec((SB, GLA_HEADS, GLA_DK, GLA_DV), lambda i: (layer * nblk + i, 0, 0, 0)),
                   pl.BlockSpec((SB, GDN_HEADS, GDN_DK, GDN_DV), lambda i: (layer * nblk + i, 0, 0, 0))),
        input_output_aliases={n_in + a: 1 + a for a in range(len(prev_states))},
        compiler_params=_params(("parallel",)),
        name="rec_sample",
    )(proj_s, proj_s, proj_s, proj_s, proj_s, proj_s, proj_s, state_gla, state_gdn, conv_t,
      w2, b2, gnw, cw, a_log, dt_bias, dnw, *prev_states)


def _reorder_w_in(w):
    d = w.shape[0]
    return jnp.concatenate([
        w[:, 0:2560],
        w[:, 2584:3096],
        w[:, 4136:5672],
        w[:, 3096:3608],
        w[:, 3624:4136],
        w[:, 5672:6184],
        w[:, 2560:2584],
        w[:, 3608:3624],
        w[:, 6184:6192],
        jnp.zeros((d, PROJ_N - 6192), w.dtype)], axis=1)


def _cmp_weight_tile(w_cmp):
    half = NSA_KV_HEADS * HEAD_DIM
    return jnp.concatenate([jnp.broadcast_to(w_cmp[:, 0:1], (CMP_BLOCK, half)),
                            jnp.broadcast_to(w_cmp[:, 1:2], (CMP_BLOCK, half))], axis=1).astype(F32)


def _cmp_weight_rows(w_cmp):
    w = jnp.repeat(w_cmp, NSA_KV_HEADS, axis=1).reshape(CMP_BLOCK * KV_ROWS, 1)
    return jnp.broadcast_to(w, (CMP_BLOCK * KV_ROWS, HEAD_DIM)).astype(F32)


def kernel(x_prompt, x_sample, cache_cmp, cache_sel, cache_win, state_gla, state_gdn, state_conv, page_table,
           p_prompt, p_sample, norm_w, ffn_w1, ffn_w3, ffn_w2, w_in, w_out, nsa_w_cmp, t5_bias,
           gla_w_gk2, gla_b_gk, gla_norm_w, gdn_conv_w, gdn_a_log, gdn_dt_bias, gdn_norm_w,
           ple_w_proj, ple_w_gate):
    depth = w_in.shape[0]
    batch, seq, d = x_prompt.shape
    nbatch = x_sample.shape[0]
    n_phys = cache_cmp.shape[1]
    n_pages = page_table.shape[1]
    past = n_pages * PAGE_SIZE
    win_len = cache_win.shape[2]
    rows_p = batch * seq
    kv_shape = (2, NSA_KV_HEADS, HEAD_DIM)

    x = jnp.concatenate([x_prompt.reshape(rows_p, d), x_sample.reshape(nbatch, d)], axis=0)
    assert win_len == WINDOW and past // CMP_BLOCK + 1 >= N_SELECT and past // CMP_BLOCK < CMP_PAD
    pool_cmp = cache_cmp.reshape(-1, HEAD_DIM)
    pool_sel = cache_sel.reshape(-1, HEAD_DIM)
    win_all = cache_win.reshape(-1, HEAD_DIM)
    sgla_all = state_gla.reshape((depth * nbatch,) + state_gla.shape[2:])
    sgdn_all = state_gdn.reshape((depth * nbatch,) + state_gdn.shape[2:])

    pb_c, pb_s, pb_w = _prompt_bias_tables(t5_bias, seq)
    sb_c, sb_s, sb_w = _sample_bias_tables(t5_bias, past, win_len)
    nb = seq // CMP_BLOCK
    expand = jnp.asarray(-MASK_BIG * np.repeat(np.eye(nb, dtype=np.float32), CMP_BLOCK, axis=1)
                         .reshape(nb, seq // KEY_CHUNK, KEY_CHUNK).transpose(1, 0, 2), BF16)

    outs = {k: [] for k in ("cmp_p", "cmp_s", "sel_p", "sel_s", "win_p", "gla_p", "gdn_p", "conv_p", "conv_s")}
    prev_win, prev_states = (), ()
    for l in range(depth):
        nw = norm_w[l].reshape(-1, 1, d)
        w1 = ffn_w1[l].astype(BF16)
        w3 = ffn_w3[l].astype(BF16)
        w2 = ffn_w2[l].astype(BF16)
        x = _ffn(x, nw[0], w1[0], w3[0], w2[0], nw[1])

        proj = _proj(x, nw[2], _reorder_w_in(w_in[l].astype(BF16)))
        proj_s = proj[rows_p:]
        proj_s3 = proj_s.reshape(nbatch, 1, PROJ_N)
        wtile = _cmp_weight_tile(nsa_w_cmp[l])
        w_gk2 = gla_w_gk2[l]
        b_gk = gla_b_gk[l].reshape(1, -1)
        gla_nw = gla_norm_w[l].reshape(1, -1)
        gdn_nw = gdn_norm_w[l].reshape(1, -1)
        conv_w = gdn_conv_w[l]
        a_log = gdn_a_log[l].reshape(1, -1)
        dt_bias = gdn_dt_bias[l].reshape(1, -1)

        cmpkv = _compress_prompt(proj, wtile, rows_p)
        kvs16 = proj[:rows_p, C_KVS:C_KVS + KVW].astype(BF16)
        kvw16 = jnp.pad(proj[:rows_p, C_KVW:C_KVW + KVW].astype(BF16).reshape(batch, seq, KVW),
                        ((0, 0), (WINDOW, 0), (0, 0))).reshape(batch * (WINDOW + seq), KVW)
        o_nsa = _nsa_prompt(proj, cmpkv, kvs16, kvw16, pb_c, pb_s, pb_w, expand, batch, seq)
        o_gla, s_gla_p = _gla_prompt(proj, w_gk2, b_gk, gla_nw, batch, seq)
        o_gdn, s_gdn_p = _gdn_prompt(proj, conv_w, a_log, dt_bias, gdn_nw, batch, seq)

        part, top_idx, win_new = _nsa_sample_a(page_table, proj_s3, pool_cmp, win_all, _cmp_weight_rows(nsa_w_cmp[l]),
                                               sb_c, sb_w, l, n_phys, prev_win)
        prev_win = (win_new,)
        o_nsa_s = _nsa_sample_b(page_table, top_idx.reshape(nbatch, NSA_KV_HEADS * N_SELECT), proj_s3, part,
                                pool_sel, sb_s, l, n_phys)
        conv_t = jnp.swapaxes(state_conv[l], 0, 1)
        o_rec_s, s_gla_s, s_gdn_s = _rec_sample(proj_s, sgla_all, sgdn_all, conv_t, w_gk2, b_gk, gla_nw,
                                                conv_w, a_log, dt_bias, gdn_nw, l, prev_states)
        prev_states = (s_gla_s, s_gdn_s)

        y = jnp.concatenate([jnp.concatenate([o_nsa, o_gla, o_gdn], axis=1),
                             jnp.concatenate([o_nsa_s.reshape(nbatch, NSA_WIDTH), o_rec_s], axis=1)], axis=0)
        x = _outproj(y, x, w_out[l].astype(BF16), nw[3])
        x = _ffn(x, nw[4], w1[1], w3[1], w2[1], nw[5])
        p = jnp.concatenate([p_prompt[l].reshape(rows_p, -1), p_sample[l].reshape(nbatch, -1)], axis=0)
        x = _ple(x, p, nw[6], ple_w_gate[l].astype(BF16), ple_w_proj[l].astype(BF16), nw[7],
                 split_rows=rows_p if l == depth - 1 else None)

        pp = proj[:rows_p]
        outs["cmp_p"].append(pp[:, C_KVC:C_KVC + KVW].reshape((batch, seq) + kv_shape))
        outs["sel_p"].append(pp[:, C_KVS:C_KVS + KVW].reshape((batch, seq) + kv_shape))
        wp = min(WINDOW, seq)
        outs["win_p"].append(pp[:, C_KVW:C_KVW + KVW].reshape((batch, seq) + kv_shape)[:, seq - wp:])
        outs["cmp_s"].append(proj_s[:, C_KVC:C_KVC + KVW].reshape((nbatch, 1) + kv_shape))
        outs["sel_s"].append(proj_s[:, C_KVS:C_KVS + KVW].reshape((nbatch, 1) + kv_shape))
        outs["gla_p"].append(s_gla_p)
        outs["gdn_p"].append(s_gdn_p)
        dqkv_p = pp[:, C_DQKV:C_DQKV + 3 * GDN_WIDTH].reshape(batch, seq, 3 * GDN_WIDTH)
        outs["conv_p"].append(dqkv_p[:, seq - (CONV_W - 1):])
        outs["conv_s"].append(jnp.concatenate(
            [state_conv[l][:, 1:], proj_s[:, None, C_DQKV:C_DQKV + 3 * GDN_WIDTH]], axis=1))

    st = lambda k: jnp.stack(outs[k])
    return (x[0].reshape(batch, seq, d), x[1].reshape(nbatch, 1, d),
            st("cmp_p"), st("cmp_s"), st("sel_p"), st("sel_s"), st("win_p"), win_new.reshape(cache_win.shape),
            st("gla_p"), s_gla_s.reshape(state_gla.shape), st("gdn_p"), s_gdn_s.reshape(state_gdn.shape),
            st("conv_p"), st("conv_s"))
```

```python
import functools
import math

import numpy as np
import jax
import jax.numpy as jnp
from jax import lax
from jax.experimental import pallas as pl
from jax.experimental.pallas import tpu as pltpu

F32 = jnp.float32
BF16 = jnp.bfloat16
I32 = jnp.int32

D_MODEL = 2048
HEAD_DIM = 128
NSA_HEADS = 8
NSA_KV_HEADS = 2
NSA_GROUP = NSA_HEADS // NSA_KV_HEADS
CMP_BLOCK = 64
N_SELECT = 8
WINDOW = 512
Q_BLOCK = 128
SEL_FORCE = 1.0e4
GLA_HEADS = 4
GLA_DK = 64
GLA_DV = 128
GLA_GATE_RANK = 16
GLA_GATE_NORM = 16.0
GDN_HEADS = 4
GDN_DK = 128
GDN_DV = 128
CHUNK = 64
CONV_W = 4
N_BUCKETS = 32
T5_MAX_DIST = 128
D_FF = 5632
PLE_DIM = 256
RMS_EPS = 1e-6
NEG = -1e30
MASK_BIG = 2.0 ** 100
PAGE_SIZE = 128

NSA_WIDTH = NSA_HEADS * HEAD_DIM
KVW = 2 * NSA_KV_HEADS * HEAD_DIM
KV_ROWS = 2 * NSA_KV_HEADS
CMP_PAD = 64
GDN_WIDTH = GDN_HEADS * GDN_DV
GLA_WIDTH = GLA_HEADS * GLA_DV

C_NQ = 0
C_KVC = 1024
C_KVS = 1536
C_KVW = 2048
C_GQ = 2560
C_GK = 2816
C_DQKV = 3072
C_GV = 4608
C_GG = 5120
C_DZ = 5632
C_SM = 6144
PROJ_N = 6272
SM_GATE, SM_GLR, SM_DB, SM_DA = 0, 24, 40, 44

VMEM_LIMIT = 56 * 1024 * 1024
TM = 640
TF = 512
TN_PROJ = 896
TB = 256


def _sigmoid(x):
    return 1.0 / (1.0 + jnp.exp(-x))


def _silu(x):
    return x * _sigmoid(x)


def _softplus(x):
    return jnp.maximum(x, 0.0) + jnp.log(1.0 + jnp.exp(-jnp.abs(x)))


def _log_sigmoid(x):
    return jnp.minimum(x, 0.0) - jnp.log(1.0 + jnp.exp(-jnp.abs(x)))


def _rms(x, w):
    return x * lax.rsqrt(jnp.mean(x * x, axis=-1, keepdims=True) + RMS_EPS) * w


_NN = (((1,), (0,)), ((), ()))
_NT = (((1,), (1,)), ((), ()))
_TN = (((0,), (0,)), ((), ()))


def _bdot(a, b, dims=_NN):
    return lax.dot_general(a.astype(BF16), b.astype(BF16), dims, preferred_element_type=F32)


def _hdot(a, b, dims=_NN):
    return lax.dot_general(a, b, dims, preferred_element_type=F32, precision=lax.Precision.HIGHEST)


def _params(sem):
    return pltpu.CompilerParams(dimension_semantics=sem, vmem_limit_bytes=VMEM_LIMIT)


def _ffn_kernel(x_ref, nwa_ref, w1_ref, w3_ref, w2_ref, nwb_ref, o_ref, h_ref, acc_ref):
    f = pl.program_id(1)

    @pl.when(f == 0)
    def _():
        h_ref[...] = _rms(x_ref[...], nwa_ref[...]).astype(BF16)
        acc_ref[...] = jnp.zeros_like(acc_ref)

    h = h_ref[...]
    a = jnp.dot(h, w1_ref[...], preferred_element_type=F32)
    b = jnp.dot(h, w3_ref[...], preferred_element_type=F32)
    g = (_silu(a) * b).astype(BF16)
    acc_ref[...] += jnp.dot(g, w2_ref[...], preferred_element_type=F32)

    @pl.when(f == pl.num_programs(1) - 1)
    def _():
        o_ref[...] = x_ref[...] + 0.5 * _rms(acc_ref[...], nwb_ref[...])


def _ffn(x, nwa, w1, w3, w2, nwb):
    m, d = x.shape
    dff = w1.shape[1]
    row = lambda i, f: (i, 0)
    return pl.pallas_call(
        _ffn_kernel,
        out_shape=jax.ShapeDtypeStruct((m, d), F32),
        grid=(m // TM, dff // TF),
        in_specs=[pl.BlockSpec((TM, d), row),
                  pl.BlockSpec((1, d), lambda i, f: (0, 0)),
                  pl.BlockSpec((d, TF), lambda i, f: (0, f)),
                  pl.BlockSpec((d, TF), lambda i, f: (0, f)),
                  pl.BlockSpec((TF, d), lambda i, f: (f, 0)),
                  pl.BlockSpec((1, d), lambda i, f: (0, 0))],
        out_specs=pl.BlockSpec((TM, d), row),
        scratch_shapes=[pltpu.VMEM((TM, d), BF16), pltpu.VMEM((TM, d), F32)],
        compiler_params=_params(("parallel", "arbitrary")),
        name="ffn",
    )(x, nwa, w1, w3, w2, nwb)


def _proj_kernel(x_ref, nw_ref, w_ref, o_ref, h_ref):
    @pl.when(pl.program_id(1) == 0)
    def _():
        h_ref[...] = _rms(x_ref[...], nw_ref[...]).astype(BF16)

    o_ref[...] = jnp.dot(h_ref[...], w_ref[...], preferred_element_type=F32)


def _proj(x, nw, w):
    m, d = x.shape
    n = w.shape[1]
    return pl.pallas_call(
        _proj_kernel,
        out_shape=jax.ShapeDtypeStruct((m, n), F32),
        grid=(m // TM, n // TN_PROJ),
        in_specs=[pl.BlockSpec((TM, d), lambda i, j: (i, 0)),
                  pl.BlockSpec((1, d), lambda i, j: (0, 0)),
                  pl.BlockSpec((d, TN_PROJ), lambda i, j: (0, j))],
        out_specs=pl.BlockSpec((TM, TN_PROJ), lambda i, j: (i, j)),
        scratch_shapes=[pltpu.VMEM((TM, d), BF16)],
        compiler_params=_params(("parallel", "arbitrary")),
        name="proj",
    )(x, nw, w)


def _outproj_kernel(y_ref, x_ref, w_ref, nw_ref, o_ref):
    z = jnp.dot(y_ref[...].astype(BF16), w_ref[...], preferred_element_type=F32)
    o_ref[...] = x_ref[...] + _rms(z, nw_ref[...])


def _outproj(y, x, w, nw):
    m, d = x.shape
    k = y.shape[1]
    return pl.pallas_call(
        _outproj_kernel,
        out_shape=jax.ShapeDtypeStruct((m, d), F32),
        grid=(m // TM,),
        in_specs=[pl.BlockSpec((TM, k), lambda i: (i, 0)),
                  pl.BlockSpec((TM, d), lambda i: (i, 0)),
                  pl.BlockSpec((k, d), lambda i: (0, 0)),
                  pl.BlockSpec((1, d), lambda i: (0, 0))],
        out_specs=pl.BlockSpec((TM, d), lambda i: (i, 0)),
        compiler_params=_params(("parallel",)),
        name="outproj",
    )(y, x, w, nw)


def _ple_kernel(split, x_ref, p_ref, nwa_ref, wg_ref, wp_ref, nwb_ref, *o_refs):
    x = x_ref[...]
    gate = _sigmoid(jnp.dot(_rms(x, nwa_ref[...]).astype(BF16), wg_ref[...], preferred_element_type=F32))
    pp = jnp.dot(p_ref[...].astype(BF16), wp_ref[...], preferred_element_type=F32)
    y = x + _rms(gate * pp, nwb_ref[...])
    o_refs[0][...] = y
    if split is not None:
        @pl.when(pl.program_id(0) == pl.num_programs(0) - 1)
        def _():
            o_refs[1][...] = y[split:]


def _ple(x, p, nwa, wg, wp, nwb, split_rows=None):
    m, d = x.shape
    pd = p.shape[1]
    nblk = m // TM
    out_shape = jax.ShapeDtypeStruct((m, d), F32)
    out_specs = pl.BlockSpec((TM, d), lambda i: (i, 0))
    split = None
    if split_rows is not None:
        split = split_rows - (nblk - 1) * TM
        assert 0 < split and m - split_rows == TM - split
        out_shape = (jax.ShapeDtypeStruct((split_rows, d), F32), jax.ShapeDtypeStruct((m - split_rows, d), F32))
        out_specs = (out_specs, pl.BlockSpec((m - split_rows, d), lambda i: (0, 0)))
    return pl.pallas_call(
        functools.partial(_ple_kernel, split),
        out_shape=out_shape,
        grid=(nblk,),
        in_specs=[pl.BlockSpec((TM, d), lambda i: (i, 0)),
                  pl.BlockSpec((TM, pd), lambda i: (i, 0)),
                  pl.BlockSpec((1, d), lambda i: (0, 0)),
                  pl.BlockSpec((d, d), lambda i: (0, 0)),
                  pl.BlockSpec((pd, d), lambda i: (0, 0)),
                  pl.BlockSpec((1, d), lambda i: (0, 0))],
        out_specs=out_specs,
        compiler_params=_params(("arbitrary",)),
        name="ple",
    )(x, p, nwa, wg, wp, nwb)


def _t5_bucket_np(dist):
    n = np.maximum(dist, 0)
    exact = N_BUCKETS // 2
    val = (np.log(np.maximum(n, 1).astype(np.float32) / np.float32(exact))
           / np.float32(math.log(T5_MAX_DIST / exact)) * np.float32(N_BUCKETS - exact))
    large = exact + val.astype(np.int32)
    return np.where(n < exact, n, np.minimum(large, N_BUCKETS - 1)).astype(np.int32)


def _bias_table(t5_bias, dist, valid):
    bucket = jnp.asarray(np.where(valid, _t5_bucket_np(dist), -1).astype(np.int32))[None]
    hshape = (t5_bias.shape[1],) + (1,) * dist.ndim
    out = jnp.full((t5_bias.shape[1],) + dist.shape, NEG, F32)
    for k in range(N_BUCKETS):
        out = jnp.where(bucket == k, t5_bias[k].reshape(hshape), out)
    return out


def _prompt_bias_tables(t5_bias, seq):
    nb = seq // CMP_BLOCK
    qpos = np.arange(seq)[:, None]
    d_c = qpos - (np.arange(nb) * CMP_BLOCK + CMP_BLOCK - 1)[None, :]
    bias_c = _bias_table(t5_bias, d_c, d_c >= 0)
    i = np.arange(Q_BLOCK)[:, None]
    jj = np.arange(Q_BLOCK)[None, :]
    d_s = np.stack([Q_BLOCK * dl + i - jj for dl in range(3)] + [i - jj - Q_BLOCK])
    bias_s = jnp.transpose(_bias_table(t5_bias, d_s, d_s >= 0), (1, 0, 2, 3))
    nwc = WINDOW // Q_BLOCK + 1
    d_w = np.stack([i + WINDOW - Q_BLOCK * cw - jj for cw in range(nwc)] + [i - jj - Q_BLOCK])
    bias_w = jnp.transpose(_bias_table(t5_bias, d_w, (d_w >= 0) & (d_w < WINDOW)), (1, 0, 2, 3))
    return bias_c, bias_s, bias_w


def _sample_bias_tables(t5_bias, past, win_len):
    nb_past = past // CMP_BLOCK
    head_g = (np.arange(NSA_HEADS) // NSA_GROUP)[:, None]

    def per_head(dist, valid):
        lane_cg = np.arange(dist.shape[-1]) % KV_ROWS
        tbl = _bias_table(t5_bias, dist, valid)
        own = jnp.asarray(lane_cg[None] == head_g)
        return tbl, own

    lane = np.arange(CMP_PAD * KV_ROWS)
    n = lane // KV_ROWS
    d_c = past - (n * CMP_BLOCK + CMP_BLOCK - 1)
    tbl, own = per_head(d_c, (d_c >= 0) & (n <= nb_past))
    bias_c = jnp.where(own, tbl, NEG)
    lane = np.arange(CMP_BLOCK * KV_ROWS)
    blk = np.arange(nb_past + 1)[:, None]
    d_s = past - (blk * CMP_BLOCK + (lane // KV_ROWS)[None, :])
    tbl, own = per_head(d_s, d_s >= 0)
    bias_s = jnp.transpose(jnp.where(own[:, None, :], tbl, NEG), (1, 0, 2))
    lane = np.arange(win_len * KV_ROWS)
    wpos = past - win_len + 1 + lane // KV_ROWS
    d_w = past - wpos
    tbl, own = per_head(d_w, (d_w >= 0) & (d_w < WINDOW) & (wpos >= 0))
    bias_w = jnp.where(own, tbl, NEG)
    return bias_c, bias_s, bias_w


def _compress_kernel(x_ref, w_ref, o_ref):
    r = x_ref.shape[0] // CMP_BLOCK
    x = x_ref[...].reshape(r, CMP_BLOCK, KVW)
    o_ref[...] = jnp.sum(x * w_ref[...][None], axis=1)


def _compress_prompt(proj, wtile, rows):
    rb = 512
    return pl.pallas_call(
        _compress_kernel,
        out_shape=jax.ShapeDtypeStruct((rows // CMP_BLOCK, KVW), F32),
        grid=(rows // rb,),
        in_specs=[pl.BlockSpec((rb, KVW), lambda i: (i, C_KVC // KVW)),
                  pl.BlockSpec((CMP_BLOCK, KVW), lambda i: (0, 0))],
        out_specs=pl.BlockSpec((rb // CMP_BLOCK, KVW), lambda i: (i, 0)),
        compiler_params=_params(("parallel",)),
        name="compress_prompt",
    )(proj, wtile)


KEY_CHUNK = 2 * Q_BLOCK
UNROLL = 2


def _nsa_prompt_kernel(q_ref, sm_ref, cmp_ref, kvs_ref, kvw_ref, bc_ref, bs_ref, bw_ref, e_ref, o_ref,
                       s_ref, mx_ref, l_ref, acc_ref):
    j = pl.program_id(1)
    nb = cmp_ref.shape[0]
    hg, qb, dh, kc = NSA_GROUP, Q_BLOCK, HEAD_DIM, KEY_CHUNK
    rows = hg * qb
    nwt = bw_ref.shape[0] - 1
    gates = _sigmoid(sm_ref[:, SM_GATE:SM_GATE + 3 * NSA_HEADS])
    qi = lax.broadcasted_iota(I32, (qb, nb), 0)
    ni = lax.broadcasted_iota(I32, (qb, nb), 1)
    cur = (qb // CMP_BLOCK) * j + qi // CMP_BLOCK
    forced = (ni == cur) | (ni == cur - 1) | (ni == 0)
    started = ni <= cur
    groups = range(NSA_KV_HEADS)
    kcols = [slice(g * dh, (g + 1) * dh) for g in groups]
    vcols = [slice((NSA_KV_HEADS + g) * dh, (NSA_KV_HEADS + g + 1) * dh) for g in groups]
    head_rows = [slice(g * hg, (g + 1) * hg) for g in groups]

    def fold(x, op):
        out = x[:, 0:qb]
        for t in range(1, x.shape[1] // qb):
            out = op(out, x[:, t * qb:(t + 1) * qb])
        return out

    q16, o_c, score = [], [], []
    for g in groups:
        q = jnp.concatenate([q_ref[:, (g * hg + h) * dh:(g * hg + h + 1) * dh] for h in range(hg)], axis=0)
        q16.append((q * (dh ** -0.5)).astype(BF16))
        bias = bc_ref[head_rows[g]].reshape(rows, nb)
        valid = bias > -1e29
        s = jnp.where(valid, _bdot(q16[g], cmp_ref[:, kcols[g]], _NT) + bias, NEG)
        p = jnp.where(valid, jnp.exp(s - jnp.max(s, axis=-1, keepdims=True)), 0.0)
        p = p / jnp.maximum(jnp.sum(p, axis=-1, keepdims=True), 1e-30)
        o_c.append(_bdot(p, cmp_ref[:, vcols[g]]))
        imp = p[0:qb]
        for h in range(1, hg):
            imp = imp + p[h * qb:(h + 1) * qb]
        score.append(jnp.where(started, jnp.where(forced, SEL_FORCE, imp), -1.0))

    unsel = [jnp.ones((qb, nb), F32) for _ in groups]
    for _ in range(min(N_SELECT, nb)):
        for g in groups:
            hit = ni == jnp.argmax(score[g], axis=-1, keepdims=True).astype(I32)
            unsel[g] = jnp.where(hit & started, 0.0, unsel[g])
            score[g] = jnp.where(hit, -3e38, score[g])
    unsel = [u.astype(BF16) for u in unsel]

    mx_ref[...] = jnp.full(mx_ref.shape, NEG, F32)
    l_ref[...] = jnp.zeros(l_ref.shape, F32)
    acc_ref[...] = jnp.zeros(acc_ref.shape, F32)
    n_chunks = j // (kc // qb) + 1

    def key_rows(c):
        return pl.ds(pl.multiple_of(c * kc, kc), kc)

    def sel_bias(g, c):
        tiles = []
        for t in range(kc // qb):
            back = j - (c * (kc // qb) + t)
            tiles.append(bs_ref[jnp.where(back < 0, 3, jnp.minimum(back, 2)), head_rows[g]].reshape(rows, qb))
        return jnp.concatenate(tiles, axis=1)

    def pass1_chunk(c):
        for g in groups:
            key_mask = jnp.dot(unsel[g], e_ref[c], preferred_element_type=F32)
            s_ = (_bdot(q16[g], kvs_ref[key_rows(c), kcols[g]], _NT) + sel_bias(g, c)
                  + jnp.concatenate([key_mask] * hg, axis=0))
            s_ref[g, c] = s_
            mx_ref[g] = jnp.maximum(mx_ref[g], fold(s_, jnp.maximum))

    def unrolled(chunk_fn):
        def body(t, carry):
            for u in range(UNROLL):
                chunk_fn(t * UNROLL + u)
            return carry
        lax.fori_loop(0, (n_chunks + UNROLL - 1) // UNROLL, body, 0)

    unrolled(pass1_chunk)
    for g in groups:
        mx_ref[g] = jnp.broadcast_to(jnp.max(mx_ref[g], axis=-1, keepdims=True), mx_ref.shape[1:])

    def pass2_chunk(c):
        for g in groups:
            p_ = jnp.exp(s_ref[g, c] - jnp.concatenate([mx_ref[g]] * (kc // qb), axis=1))
            l_ref[g] += fold(p_, jnp.add)
            acc_ref[g] += _bdot(p_, kvs_ref[key_rows(c), vcols[g]])

    unrolled(pass2_chunk)
    o_s = [acc_ref[g] / jnp.maximum(jnp.sum(l_ref[g], axis=-1, keepdims=True), 1e-30) for g in groups]

    w_rows = pl.ds(pl.multiple_of(j * qb, qb), nwt * qb)
    o_w = []
    for g in groups:
        bias = jnp.concatenate(
            [bw_ref[jnp.where(j - (nwt - 1) + t >= 0, t, nwt), head_rows[g]].reshape(rows, qb) for t in range(nwt)],
            axis=1)
        s = _bdot(q16[g], kvw_ref[w_rows, kcols[g]], _NT) + bias
        p = jnp.exp(s - jnp.max(fold(s, jnp.maximum), axis=-1, keepdims=True))
        den = jnp.sum(fold(p, jnp.add), axis=-1, keepdims=True)
        o_w.append(_bdot(p, kvw_ref[w_rows, vcols[g]]) / jnp.maximum(den, 1e-30))

    for g in groups:
        for h in range(hg):
            c0 = SM_GATE + (g * hg + h) * 3
            rs = slice(h * qb, (h + 1) * qb)
            o_ref[:, (g * hg + h) * dh:(g * hg + h + 1) * dh] = (
                o_c[g][rs] * gates[:, c0:c0 + 1] + o_s[g][rs] * gates[:, c0 + 1:c0 + 2]
                + o_w[g][rs] * gates[:, c0 + 2:c0 + 3])


def _nsa_prompt(proj, cmpkv, kvs16, kvw16, bias_c, bias_s, bias_w, expand, batch, seq):
    nqb = seq // Q_BLOCK
    nb = seq // CMP_BLOCK
    rows = NSA_GROUP * Q_BLOCK
    full = lambda shape: pl.BlockSpec(shape, lambda b, j: (0,) * len(shape))
    return pl.pallas_call(
        _nsa_prompt_kernel,
        out_shape=jax.ShapeDtypeStruct((batch * seq, NSA_WIDTH), F32),
        grid=(batch, nqb),
        in_specs=[pl.BlockSpec((Q_BLOCK, NSA_WIDTH), lambda b, j: (b * nqb + j, 0)),
                  pl.BlockSpec((Q_BLOCK, 128), lambda b, j: (b * nqb + j, C_SM // 128)),
                  pl.BlockSpec((nb, KVW), lambda b, j: (b, 0)),
                  pl.BlockSpec((seq, KVW), lambda b, j: (b, 0)),
                  pl.BlockSpec((WINDOW + seq, KVW), lambda b, j: (b, 0)),
                  pl.BlockSpec((NSA_HEADS, Q_BLOCK, nb), lambda b, j: (0, j, 0)),
                  full(bias_s.shape), full(bias_w.shape), full(expand.shape)],
        out_specs=pl.BlockSpec((Q_BLOCK, NSA_WIDTH), lambda b, j: (b * nqb + j, 0)),
        scratch_shapes=[pltpu.VMEM((NSA_KV_HEADS, seq // KEY_CHUNK, rows, KEY_CHUNK), F32),
                        pltpu.VMEM((NSA_KV_HEADS, rows, Q_BLOCK), F32), pltpu.VMEM((NSA_KV_HEADS, rows, Q_BLOCK), F32),
                        pltpu.VMEM((NSA_KV_HEADS, rows, HEAD_DIM), F32)],
        compiler_params=_params(("parallel", "arbitrary")),
        name="nsa_prompt",
    )(proj, proj, cmpkv, kvs16, kvw16, bias_c, bias_s, bias_w, expand)


ROWS_BLK = 512


def _cache_rows_kernel(per_b, kvc_ref, kvs_ref, kvw_ref, *rest):
    oc_ref, os_ref, ow_ref = rest[-3:]
    n = kvc_ref.shape[0]
    dh = HEAD_DIM

    def put(o_ref, x_ref):
        for cg in range(KV_ROWS):
            o_ref[pl.ds(cg, n, stride=KV_ROWS), :] = x_ref[:, cg * dh:(cg + 1) * dh]

    put(oc_ref, kvc_ref)
    put(os_ref, kvs_ref)

    @pl.when(pl.program_id(0) % per_b == per_b - 1)
    def _():
        put(ow_ref, kvw_ref)


def _cache_rows(proj, batch, seq, depth, layer, prev):
    assert seq % ROWS_BLK == 0 and WINDOW == ROWS_BLK
    per_b = seq // ROWS_BLK
    nblk = batch * per_b
    blk = lambda c: pl.BlockSpec((ROWS_BLK, KVW), lambda i: (i, c // KVW))
    out_rows = ROWS_BLK * KV_ROWS
    return pl.pallas_call(
        functools.partial(_cache_rows_kernel, per_b),
        out_shape=(jax.ShapeDtypeStruct((depth * batch * seq * KV_ROWS, HEAD_DIM), F32),
                   jax.ShapeDtypeStruct((depth * batch * seq * KV_ROWS, HEAD_DIM), F32),
                   jax.ShapeDtypeStruct((depth * batch * WINDOW * KV_ROWS, HEAD_DIM), F32)),
        grid=(nblk,),
        in_specs=[blk(C_KVC), blk(C_KVS),
                  pl.BlockSpec((ROWS_BLK, KVW), lambda i: ((i // per_b) * per_b + per_b - 1, C_KVW // KVW))]
        + [pl.BlockSpec(memory_space=pl.ANY)] * len(prev),
        out_specs=(pl.BlockSpec((out_rows, HEAD_DIM), lambda i: (layer * nblk + i, 0)),
                   pl.BlockSpec((out_rows, HEAD_DIM), lambda i: (layer * nblk + i, 0)),
                   pl.BlockSpec((out_rows, HEAD_DIM), lambda i: (layer * batch + i // per_b, 0))),
        input_output_aliases={3 + a: a for a in range(len(prev))},
        compiler_params=_params(("arbitrary",)),
        name="cache_rows",
    )(proj, proj, proj, *prev)


_BNN = (((2,), (1,)), ((0,), (0,)))
_BNT = (((2,), (2,)), ((0,), (0,)))
_BTN = (((1,), (1,)), ((0,), (0,)))


def _bdot_b(a, b, dims=_BNN):
    return lax.dot_general(a.astype(BF16), b.astype(BF16), dims, preferred_element_type=F32)


def _tri_masks():
    r = lax.broadcasted_iota(I32, (CHUNK, CHUNK), 0)
    c = lax.broadcasted_iota(I32, (CHUNK, CHUNK), 1)
    return r, c


def _gla_prompt_kernel(q_ref, k_ref, v_ref, gg_ref, sm_ref, w2_ref, b2_ref, nw_ref, o_ref, so_ref, s_ref):
    t = pl.program_id(1)

    @pl.when(t == 0)
    def _():
        s_ref[...] = jnp.zeros_like(s_ref)

    r, c = _tri_masks()
    lower = r >= c
    tril = jnp.where(lower, 1.0, 0.0).astype(F32)
    dk, dv = GLA_DK, GLA_DV
    nch = q_ref.shape[0] // CHUNK
    pairs = [(ch, h) for ch in range(nch) for h in range(GLA_HEADS)]
    rs = lambda ch: slice(ch * CHUNK, (ch + 1) * CHUNK)
    stack = lambda fn: jnp.stack([fn(ch, h) for ch, h in pairs], axis=0)

    log_a = _log_sigmoid(_bdot(sm_ref[:, SM_GLR:SM_GLR + GLA_GATE_RANK], w2_ref[...]) + b2_ref[...]) / GLA_GATE_NORM
    width = GLA_HEADS * dk
    b_all = _hdot(tril, jnp.concatenate([log_a[rs(ch)] for ch in range(nch)], axis=1))
    b_all_t = b_all.T
    b = stack(lambda ch, h: b_all[:, ch * width + h * dk:ch * width + (h + 1) * dk])
    b_last = b[:, CHUNK - 1:CHUNK, :]
    b_last_col = stack(lambda ch, h: b_all_t[ch * width + h * dk:ch * width + (h + 1) * dk, CHUNK - 1:CHUNK])
    q = stack(lambda ch, h: q_ref[rs(ch), h * dk:(h + 1) * dk]) * (dk ** -0.5)
    k = stack(lambda ch, h: k_ref[rs(ch), h * dk:(h + 1) * dk])
    v = stack(lambda ch, h: v_ref[rs(ch), h * dv:(h + 1) * dv])
    qe = q * jnp.exp(b)
    att = jnp.where(lower[None], _bdot_b(qe, k * jnp.exp(-b), _BNT), 0.0)
    o_intra = _bdot_b(att, v)
    kv = _bdot_b(k * jnp.exp(b_last - b), v, _BTN)
    s_decay = jnp.exp(b_last_col)

    for p, (ch, h) in enumerate(pairs):
        s = s_ref[h]
        o = _bdot(qe[p], s) + o_intra[p]
        s_ref[h] = s_decay[p] * s + kv[p]
        o_ref[rs(ch), h * dv:(h + 1) * dv] = _rms(o, nw_ref[...]) * _silu(gg_ref[rs(ch), h * dv:(h + 1) * dv])
    so_ref[0] = s_ref[...]


def _gla_prompt(proj, w2, b2, nw, batch, seq):
    nt = seq // TB
    row = lambda w, col: pl.BlockSpec((TB, w), lambda b, t: (b * nt + t, col // w))
    full = lambda shape: pl.BlockSpec(shape, lambda b, t: (0,) * len(shape))
    return pl.pallas_call(
        _gla_prompt_kernel,
        out_shape=(jax.ShapeDtypeStruct((batch * seq, GLA_WIDTH), F32),
                   jax.ShapeDtypeStruct((batch, GLA_HEADS, GLA_DK, GLA_DV), F32)),
        grid=(batch, nt),
        in_specs=[row(GLA_HEADS * GLA_DK, C_GQ), row(GLA_HEADS * GLA_DK, C_GK), row(GLA_WIDTH, C_GV),
                  row(GLA_WIDTH, C_GG), row(128, C_SM), full(w2.shape), full(b2.shape), full(nw.shape)],
        out_specs=(pl.BlockSpec((TB, GLA_WIDTH), lambda b, t: (b * nt + t, 0)),
                   pl.BlockSpec((1, GLA_HEADS, GLA_DK, GLA_DV), lambda b, t: (b, 0, 0, 0))),
        scratch_shapes=[pltpu.VMEM((GLA_HEADS, GLA_DK, GLA_DV), F32)],
        compiler_params=_params(("parallel", "arbitrary")),
        name="gla_prompt",
    )(proj, proj, proj, proj, proj, w2, b2, nw)


def _unit_lower_inverse(m, r, c):
    eye = jnp.where(r == c, 1.0, 0.0).astype(F32)[None]
    base = 8
    m8 = jnp.where(((r // base) == (c // base))[None], m, 0.0)
    m2 = _bdot_b(m8, m8)
    m4 = _bdot_b(m2, m2)
    t = _bdot_b(_bdot_b(eye - m8, eye + m2), eye + m4)
    s = base
    while s < CHUNK:
        off = ((r // (2 * s)) == (c // (2 * s))) & ((r // s) != (c // s))
        t = t - _bdot_b(t, _bdot_b(jnp.where(off[None], m, 0.0), t))
        s *= 2
    return t


def _gdn_prompt_kernel(x_ref, dz_ref, sm_ref, cw_ref, al_ref, dt_ref, nw_ref, o_ref, so_ref, s_ref, tail_ref):
    t = pl.program_id(1)

    @pl.when(t == 0)
    def _():
        s_ref[...] = jnp.zeros_like(s_ref)
        tail_ref[...] = jnp.zeros_like(tail_ref)

    tb = x_ref.shape[0]
    x = x_ref[...]
    xc = jnp.concatenate([tail_ref[...], x], axis=0)
    off = 8 - (CONV_W - 1)
    y = xc[off:off + tb] * cw_ref[0:1, :]
    for jw in range(1, CONV_W):
        y = y + xc[off + jw:off + jw + tb] * cw_ref[jw:jw + 1, :]
    y = _silu(y)
    tail_ref[...] = x[tb - 8:tb]

    beta_all = _sigmoid(sm_ref[:, SM_DB:SM_DB + GDN_HEADS])
    g_all = -jnp.exp(al_ref[...]) * _softplus(sm_ref[:, SM_DA:SM_DA + GDN_HEADS] + dt_ref[...])
    r, c = _tri_masks()
    lower = r >= c
    strict = r > c
    tril = jnp.where(lower, 1.0, 0.0).astype(F32)
    dk, dv = GDN_DK, GDN_DV
    nch = tb // CHUNK
    pairs = [(ch, h) for ch in range(nch) for h in range(GDN_HEADS)]
    rs = lambda ch: slice(ch * CHUNK, (ch + 1) * CHUNK)
    stack = lambda fn: jnp.stack([fn(ch, h) for ch, h in pairs], axis=0)

    cq = stack(lambda ch, h: y[rs(ch), h * dk:(h + 1) * dk])
    ck = stack(lambda ch, h: y[rs(ch), GDN_WIDTH + h * dk:GDN_WIDTH + (h + 1) * dk])
    v = stack(lambda ch, h: y[rs(ch), 2 * GDN_WIDTH + h * dv:2 * GDN_WIDTH + (h + 1) * dv])
    q = cq * lax.rsqrt(jnp.sum(cq * cq, axis=-1, keepdims=True) + 1e-6) * (dk ** -0.5)
    k = ck * lax.rsqrt(jnp.sum(ck * ck, axis=-1, keepdims=True) + 1e-6)
    beta = stack(lambda ch, h: jnp.broadcast_to(beta_all[rs(ch), h:h + 1], (CHUNK, dk)))
    g_cols = jnp.concatenate([g_all[rs(ch)] for ch in range(nch)], axis=1)
    gam_cols = _hdot(tril, g_cols)
    gam_rows = gam_cols.T
    gam = jnp.stack([jnp.broadcast_to(gam_cols[:, p:p + 1], (CHUNK, dk)) for p in range(len(pairs))], axis=0)
    decay = jnp.exp(jnp.where(lower[None], gam[:, :, 0:CHUNK] - gam_rows[:, None, :], NEG))
    kb = k * beta
    m = jnp.where(strict[None], _bdot_b(kb, k, _BNT) * decay, 0.0)
    tinv = _unit_lower_inverse(m, r, c)
    eg = jnp.exp(gam)
    u = _bdot_b(tinv, v * beta)
    w = _bdot_b(tinv, kb * eg)
    att = _bdot_b(q, k, _BNT) * decay
    qe = q * eg
    g_last = gam[:, CHUNK - 1:CHUNK, :]
    kd = k * jnp.exp(g_last - gam)
    eg_last = jnp.exp(g_last)

    for p, (ch, h) in enumerate(pairs):
        s = s_ref[h]
        v_new = u[p] - _bdot(w[p], s)
        o = _bdot(qe[p], s) + _bdot(att[p], v_new)
        s_ref[h] = eg_last[p] * s + _bdot(kd[p], v_new, _TN)
        o_ref[rs(ch), h * dv:(h + 1) * dv] = _rms(o, nw_ref[...]) * _silu(dz_ref[rs(ch), h * dv:(h + 1) * dv])
    so_ref[0] = s_ref[...]


def _gdn_prompt(proj, cw, a_log, dt_bias, nw, batch, seq):
    nt = seq // TB
    row = lambda w, col: pl.BlockSpec((TB, w), lambda b, t: (b * nt + t, col // w))
    full = lambda shape: pl.BlockSpec(shape, lambda b, t: (0,) * len(shape))
    return pl.pallas_call(
        _gdn_prompt_kernel,
        out_shape=(jax.ShapeDtypeStruct((batch * seq, GDN_WIDTH), F32),
                   jax.ShapeDtypeStruct((batch, GDN_HEADS, GDN_DK, GDN_DV), F32)),
        grid=(batch, nt),
        in_specs=[row(3 * GDN_WIDTH, C_DQKV), row(GDN_WIDTH, C_DZ), row(128, C_SM),
                  full(cw.shape), full(a_log.shape), full(dt_bias.shape), full(nw.shape)],
        out_specs=(pl.BlockSpec((TB, GDN_WIDTH), lambda b, t: (b * nt + t, 0)),
                   pl.BlockSpec((1, GDN_HEADS, GDN_DK, GDN_DV), lambda b, t: (b, 0, 0, 0))),
        scratch_shapes=[pltpu.VMEM((GDN_HEADS, GDN_DK, GDN_DV), F32), pltpu.VMEM((8, 3 * GDN_WIDTH), F32)],
        compiler_params=_params(("parallel", "arbitrary")),
        name="gdn_prompt",
    )(proj, proj, proj, cw, a_log, dt_bias, nw)


def _masked_softmax_rows(s):
    valid = s > -1e29
    p = jnp.where(valid, jnp.exp(s - jnp.max(s, axis=-1, keepdims=True)), 0.0)
    return p / jnp.maximum(jnp.sum(p, axis=-1, keepdims=True), 1e-30)


def _head_rows(q_ref, sb, lo, hi):
    dh = HEAD_DIM
    q = jnp.concatenate([q_ref[sb, :, h * dh:(h + 1) * dh] for h in range(lo, hi)], axis=0)
    return (q * (dh ** -0.5)).astype(BF16)


def _kv_rows(kv_ref, sb):
    dh = HEAD_DIM
    return jnp.concatenate([kv_ref[sb, :, r * dh:(r + 1) * dh] for r in range(KV_ROWS)], axis=0)


def _value_weights(p):
    return pltpu.roll(p, NSA_KV_HEADS, 1)


SN = 2


def _nsa_sample_a_kernel(n_pages, nb_past, pt_ref, q_ref, kvc_ref, kvw_ref, sm_ref, *rest):
    page_refs = rest[:SN * n_pages]
    win_ref, w4_ref, bc_ref, bw_ref = rest[SN * n_pages:SN * n_pages + 4]
    part_ref, idx_ref, wo_ref, cmp_ref = rest[-4:]
    dh = HEAD_DIM
    blk_rows = CMP_BLOCK * KV_ROWS
    bpp = PAGE_SIZE // CMP_BLOCK
    wl = win_ref.shape[0] // SN
    w4 = w4_ref[...]
    lanes = cmp_ref.shape[1]
    lane = lax.broadcasted_iota(I32, (NSA_KV_HEADS, lanes), 1)
    gi = lax.broadcasted_iota(I32, (NSA_KV_HEADS, lanes), 0)
    n = lane // KV_ROWS
    cur = nb_past
    cand = ((lane % KV_ROWS) == gi) & (n <= cur)
    forced = (n == cur) | (n == cur - 1) | (n == 0)
    li = lax.broadcasted_iota(I32, (NSA_KV_HEADS, N_SELECT), 1)

    for sb in range(SN):
        for p in range(n_pages):
            x = page_refs[sb * n_pages + p][...]
            sums = []
            for half in range(bpp):
                pr = x[half * blk_rows:(half + 1) * blk_rows] * w4
                s8 = jnp.sum(pr.reshape(blk_rows // 8, 8, dh), axis=0)
                sums.append(s8[0:KV_ROWS] + s8[KV_ROWS:2 * KV_ROWS])
            cmp_ref[sb, p * bpp * KV_ROWS:(p + 1) * bpp * KV_ROWS, :] = jnp.concatenate(sums, axis=0)
        r0 = nb_past * KV_ROWS
        cmp_ref[sb, r0:r0 + KV_ROWS, :] = _kv_rows(kvc_ref, sb) * w4[0:KV_ROWS]
        cmp_ref[sb, r0 + KV_ROWS:, :] = jnp.zeros((lanes - r0 - KV_ROWS, dh), F32)
        cm16 = cmp_ref[sb].astype(BF16)

        q16 = _head_rows(q_ref, sb, 0, NSA_HEADS)
        p = _masked_softmax_rows(_bdot(q16, cm16, _NT) + bc_ref[...])
        o_c = _bdot(_value_weights(p), cm16)

        imp = jnp.concatenate([jnp.sum(p[g * NSA_GROUP:(g + 1) * NSA_GROUP], axis=0, keepdims=True)
                               for g in range(NSA_KV_HEADS)], axis=0)
        score = jnp.where(cand, jnp.where(forced, SEL_FORCE, imp), -3e38)
        top = jnp.zeros((NSA_KV_HEADS, N_SELECT), I32)
        for r in range(N_SELECT):
            a = jnp.argmax(score, axis=-1, keepdims=True).astype(I32)
            top = jnp.where(li == r, a // KV_ROWS, top)
            score = jnp.where(lane == a, -3e38, score)
        idx_ref[sb] = top

        ws = slice(sb * wl, (sb + 1) * wl)
        wo_ref[ws, :] = pltpu.roll(win_ref[ws, :], wl - KV_ROWS, 0)
        wo_ref[(sb + 1) * wl - KV_ROWS:(sb + 1) * wl, :] = _kv_rows(kvw_ref, sb)
        w16 = wo_ref[ws, :].astype(BF16)
        pw = _masked_softmax_rows(_bdot(q16, w16, _NT) + bw_ref[...])
        o_w = _bdot(_value_weights(pw), w16)

        gates = _sigmoid(sm_ref[sb, :, SM_GATE:SM_GATE + 3 * NSA_HEADS])
        for h in range(NSA_HEADS):
            c0 = 3 * h
            part_ref[sb, :, h * dh:(h + 1) * dh] = (
                o_c[h:h + 1] * gates[:, c0:c0 + 1] + o_w[h:h + 1] * gates[:, c0 + 2:c0 + 3])


def _nsa_sample_a(page_table, proj_s3, pool_cmp, win_rows, w4, bias_c, bias_w, layer, n_phys, prev_win):
    nbatch, n_pages = page_table.shape
    nb_past = n_pages * PAGE_SIZE // CMP_BLOCK
    wl = bias_w.shape[1]
    page_rows = PAGE_SIZE * KV_ROWS
    nsteps = nbatch // SN
    col = lambda w, c: pl.BlockSpec((SN, 1, w), lambda i, pt: (i, 0, c // w))
    full = lambda shape: pl.BlockSpec(shape, lambda i, pt: (0,) * len(shape))
    page = lambda sb, p: pl.BlockSpec((page_rows, HEAD_DIM), lambda i, pt: (layer * n_phys + pt[SN * i + sb, p], 0))
    gs = pltpu.PrefetchScalarGridSpec(
        num_scalar_prefetch=1, grid=(nsteps,),
        in_specs=[col(NSA_WIDTH, C_NQ), col(KVW, C_KVC), col(KVW, C_KVW), col(128, C_SM)]
        + [page(sb, p) for sb in range(SN) for p in range(n_pages)]
        + [pl.BlockSpec((SN * wl, HEAD_DIM), lambda i, pt: (layer * nsteps + i, 0)),
           full(w4.shape), full(bias_c.shape), full(bias_w.shape)]
        + [pl.BlockSpec(memory_space=pl.ANY)] * len(prev_win),
        out_specs=(pl.BlockSpec((SN, 1, NSA_WIDTH), lambda i, pt: (i, 0, 0)),
                   pl.BlockSpec((SN, NSA_KV_HEADS, N_SELECT), lambda i, pt: (i, 0, 0)),
                   pl.BlockSpec((SN * wl, HEAD_DIM), lambda i, pt: (layer * nsteps + i, 0))),
        scratch_shapes=[pltpu.VMEM((SN, CMP_PAD * KV_ROWS, HEAD_DIM), F32)])
    return pl.pallas_call(
        functools.partial(_nsa_sample_a_kernel, n_pages, nb_past),
        out_shape=(jax.ShapeDtypeStruct((nbatch, 1, NSA_WIDTH), F32),
                   jax.ShapeDtypeStruct((nbatch, NSA_KV_HEADS, N_SELECT), I32),
                   jax.ShapeDtypeStruct(win_rows.shape, F32)),
        grid_spec=gs,
        input_output_aliases={1 + 4 + SN * n_pages + 4 + a: 2 for a in range(len(prev_win))},
        compiler_params=_params(("arbitrary",)),
        name="nsa_sample_a",
    )(page_table, proj_s3, proj_s3, proj_s3, proj_s3, *([pool_cmp] * (SN * n_pages)), win_rows, w4, bias_c, bias_w,
      *prev_win)


def _nsa_sample_b_kernel(nb_past, pt_ref, ix_ref, q_ref, kvs_ref, sm_ref, part_ref, *rest):
    nsel = NSA_KV_HEADS * N_SELECT
    blk_refs = rest[:SN * nsel]
    bs_ref, o_ref = rest[SN * nsel:]
    i0 = pl.program_id(0) * SN
    hg, dh = NSA_GROUP, HEAD_DIM
    blk_rows = CMP_BLOCK * KV_ROWS
    q16 = jnp.concatenate([_head_rows(q_ref, sb, 0, NSA_HEADS) for sb in range(SN)], axis=0)
    masked = jnp.full((hg, blk_rows), NEG, F32)
    keys, bias = [], []
    for sb in range(SN):
        new_blk = jnp.concatenate([_kv_rows(kvs_ref, sb), jnp.zeros((blk_rows - KV_ROWS, dh), F32)], axis=0)
        for g in range(NSA_KV_HEADS):
            owner = sb * NSA_KV_HEADS + g
            for i in range(N_SELECT):
                idx = ix_ref[i0 + sb, g * N_SELECT + i]
                past_blk = blk_refs[sb * nsel + g * N_SELECT + i][...]
                keys.append(jnp.where(idx >= nb_past, new_blk, past_blk).astype(BF16))
                bt = bs_ref[jnp.clip(idx, 0, nb_past), g * hg:(g + 1) * hg, :]
                bias.append(jnp.concatenate([bt if rg == owner else masked
                                             for rg in range(SN * NSA_KV_HEADS)], axis=0))
    k_all = jnp.concatenate(keys, axis=0)
    p = _masked_softmax_rows(_bdot(q16, k_all, _NT) + jnp.concatenate(bias, axis=1))
    o_s = _bdot(_value_weights(p), k_all)
    for sb in range(SN):
        gates = _sigmoid(sm_ref[sb, :, SM_GATE:SM_GATE + 3 * NSA_HEADS])
        for h in range(NSA_HEADS):
            cs = slice(h * dh, (h + 1) * dh)
            r = sb * NSA_HEADS + h
            o_ref[sb, :, cs] = part_ref[sb, :, cs] + o_s[r:r + 1] * gates[:, 3 * h + 1:3 * h + 2]


def _nsa_sample_b(page_table, top_idx, proj_s3, part, pool_sel, bias_s, layer, n_phys):
    nbatch, n_pages = page_table.shape
    nb_past = n_pages * PAGE_SIZE // CMP_BLOCK
    bpp = PAGE_SIZE // CMP_BLOCK
    nsel = NSA_KV_HEADS * N_SELECT
    col = lambda w, c: pl.BlockSpec((SN, 1, w), lambda i, pt, ix: (i, 0, c // w))

    def blk(sb, gi):
        def imap(i, pt, ix):
            b = SN * i + sb
            ip = jnp.clip(ix[b, gi], 0, nb_past - 1)
            return ((layer * n_phys + pt[b, ip // bpp]) * bpp + ip % bpp, 0)
        return pl.BlockSpec((CMP_BLOCK * KV_ROWS, HEAD_DIM), imap)

    gs = pltpu.PrefetchScalarGridSpec(
        num_scalar_prefetch=2, grid=(nbatch // SN,),
        in_specs=[col(NSA_WIDTH, C_NQ), col(KVW, C_KVS), col(128, C_SM),
                  pl.BlockSpec((SN, 1, NSA_WIDTH), lambda i, pt, ix: (i, 0, 0))]
        + [blk(sb, gi) for sb in range(SN) for gi in range(nsel)]
        + [pl.BlockSpec(bias_s.shape, lambda i, pt, ix: (0, 0, 0))],
        out_specs=pl.BlockSpec((SN, 1, NSA_WIDTH), lambda i, pt, ix: (i, 0, 0)))
    return pl.pallas_call(
        functools.partial(_nsa_sample_b_kernel, nb_past),
        out_shape=jax.ShapeDtypeStruct((nbatch, 1, NSA_WIDTH), F32),
        grid_spec=gs,
        compiler_params=_params(("arbitrary",)),
        name="nsa_sample_b",
    )(page_table, top_idx, proj_s3, proj_s3, proj_s3, part, *([pool_sel] * (SN * nsel)), bias_s)


SB = 8


def _rec_sample_kernel(gq_ref, gk_ref, gv_ref, gg_ref, x_ref, dz_ref, sm_ref, sg_ref, sd_ref, cb_ref,
                       w2_ref, b2_ref, gnw_ref, cw_ref, al_ref, dt_ref, dnw_ref, *rest):
    o_ref, sgo_ref, sdo_ref = rest[-3:]
    sm = sm_ref[...]
    ri = lax.broadcasted_iota(I32, (SB, 128), 0)
    log_a = _log_sigmoid(_bdot(sm[:, SM_GLR:SM_GLR + GLA_GATE_RANK], w2_ref[...]) + b2_ref[...]) / GLA_GATE_NORM
    ea_t = jnp.exp(log_a).T
    k_t = gk_ref[...].T
    q_t = (gq_ref[...] * (GLA_DK ** -0.5)).T
    gv = gv_ref[...]
    for h in range(GLA_HEADS):
        hs = slice(h * GLA_DK, (h + 1) * GLA_DK)
        vs = slice(h * GLA_DV, (h + 1) * GLA_DV)
        o_h = jnp.zeros((SB, GLA_DV), F32)
        for i in range(SB):
            s = ea_t[hs, i:i + 1] * sg_ref[i, h] + k_t[hs, i:i + 1] * gv[i:i + 1, vs]
            sgo_ref[i, h] = s
            o = jnp.sum(q_t[hs, i:i + 1] * s, axis=0, keepdims=True)
            o_h = jnp.where(ri == i, o, o_h)
        o_ref[:, vs] = _rms(o_h, gnw_ref[...]) * _silu(gg_ref[:, vs])
    y = x_ref[...] * cw_ref[CONV_W - 1:CONV_W, :]
    for jw in range(CONV_W - 1):
        y = y + cb_ref[jw] * cw_ref[jw:jw + 1, :]
    y = _silu(y)
    beta = _sigmoid(sm[:, SM_DB:SM_DB + GDN_HEADS])
    eg = jnp.exp(-jnp.exp(al_ref[...]) * _softplus(sm[:, SM_DA:SM_DA + GDN_HEADS] + dt_ref[...]))
    dk, dv = GDN_DK, GDN_DV
    for h in range(GDN_HEADS):
        cq = y[:, h * dk:(h + 1) * dk]
        ck = y[:, GDN_WIDTH + h * dk:GDN_WIDTH + (h + 1) * dk]
        v = y[:, 2 * GDN_WIDTH + h * dv:2 * GDN_WIDTH + (h + 1) * dv]
        q_t = (cq * lax.rsqrt(jnp.sum(cq * cq, axis=-1, keepdims=True) + 1e-6) * (dk ** -0.5)).T
        k_t = (ck * lax.rsqrt(jnp.sum(ck * ck, axis=-1, keepdims=True) + 1e-6)).T
        o_h = jnp.zeros((SB, dv), F32)
        for i in range(SB):
            s = eg[i:i + 1, h:h + 1] * sd_ref[i, h]
            kc = k_t[:, i:i + 1]
            delta = (v[i:i + 1] - jnp.sum(kc * s, axis=0, keepdims=True)) * beta[i:i + 1, h:h + 1]
            s = s + kc * delta
            sdo_ref[i, h] = s
            o = jnp.sum(q_t[:, i:i + 1] * s, axis=0, keepdims=True)
            o_h = jnp.where(ri == i, o, o_h)
        vs = slice(GLA_WIDTH + h * dv, GLA_WIDTH + (h + 1) * dv)
        o_ref[:, vs] = _rms(o_h, dnw_ref[...]) * _silu(dz_ref[:, h * dv:(h + 1) * dv])


def _rec_sample(proj_s, state_gla, state_gdn, conv_t, w2, b2, gnw, cw, a_log, dt_bias, dnw, layer, prev_states):
    nbatch = proj_s.shape[0]
    nblk = nbatch // SB
    depth = state_gla.shape[0] // nbatch
    alias_specs = [pl.BlockSpec(memory_space=pl.ANY)] * len(prev_states)
    n_in = 17
    row = lambda w, col: pl.BlockSpec((SB, w), lambda i: (i, col // w))
    full = lambda shape: pl.BlockSpec(shape, lambda i: (0,) * len(shape))
    return pl.pallas_call(
        _rec_sample_kernel,
        out_shape=(jax.ShapeDtypeStruct((nbatch, GLA_WIDTH + GDN_WIDTH), F32),
                   jax.ShapeDtypeStruct(state_gla.shape, F32),
                   jax.ShapeDtypeStruct(state_gdn.shape, F32)),
        grid=(nblk,),
        in_specs=[row(GLA_HEADS * GLA_DK, C_GQ), row(GLA_HEADS * GLA_DK, C_GK), row(GLA_WIDTH, C_GV),
                  row(GLA_WIDTH, C_GG), row(3 * GDN_WIDTH, C_DQKV), row(GDN_WIDTH, C_DZ), row(128, C_SM),
                  pl.BlockSpec((SB, GLA_HEADS, GLA_DK, GLA_DV), lambda i: (layer * nblk + i, 0, 0, 0)),
                  pl.BlockSpec((SB, GDN_HEADS, GDN_DK, GDN_DV), lambda i: (layer * nblk + i, 0, 0, 0)),
                  pl.BlockSpec((CONV_W - 1, SB, 3 * GDN_WIDTH), lambda i: (0, i, 0)),
                  full(w2.shape), full(b2.shape), full(gnw.shape), full(cw.shape), full(a_log.shape),
                  full(dt_bias.shape), full(dnw.shape)] + alias_specs,
        out_specs=(pl.BlockSpec((SB, GLA_WIDTH + GDN_WIDTH), lambda i: (i, 0)),
                   pl.BlockSpec((SB, GLA_HEADS, GLA_DK, GLA_DV), lambda i: (layer * nblk + i, 0, 0, 0)),
                   pl.BlockSpec((SB, GDN_HEADS, GDN_DK, GDN_DV), lambda i: (layer * nblk + i, 0, 0, 0))),
        input_output_aliases={n_in + a: 1 + a for a in range(len(prev_states))},
        compiler_params=_params(("parallel",)),
        name="rec_sample",
    )(proj_s, proj_s, proj_s, proj_s, proj_s, proj_s, proj_s, state_gla, state_gdn, conv_t,
      w2, b2, gnw, cw, a_log, dt_bias, dnw, *prev_states)


def _reorder_w_in(w):
    d = w.shape[0]
    return jnp.concatenate([
        w[:, 0:2560],
        w[:, 2584:3096],
        w[:, 4136:5672],
        w[:, 3096:3608],
        w[:, 3624:4136],
        w[:, 5672:6184],
        w[:, 2560:2584],
        w[:, 3608:3624],
        w[:, 6184:6192],
        jnp.zeros((d, PROJ_N - 6192), w.dtype)], axis=1)


def _cmp_weight_tile(w_cmp):
    half = NSA_KV_HEADS * HEAD_DIM
    return jnp.concatenate([jnp.broadcast_to(w_cmp[:, 0:1], (CMP_BLOCK, half)),
                            jnp.broadcast_to(w_cmp[:, 1:2], (CMP_BLOCK, half))], axis=1).astype(F32)


def _cmp_weight_rows(w_cmp):
    w = jnp.repeat(w_cmp, NSA_KV_HEADS, axis=1).reshape(CMP_BLOCK * KV_ROWS, 1)
    return jnp.broadcast_to(w, (CMP_BLOCK * KV_ROWS, HEAD_DIM)).astype(F32)


def kernel(x_prompt, x_sample, cache_cmp, cache_sel, cache_win, state_gla, state_gdn, state_conv, page_table,
           p_prompt, p_sample, norm_w, ffn_w1, ffn_w3, ffn_w2, w_in, w_out, nsa_w_cmp, t5_bias,
           gla_w_gk2, gla_b_gk, gla_norm_w, gdn_conv_w, gdn_a_log, gdn_dt_bias, gdn_norm_w,
           ple_w_proj, ple_w_gate):
    depth = w_in.shape[0]
    batch, seq, d = x_prompt.shape
    nbatch = x_sample.shape[0]
    n_phys = cache_cmp.shape[1]
    n_pages = page_table.shape[1]
    past = n_pages * PAGE_SIZE
    win_len = cache_win.shape[2]
    rows_p = batch * seq
    kv_shape = (2, NSA_KV_HEADS, HEAD_DIM)

    x = jnp.concatenate([x_prompt.reshape(rows_p, d), x_sample.reshape(nbatch, d)], axis=0)
    assert win_len == WINDOW and past // CMP_BLOCK + 1 >= N_SELECT and past // CMP_BLOCK < CMP_PAD
    assert seq >= WINDOW
    pool_cmp = cache_cmp.reshape(-1, HEAD_DIM)
    pool_sel = cache_sel.reshape(-1, HEAD_DIM)
    win_all = cache_win.reshape(-1, HEAD_DIM)
    sgla_all = state_gla.reshape((depth * nbatch,) + state_gla.shape[2:])
    sgdn_all = state_gdn.reshape((depth * nbatch,) + state_gdn.shape[2:])

    pb_c, pb_s, pb_w = _prompt_bias_tables(t5_bias, seq)
    sb_c, sb_s, sb_w = _sample_bias_tables(t5_bias, past, win_len)
    nb = seq // CMP_BLOCK
    expand = jnp.asarray(-MASK_BIG * np.repeat(np.eye(nb, dtype=np.float32), CMP_BLOCK, axis=1)
                         .reshape(nb, seq // KEY_CHUNK, KEY_CHUNK).transpose(1, 0, 2), BF16)

    outs = {k: [] for k in ("cmp_s", "sel_s", "gla_p", "gdn_p", "conv_p", "conv_s")}
    prev_win, prev_states, prev_rows = (), (), ()
    for l in range(depth):
        nw = norm_w[l].reshape(-1, 1, d)
        w1 = ffn_w1[l].astype(BF16)
        w3 = ffn_w3[l].astype(BF16)
        w2 = ffn_w2[l].astype(BF16)
        x = _ffn(x, nw[0], w1[0], w3[0], w2[0], nw[1])

        proj = _proj(x, nw[2], _reorder_w_in(w_in[l].astype(BF16)))
        proj_s = proj[rows_p:]
        proj_s3 = proj_s.reshape(nbatch, 1, PROJ_N)
        wtile = _cmp_weight_tile(nsa_w_cmp[l])
        w_gk2 = gla_w_gk2[l]
        b_gk = gla_b_gk[l].reshape(1, -1)
        gla_nw = gla_norm_w[l].reshape(1, -1)
        gdn_nw = gdn_norm_w[l].reshape(1, -1)
        conv_w = gdn_conv_w[l]
        a_log = gdn_a_log[l].reshape(1, -1)
        dt_bias = gdn_dt_bias[l].reshape(1, -1)

        cmpkv = _compress_prompt(proj, wtile, rows_p)
        kvs16 = proj[:rows_p, C_KVS:C_KVS + KVW].astype(BF16)
        kvw16 = jnp.pad(proj[:rows_p, C_KVW:C_KVW + KVW].astype(BF16).reshape(batch, seq, KVW),
                        ((0, 0), (WINDOW, 0), (0, 0))).reshape(batch * (WINDOW + seq), KVW)
        o_nsa = _nsa_prompt(proj, cmpkv, kvs16, kvw16, pb_c, pb_s, pb_w, expand, batch, seq)
        o_gla, s_gla_p = _gla_prompt(proj, w_gk2, b_gk, gla_nw, batch, seq)
        o_gdn, s_gdn_p = _gdn_prompt(proj, conv_w, a_log, dt_bias, gdn_nw, batch, seq)

        part, top_idx, win_new = _nsa_sample_a(page_table, proj_s3, pool_cmp, win_all, _cmp_weight_rows(nsa_w_cmp[l]),
                                               sb_c, sb_w, l, n_phys, prev_win)
        prev_win = (win_new,)
        o_nsa_s = _nsa_sample_b(page_table, top_idx.reshape(nbatch, NSA_KV_HEADS * N_SELECT), proj_s3, part,
                                pool_sel, sb_s, l, n_phys)
        conv_t = jnp.swapaxes(state_conv[l], 0, 1)
        o_rec_s, s_gla_s, s_gdn_s = _rec_sample(proj_s, sgla_all, sgdn_all, conv_t, w_gk2, b_gk, gla_nw,
                                                conv_w, a_log, dt_bias, gdn_nw, l, prev_states)
        prev_states = (s_gla_s, s_gdn_s)

        y = jnp.concatenate([jnp.concatenate([o_nsa, o_gla, o_gdn], axis=1),
                             jnp.concatenate([o_nsa_s.reshape(nbatch, NSA_WIDTH), o_rec_s], axis=1)], axis=0)
        x = _outproj(y, x, w_out[l].astype(BF16), nw[3])
        x = _ffn(x, nw[4], w1[1], w3[1], w2[1], nw[5])
        p = jnp.concatenate([p_prompt[l].reshape(rows_p, -1), p_sample[l].reshape(nbatch, -1)], axis=0)
        x = _ple(x, p, nw[6], ple_w_gate[l].astype(BF16), ple_w_proj[l].astype(BF16), nw[7],
                 split_rows=rows_p if l == depth - 1 else None)

        prev_rows = _cache_rows(proj, batch, seq, depth, l, prev_rows)
        outs["cmp_s"].append(proj_s[:, C_KVC:C_KVC + KVW].reshape((nbatch, 1) + kv_shape))
        outs["sel_s"].append(proj_s[:, C_KVS:C_KVS + KVW].reshape((nbatch, 1) + kv_shape))
        outs["gla_p"].append(s_gla_p)
        outs["gdn_p"].append(s_gdn_p)
        outs["conv_p"].append(jnp.stack([proj[(b + 1) * seq - (CONV_W - 1):(b + 1) * seq, C_DQKV:C_DQKV + 3 * GDN_WIDTH]
                                         for b in range(batch)]))
        outs["conv_s"].append(jnp.concatenate(
            [state_conv[l][:, 1:], proj_s[:, None, C_DQKV:C_DQKV + 3 * GDN_WIDTH]], axis=1))

    st = lambda k: jnp.stack(outs[k])
    return (x[0].reshape(batch, seq, d), x[1].reshape(nbatch, 1, d),
            prev_rows[0].reshape((depth, batch, seq) + kv_shape), st("cmp_s"),
            prev_rows[1].reshape((depth, batch, seq) + kv_shape), st("sel_s"),
            prev_rows[2].reshape((depth, batch, WINDOW) + kv_shape), win_new.reshape(cache_win.shape),
            st("gla_p"), s_gla_s.reshape(state_gla.shape), st("gdn_p"), s_gdn_s.reshape(state_gdn.shape),
            st("conv_p"), st("conv_s"))
```

```python
import functools
import math

import numpy as np
import jax
import jax.numpy as jnp
from jax import lax
from jax.experimental import pallas as pl
from jax.experimental.pallas import tpu as pltpu

F32 = jnp.float32
BF16 = jnp.bfloat16
I32 = jnp.int32

D_MODEL = 2048
HEAD_DIM = 128
NSA_HEADS = 8
NSA_KV_HEADS = 2
NSA_GROUP = NSA_HEADS // NSA_KV_HEADS
CMP_BLOCK = 64
N_SELECT = 8
WINDOW = 512
Q_BLOCK = 128
SEL_FORCE = 1.0e4
GLA_HEADS = 4
GLA_DK = 64
GLA_DV = 128
GLA_GATE_RANK = 16
GLA_GATE_NORM = 16.0
GDN_HEADS = 4
GDN_DK = 128
GDN_DV = 128
CHUNK = 64
CONV_W = 4
N_BUCKETS = 32
T5_MAX_DIST = 128
D_FF = 5632
PLE_DIM = 256
RMS_EPS = 1e-6
NEG = -1e30
MASK_BIG = 2.0 ** 100
PAGE_SIZE = 128

NSA_WIDTH = NSA_HEADS * HEAD_DIM
KVW = 2 * NSA_KV_HEADS * HEAD_DIM
KV_ROWS = 2 * NSA_KV_HEADS
CMP_PAD = 64
GDN_WIDTH = GDN_HEADS * GDN_DV
GLA_WIDTH = GLA_HEADS * GLA_DV
MIX_WIDTH = NSA_WIDTH + GLA_WIDTH + GDN_WIDTH

C_NQ = 0
C_KVC = 1024
C_KVS = 1536
C_KVW = 2048
C_GQ = 2560
C_GK = 2816
C_DQKV = 3072
C_GV = 4608
C_GG = 5120
C_DZ = 5632
C_SM = 6144
PROJ_N = 6272
SM_GATE, SM_GLR, SM_DB, SM_DA = 0, 24, 40, 44

VMEM_LIMIT = 56 * 1024 * 1024
TM = 640
TF = 512
TN_PROJ = 896
TB = 256


def _sigmoid(x):
    return 1.0 / (1.0 + jnp.exp(-x))


def _silu(x):
    return x * _sigmoid(x)


def _softplus(x):
    return jnp.maximum(x, 0.0) + jnp.log(1.0 + jnp.exp(-jnp.abs(x)))


def _log_sigmoid(x):
    return jnp.minimum(x, 0.0) - jnp.log(1.0 + jnp.exp(-jnp.abs(x)))


def _rms(x, w):
    return x * lax.rsqrt(jnp.mean(x * x, axis=-1, keepdims=True) + RMS_EPS) * w


_NN = (((1,), (0,)), ((), ()))
_NT = (((1,), (1,)), ((), ()))
_TN = (((0,), (0,)), ((), ()))


def _bdot(a, b, dims=_NN):
    return lax.dot_general(a.astype(BF16), b.astype(BF16), dims, preferred_element_type=F32)


def _hdot(a, b, dims=_NN):
    return lax.dot_general(a, b, dims, preferred_element_type=F32, precision=lax.Precision.HIGHEST)


def _params(sem):
    return pltpu.CompilerParams(dimension_semantics=sem, vmem_limit_bytes=VMEM_LIMIT)


def _ffn_kernel(x_ref, nwa_ref, w1_ref, w3_ref, w2_ref, nwb_ref, o_ref, h_ref, acc_ref):
    f = pl.program_id(1)

    @pl.when(f == 0)
    def _():
        h_ref[...] = _rms(x_ref[...], nwa_ref[...]).astype(BF16)
        acc_ref[...] = jnp.zeros_like(acc_ref)

    h = h_ref[...]
    a = jnp.dot(h, w1_ref[...], preferred_element_type=F32)
    b = jnp.dot(h, w3_ref[...], preferred_element_type=F32)
    g = (_silu(a) * b).astype(BF16)
    acc_ref[...] += jnp.dot(g, w2_ref[...], preferred_element_type=F32)

    @pl.when(f == pl.num_programs(1) - 1)
    def _():
        o_ref[...] = x_ref[...] + 0.5 * _rms(acc_ref[...], nwb_ref[...])


def _ffn(x, nwa, w1, w3, w2, nwb, layer, which):
    m, d = x.shape
    dff = w1.shape[-1]
    row = lambda i, f: (i, 0)
    return pl.pallas_call(
        _ffn_kernel,
        out_shape=jax.ShapeDtypeStruct((m, d), F32),
        grid=(m // TM, dff // TF),
        in_specs=[pl.BlockSpec((TM, d), row),
                  pl.BlockSpec((1, d), lambda i, f: (0, 0)),
                  pl.BlockSpec((None, None, d, TF), lambda i, f: (layer, which, 0, f)),
                  pl.BlockSpec((None, None, d, TF), lambda i, f: (layer, which, 0, f)),
                  pl.BlockSpec((None, None, TF, d), lambda i, f: (layer, which, f, 0)),
                  pl.BlockSpec((1, d), lambda i, f: (0, 0))],
        out_specs=pl.BlockSpec((TM, d), row),
        scratch_shapes=[pltpu.VMEM((TM, d), BF16), pltpu.VMEM((TM, d), F32)],
        compiler_params=_params(("parallel", "arbitrary")),
        name="ffn",
    )(x, nwa, w1, w3, w2, nwb)


def _proj_kernel(x_ref, nw_ref, w_ref, o_ref, h_ref):
    @pl.when(pl.program_id(1) == 0)
    def _():
        h_ref[...] = _rms(x_ref[...], nw_ref[...]).astype(BF16)

    o_ref[...] = jnp.dot(h_ref[...], w_ref[...], preferred_element_type=F32)


def _proj(x, nw, w, layer):
    m, d = x.shape
    n = w.shape[-1]
    return pl.pallas_call(
        _proj_kernel,
        out_shape=jax.ShapeDtypeStruct((m, n), F32),
        grid=(m // TM, n // TN_PROJ),
        in_specs=[pl.BlockSpec((TM, d), lambda i, j: (i, 0)),
                  pl.BlockSpec((1, d), lambda i, j: (0, 0)),
                  pl.BlockSpec((None, d, TN_PROJ), lambda i, j: (layer, 0, j))],
        out_specs=pl.BlockSpec((TM, TN_PROJ), lambda i, j: (i, j)),
        scratch_shapes=[pltpu.VMEM((TM, d), BF16)],
        compiler_params=_params(("parallel", "arbitrary")),
        name="proj",
    )(x, nw, w)


def _outproj_kernel(y_ref, x_ref, w_ref, nw_ref, o_ref):
    z = jnp.dot(y_ref[...].astype(BF16), w_ref[...], preferred_element_type=F32)
    o_ref[...] = x_ref[...] + _rms(z, nw_ref[...])


def _outproj(y, x, w, nw, layer):
    m, d = x.shape
    k = y.shape[1]
    return pl.pallas_call(
        _outproj_kernel,
        out_shape=jax.ShapeDtypeStruct((m, d), F32),
        grid=(m // TM,),
        in_specs=[pl.BlockSpec((TM, k), lambda i: (i, 0)),
                  pl.BlockSpec((TM, d), lambda i: (i, 0)),
                  pl.BlockSpec((None, k, d), lambda i: (layer, 0, 0)),
                  pl.BlockSpec((1, d), lambda i: (0, 0))],
        out_specs=pl.BlockSpec((TM, d), lambda i: (i, 0)),
        compiler_params=_params(("parallel",)),
        name="outproj",
    )(y, x, w, nw)


def _ple_kernel(split, x_ref, p_ref, nwa_ref, wg_ref, wp_ref, nwb_ref, *o_refs):
    x = x_ref[...]
    gate = _sigmoid(jnp.dot(_rms(x, nwa_ref[...]).astype(BF16), wg_ref[...], preferred_element_type=F32))
    pp = jnp.dot(p_ref[...].astype(BF16), wp_ref[...], preferred_element_type=F32)
    y = x + _rms(gate * pp, nwb_ref[...])
    o_refs[0][...] = y
    if split is not None:
        @pl.when(pl.program_id(0) == pl.num_programs(0) - 1)
        def _():
            o_refs[1][...] = y[split:]


def _ple(x, p, nwa, wg, wp, nwb, layer, split_rows=None):
    m, d = x.shape
    pd = p.shape[1]
    nblk = m // TM
    out_shape = jax.ShapeDtypeStruct((m, d), F32)
    out_specs = pl.BlockSpec((TM, d), lambda i: (i, 0))
    split = None
    if split_rows is not None:
        split = split_rows - (nblk - 1) * TM
        assert 0 < split and m - split_rows == TM - split
        out_shape = (jax.ShapeDtypeStruct((split_rows, d), F32), jax.ShapeDtypeStruct((m - split_rows, d), F32))
        out_specs = (out_specs, pl.BlockSpec((m - split_rows, d), lambda i: (0, 0)))
    return pl.pallas_call(
        functools.partial(_ple_kernel, split),
        out_shape=out_shape,
        grid=(nblk,),
        in_specs=[pl.BlockSpec((TM, d), lambda i: (i, 0)),
                  pl.BlockSpec((TM, pd), lambda i: (i, 0)),
                  pl.BlockSpec((1, d), lambda i: (0, 0)),
                  pl.BlockSpec((None, d, d), lambda i: (layer, 0, 0)),
                  pl.BlockSpec((None, pd, d), lambda i: (layer, 0, 0)),
                  pl.BlockSpec((1, d), lambda i: (0, 0))],
        out_specs=out_specs,
        compiler_params=_params(("arbitrary",)),
        name="ple",
    )(x, p, nwa, wg, wp, nwb)


def _t5_bucket_np(dist):
    n = np.maximum(dist, 0)
    exact = N_BUCKETS // 2
    val = (np.log(np.maximum(n, 1).astype(np.float32) / np.float32(exact))
           / np.float32(math.log(T5_MAX_DIST / exact)) * np.float32(N_BUCKETS - exact))
    large = exact + val.astype(np.int32)
    return np.where(n < exact, n, np.minimum(large, N_BUCKETS - 1)).astype(np.int32)


def _bias_table(t5_bias, dist, valid):
    bucket = jnp.asarray(np.where(valid, _t5_bucket_np(dist), -1).astype(np.int32))[None]
    hshape = (t5_bias.shape[1],) + (1,) * dist.ndim
    out = jnp.full((t5_bias.shape[1],) + dist.shape, NEG, F32)
    for k in range(N_BUCKETS):
        out = jnp.where(bucket == k, t5_bias[k].reshape(hshape), out)
    return out


def _prompt_bias_tables(t5_bias, seq):
    nb = seq // CMP_BLOCK
    qpos = np.arange(seq)[:, None]
    d_c = qpos - (np.arange(nb) * CMP_BLOCK + CMP_BLOCK - 1)[None, :]
    bias_c = _bias_table(t5_bias, d_c, d_c >= 0)
    i = np.arange(Q_BLOCK)[:, None]
    jj = np.arange(Q_BLOCK)[None, :]
    d_s = np.stack([Q_BLOCK * dl + i - jj for dl in range(3)] + [i - jj - Q_BLOCK])
    bias_s = jnp.transpose(_bias_table(t5_bias, d_s, d_s >= 0), (1, 0, 2, 3))
    nwc = WINDOW // Q_BLOCK + 1
    d_w = np.stack([i + WINDOW - Q_BLOCK * cw - jj for cw in range(nwc)] + [i - jj - Q_BLOCK])
    bias_w = jnp.transpose(_bias_table(t5_bias, d_w, (d_w >= 0) & (d_w < WINDOW)), (1, 0, 2, 3))
    return bias_c, bias_s, bias_w


def _sample_bias_tables(t5_bias, past, win_len):
    nb_past = past // CMP_BLOCK
    head_g = (np.arange(NSA_HEADS) // NSA_GROUP)[:, None]

    def per_head(dist, valid):
        lane_cg = np.arange(dist.shape[-1]) % KV_ROWS
        tbl = _bias_table(t5_bias, dist, valid)
        own = jnp.asarray(lane_cg[None] == head_g)
        return tbl, own

    lane = np.arange(CMP_PAD * KV_ROWS)
    n = lane // KV_ROWS
    d_c = past - (n * CMP_BLOCK + CMP_BLOCK - 1)
    tbl, own = per_head(d_c, (d_c >= 0) & (n <= nb_past))
    bias_c = jnp.where(own, tbl, NEG)
    lane = np.arange(CMP_BLOCK * KV_ROWS)
    blk = np.arange(nb_past + 1)[:, None]
    d_s = past - (blk * CMP_BLOCK + (lane // KV_ROWS)[None, :])
    tbl, own = per_head(d_s, d_s >= 0)
    bias_s = jnp.transpose(jnp.where(own[:, None, :], tbl, NEG), (1, 0, 2))
    lane = np.arange(win_len * KV_ROWS)
    wpos = past - win_len + 1 + lane // KV_ROWS
    d_w = past - wpos
    tbl, own = per_head(d_w, (d_w >= 0) & (d_w < WINDOW) & (wpos >= 0))
    bias_w = jnp.where(own, tbl, NEG)
    return bias_c, bias_s, bias_w


def _compress_kernel(x_ref, w_ref, o_ref):
    r = x_ref.shape[0] // CMP_BLOCK
    x = x_ref[...].reshape(r, CMP_BLOCK, KVW)
    o_ref[...] = jnp.sum(x * w_ref[...][None], axis=1)


def _compress_prompt(proj, wtile, rows):
    rb = 512
    return pl.pallas_call(
        _compress_kernel,
        out_shape=jax.ShapeDtypeStruct((rows // CMP_BLOCK, KVW), F32),
        grid=(rows // rb,),
        in_specs=[pl.BlockSpec((rb, KVW), lambda i: (i, C_KVC // KVW)),
                  pl.BlockSpec((CMP_BLOCK, KVW), lambda i: (0, 0))],
        out_specs=pl.BlockSpec((rb // CMP_BLOCK, KVW), lambda i: (i, 0)),
        compiler_params=_params(("parallel",)),
        name="compress_prompt",
    )(proj, wtile)


KEY_CHUNK = 2 * Q_BLOCK
UNROLL = 2


def _nsa_prompt_kernel(q_ref, sm_ref, cmp_ref, kvs_ref, kvw_ref, bc_ref, bs_ref, bw_ref, e_ref, o_ref,
                       s_ref, mx_ref, l_ref, acc_ref):
    j = pl.program_id(1)
    nb = cmp_ref.shape[0]
    hg, qb, dh, kc = NSA_GROUP, Q_BLOCK, HEAD_DIM, KEY_CHUNK
    rows = hg * qb
    nwt = bw_ref.shape[0] - 1
    gates = _sigmoid(sm_ref[:, SM_GATE:SM_GATE + 3 * NSA_HEADS])
    qi = lax.broadcasted_iota(I32, (qb, nb), 0)
    ni = lax.broadcasted_iota(I32, (qb, nb), 1)
    cur = (qb // CMP_BLOCK) * j + qi // CMP_BLOCK
    forced = (ni == cur) | (ni == cur - 1) | (ni == 0)
    started = ni <= cur
    groups = range(NSA_KV_HEADS)
    kcols = [slice(g * dh, (g + 1) * dh) for g in groups]
    vcols = [slice((NSA_KV_HEADS + g) * dh, (NSA_KV_HEADS + g + 1) * dh) for g in groups]
    head_rows = [slice(g * hg, (g + 1) * hg) for g in groups]

    def fold(x, op):
        out = x[:, 0:qb]
        for t in range(1, x.shape[1] // qb):
            out = op(out, x[:, t * qb:(t + 1) * qb])
        return out

    q16, o_c, score = [], [], []
    for g in groups:
        q = jnp.concatenate([q_ref[:, (g * hg + h) * dh:(g * hg + h + 1) * dh] for h in range(hg)], axis=0)
        q16.append((q * (dh ** -0.5)).astype(BF16))
        bias = bc_ref[head_rows[g]].reshape(rows, nb)
        valid = bias > -1e29
        s = jnp.where(valid, _bdot(q16[g], cmp_ref[:, kcols[g]], _NT) + bias, NEG)
        p = jnp.where(valid, jnp.exp(s - jnp.max(s, axis=-1, keepdims=True)), 0.0)
        p = p / jnp.maximum(jnp.sum(p, axis=-1, keepdims=True), 1e-30)
        o_c.append(_bdot(p, cmp_ref[:, vcols[g]]))
        imp = p[0:qb]
        for h in range(1, hg):
            imp = imp + p[h * qb:(h + 1) * qb]
        score.append(jnp.where(started, jnp.where(forced, SEL_FORCE, imp), -1.0))

    unsel = [jnp.ones((qb, nb), F32) for _ in groups]
    for _ in range(min(N_SELECT, nb)):
        for g in groups:
            hit = ni == jnp.argmax(score[g], axis=-1, keepdims=True).astype(I32)
            unsel[g] = jnp.where(hit & started, 0.0, unsel[g])
            score[g] = jnp.where(hit, -3e38, score[g])
    unsel = [u.astype(BF16) for u in unsel]

    mx_ref[...] = jnp.full(mx_ref.shape, NEG, F32)
    l_ref[...] = jnp.zeros(l_ref.shape, F32)
    acc_ref[...] = jnp.zeros(acc_ref.shape, F32)
    n_chunks = j // (kc // qb) + 1

    def key_rows(c):
        return pl.ds(pl.multiple_of(c * kc, kc), kc)

    def sel_bias(g, c):
        tiles = []
        for t in range(kc // qb):
            back = j - (c * (kc // qb) + t)
            tiles.append(bs_ref[jnp.where(back < 0, 3, jnp.minimum(back, 2)), head_rows[g]].reshape(rows, qb))
        return jnp.concatenate(tiles, axis=1)

    def pass1_chunk(c):
        for g in groups:
            key_mask = jnp.dot(unsel[g], e_ref[c], preferred_element_type=F32)
            s_ = (_bdot(q16[g], kvs_ref[key_rows(c), kcols[g]], _NT) + sel_bias(g, c)
                  + jnp.concatenate([key_mask] * hg, axis=0))
            s_ref[g, c] = s_
            mx_ref[g] = jnp.maximum(mx_ref[g], fold(s_, jnp.maximum))

    def unrolled(chunk_fn):
        def body(t, carry):
            for u in range(UNROLL):
                chunk_fn(t * UNROLL + u)
            return carry
        lax.fori_loop(0, (n_chunks + UNROLL - 1) // UNROLL, body, 0)

    unrolled(pass1_chunk)
    for g in groups:
        mx_ref[g] = jnp.broadcast_to(jnp.max(mx_ref[g], axis=-1, keepdims=True), mx_ref.shape[1:])

    def pass2_chunk(c):
        for g in groups:
            p_ = jnp.exp(s_ref[g, c] - jnp.concatenate([mx_ref[g]] * (kc // qb), axis=1))
            l_ref[g] += fold(p_, jnp.add)
            acc_ref[g] += _bdot(p_, kvs_ref[key_rows(c), vcols[g]])

    unrolled(pass2_chunk)
    o_s = [acc_ref[g] / jnp.maximum(jnp.sum(l_ref[g], axis=-1, keepdims=True), 1e-30) for g in groups]

    lead = jnp.maximum(nwt - 1 - j, 0)
    w_rows = pl.ds(pl.multiple_of(jnp.maximum(j - (nwt - 1), 0) * qb, qb), nwt * qb)
    o_w = []
    for g in groups:
        bias = jnp.concatenate(
            [bw_ref[jnp.minimum(t + lead, nwt), head_rows[g]].reshape(rows, qb) for t in range(nwt)], axis=1)
        s = _bdot(q16[g], kvw_ref[w_rows, kcols[g]], _NT) + bias
        p = jnp.exp(s - jnp.max(fold(s, jnp.maximum), axis=-1, keepdims=True))
        den = jnp.sum(fold(p, jnp.add), axis=-1, keepdims=True)
        o_w.append(_bdot(p, kvw_ref[w_rows, vcols[g]]) / jnp.maximum(den, 1e-30))

    for g in groups:
        for h in range(hg):
            c0 = SM_GATE + (g * hg + h) * 3
            rs = slice(h * qb, (h + 1) * qb)
            o_ref[:, (g * hg + h) * dh:(g * hg + h + 1) * dh] = (
                o_c[g][rs] * gates[:, c0:c0 + 1] + o_s[g][rs] * gates[:, c0 + 1:c0 + 2]
                + o_w[g][rs] * gates[:, c0 + 2:c0 + 3])


def _nsa_prompt(proj, cmpkv, kvs16, kvw16, bias_c, bias_s, bias_w, expand, batch, seq):
    nqb = seq // Q_BLOCK
    nb = seq // CMP_BLOCK
    rows = NSA_GROUP * Q_BLOCK
    full = lambda shape: pl.BlockSpec(shape, lambda b, j: (0,) * len(shape))
    return pl.pallas_call(
        _nsa_prompt_kernel,
        out_shape=jax.ShapeDtypeStruct((proj.shape[0], MIX_WIDTH), F32),
        grid=(batch, nqb),
        in_specs=[pl.BlockSpec((Q_BLOCK, NSA_WIDTH), lambda b, j: (b * nqb + j, 0)),
                  pl.BlockSpec((Q_BLOCK, 128), lambda b, j: (b * nqb + j, C_SM // 128)),
                  pl.BlockSpec((nb, KVW), lambda b, j: (b, 0)),
                  pl.BlockSpec((seq, KVW), lambda b, j: (b, 0)),
                  pl.BlockSpec((seq, KVW), lambda b, j: (b, 0)),
                  pl.BlockSpec((NSA_HEADS, Q_BLOCK, nb), lambda b, j: (0, j, 0)),
                  full(bias_s.shape), full(bias_w.shape), full(expand.shape)],
        out_specs=pl.BlockSpec((Q_BLOCK, NSA_WIDTH), lambda b, j: (b * nqb + j, 0)),
        scratch_shapes=[pltpu.VMEM((NSA_KV_HEADS, seq // KEY_CHUNK, rows, KEY_CHUNK), F32),
                        pltpu.VMEM((NSA_KV_HEADS, rows, Q_BLOCK), F32), pltpu.VMEM((NSA_KV_HEADS, rows, Q_BLOCK), F32),
                        pltpu.VMEM((NSA_KV_HEADS, rows, HEAD_DIM), F32)],
        compiler_params=_params(("parallel", "arbitrary")),
        name="nsa_prompt",
    )(proj, proj, cmpkv, kvs16, kvw16, bias_c, bias_s, bias_w, expand)


ROWS_BLK = 512


def _cache_rows_kernel(per_b, kvc_ref, kvs_ref, kvw_ref, *rest):
    oc_ref, os_ref, ow_ref, s16_ref, w16_ref = rest[-5:]
    n = kvc_ref.shape[0]
    dh = HEAD_DIM

    def put(o_ref, x_ref):
        for cg in range(KV_ROWS):
            o_ref[pl.ds(cg, n, stride=KV_ROWS), :] = x_ref[:, cg * dh:(cg + 1) * dh]

    put(oc_ref, kvc_ref)
    put(os_ref, kvs_ref)
    s16_ref[...] = kvs_ref[...].astype(BF16)
    w16_ref[...] = kvw_ref[...].astype(BF16)

    @pl.when(pl.program_id(0) % per_b == per_b - 1)
    def _():
        put(ow_ref, kvw_ref)


def _cache_rows(proj, batch, seq, depth, layer, prev):
    assert seq % ROWS_BLK == 0 and WINDOW == ROWS_BLK
    per_b = seq // ROWS_BLK
    nblk = batch * per_b
    blk = lambda c: pl.BlockSpec((ROWS_BLK, KVW), lambda i: (i, c // KVW))
    out_rows = ROWS_BLK * KV_ROWS
    return pl.pallas_call(
        functools.partial(_cache_rows_kernel, per_b),
        out_shape=(jax.ShapeDtypeStruct((depth * batch * seq * KV_ROWS, HEAD_DIM), F32),
                   jax.ShapeDtypeStruct((depth * batch * seq * KV_ROWS, HEAD_DIM), F32),
                   jax.ShapeDtypeStruct((depth * batch * WINDOW * KV_ROWS, HEAD_DIM), F32),
                   jax.ShapeDtypeStruct((batch * seq, KVW), BF16),
                   jax.ShapeDtypeStruct((batch * seq, KVW), BF16)),
        grid=(nblk,),
        in_specs=[blk(C_KVC), blk(C_KVS), blk(C_KVW)]
        + [pl.BlockSpec(memory_space=pl.ANY)] * len(prev),
        out_specs=(pl.BlockSpec((out_rows, HEAD_DIM), lambda i: (layer * nblk + i, 0)),
                   pl.BlockSpec((out_rows, HEAD_DIM), lambda i: (layer * nblk + i, 0)),
                   pl.BlockSpec((out_rows, HEAD_DIM), lambda i: (layer * batch + i // per_b, 0)),
                   pl.BlockSpec((ROWS_BLK, KVW), lambda i: (i, 0)),
                   pl.BlockSpec((ROWS_BLK, KVW), lambda i: (i, 0))),
        input_output_aliases={3 + a: a for a in range(len(prev))},
        compiler_params=_params(("arbitrary",)),
        name="cache_rows",
    )(proj, proj, proj, *prev)


_BNN = (((2,), (1,)), ((0,), (0,)))
_BNT = (((2,), (2,)), ((0,), (0,)))
_BTN = (((1,), (1,)), ((0,), (0,)))


def _bdot_b(a, b, dims=_BNN):
    return lax.dot_general(a.astype(BF16), b.astype(BF16), dims, preferred_element_type=F32)


def _tri_masks():
    r = lax.broadcasted_iota(I32, (CHUNK, CHUNK), 0)
    c = lax.broadcasted_iota(I32, (CHUNK, CHUNK), 1)
    return r, c


def _gla_prompt_kernel(q_ref, k_ref, v_ref, gg_ref, sm_ref, w2_ref, b2_ref, nw_ref, _y_ref, o_ref, so_ref, s_ref):
    t = pl.program_id(1)

    @pl.when(t == 0)
    def _():
        s_ref[...] = jnp.zeros_like(s_ref)

    r, c = _tri_masks()
    lower = r >= c
    tril = jnp.where(lower, 1.0, 0.0).astype(F32)
    dk, dv = GLA_DK, GLA_DV
    nch = q_ref.shape[0] // CHUNK
    pairs = [(ch, h) for ch in range(nch) for h in range(GLA_HEADS)]
    rs = lambda ch: slice(ch * CHUNK, (ch + 1) * CHUNK)
    stack = lambda fn: jnp.stack([fn(ch, h) for ch, h in pairs], axis=0)

    log_a = _log_sigmoid(_bdot(sm_ref[:, SM_GLR:SM_GLR + GLA_GATE_RANK], w2_ref[...]) + b2_ref[...]) / GLA_GATE_NORM
    width = GLA_HEADS * dk
    b_all = _hdot(tril, jnp.concatenate([log_a[rs(ch)] for ch in range(nch)], axis=1))
    b_all_t = b_all.T
    b = stack(lambda ch, h: b_all[:, ch * width + h * dk:ch * width + (h + 1) * dk])
    b_last = b[:, CHUNK - 1:CHUNK, :]
    b_last_col = stack(lambda ch, h: b_all_t[ch * width + h * dk:ch * width + (h + 1) * dk, CHUNK - 1:CHUNK])
    q = stack(lambda ch, h: q_ref[rs(ch), h * dk:(h + 1) * dk]) * (dk ** -0.5)
    k = stack(lambda ch, h: k_ref[rs(ch), h * dk:(h + 1) * dk])
    v = stack(lambda ch, h: v_ref[rs(ch), h * dv:(h + 1) * dv])
    qe = q * jnp.exp(b)
    att = jnp.where(lower[None], _bdot_b(qe, k * jnp.exp(-b), _BNT), 0.0)
    o_intra = _bdot_b(att, v)
    kv = _bdot_b(k * jnp.exp(b_last - b), v, _BTN)
    s_decay = jnp.exp(b_last_col)

    for p, (ch, h) in enumerate(pairs):
        s = s_ref[h]
        o = _bdot(qe[p], s) + o_intra[p]
        s_ref[h] = s_decay[p] * s + kv[p]
        o_ref[rs(ch), h * dv:(h + 1) * dv] = _rms(o, nw_ref[...]) * _silu(gg_ref[rs(ch), h * dv:(h + 1) * dv])
    so_ref[0] = s_ref[...]


def _gla_prompt(proj, w2, b2, nw, batch, seq, y):
    nt = seq // TB
    row = lambda w, col: pl.BlockSpec((TB, w), lambda b, t: (b * nt + t, col // w))
    full = lambda shape: pl.BlockSpec(shape, lambda b, t: (0,) * len(shape))
    return pl.pallas_call(
        _gla_prompt_kernel,
        out_shape=(jax.ShapeDtypeStruct(y.shape, F32),
                   jax.ShapeDtypeStruct((batch, GLA_HEADS, GLA_DK, GLA_DV), F32)),
        grid=(batch, nt),
        in_specs=[row(GLA_HEADS * GLA_DK, C_GQ), row(GLA_HEADS * GLA_DK, C_GK), row(GLA_WIDTH, C_GV),
                  row(GLA_WIDTH, C_GG), row(128, C_SM), full(w2.shape), full(b2.shape), full(nw.shape),
                  pl.BlockSpec(memory_space=pl.ANY)],
        out_specs=(pl.BlockSpec((TB, GLA_WIDTH), lambda b, t: (b * nt + t, NSA_WIDTH // GLA_WIDTH)),
                   pl.BlockSpec((1, GLA_HEADS, GLA_DK, GLA_DV), lambda b, t: (b, 0, 0, 0))),
        scratch_shapes=[pltpu.VMEM((GLA_HEADS, GLA_DK, GLA_DV), F32)],
        input_output_aliases={8: 0},
        compiler_params=_params(("parallel", "arbitrary")),
        name="gla_prompt",
    )(proj, proj, proj, proj, proj, w2, b2, nw, y)


def _unit_lower_inverse(m, r, c):
    eye = jnp.where(r == c, 1.0, 0.0).astype(F32)[None]
    base = 8
    m8 = jnp.where(((r // base) == (c // base))[None], m, 0.0)
    m2 = _bdot_b(m8, m8)
    m4 = _bdot_b(m2, m2)
    t = _bdot_b(_bdot_b(eye - m8, eye + m2), eye + m4)
    s = base
    while s < CHUNK:
        off = ((r // (2 * s)) == (c // (2 * s))) & ((r // s) != (c // s))
        t = t - _bdot_b(t, _bdot_b(jnp.where(off[None], m, 0.0), t))
        s *= 2
    return t


def _gdn_prompt_kernel(x_ref, dz_ref, sm_ref, cw_ref, al_ref, dt_ref, nw_ref, _y_ref, o_ref, so_ref, s_ref,
                       tail_ref):
    t = pl.program_id(1)

    @pl.when(t == 0)
    def _():
        s_ref[...] = jnp.zeros_like(s_ref)
        tail_ref[...] = jnp.zeros_like(tail_ref)

    tb = x_ref.shape[0]
    x = x_ref[...]
    xc = jnp.concatenate([tail_ref[...], x], axis=0)
    off = 8 - (CONV_W - 1)
    y = xc[off:off + tb] * cw_ref[0:1, :]
    for jw in range(1, CONV_W):
        y = y + xc[off + jw:off + jw + tb] * cw_ref[jw:jw + 1, :]
    y = _silu(y)
    tail_ref[...] = x[tb - 8:tb]

    beta_all = _sigmoid(sm_ref[:, SM_DB:SM_DB + GDN_HEADS])
    g_all = -jnp.exp(al_ref[...]) * _softplus(sm_ref[:, SM_DA:SM_DA + GDN_HEADS] + dt_ref[...])
    r, c = _tri_masks()
    lower = r >= c
    strict = r > c
    tril = jnp.where(lower, 1.0, 0.0).astype(F32)
    dk, dv = GDN_DK, GDN_DV
    nch = tb // CHUNK
    pairs = [(ch, h) for ch in range(nch) for h in range(GDN_HEADS)]
    rs = lambda ch: slice(ch * CHUNK, (ch + 1) * CHUNK)
    stack = lambda fn: jnp.stack([fn(ch, h) for ch, h in pairs], axis=0)

    cq = stack(lambda ch, h: y[rs(ch), h * dk:(h + 1) * dk])
    ck = stack(lambda ch, h: y[rs(ch), GDN_WIDTH + h * dk:GDN_WIDTH + (h + 1) * dk])
    v = stack(lambda ch, h: y[rs(ch), 2 * GDN_WIDTH + h * dv:2 * GDN_WIDTH + (h + 1) * dv])
    q = cq * lax.rsqrt(jnp.sum(cq * cq, axis=-1, keepdims=True) + 1e-6) * (dk ** -0.5)
    k = ck * lax.rsqrt(jnp.sum(ck * ck, axis=-1, keepdims=True) + 1e-6)
    beta = stack(lambda ch, h: jnp.broadcast_to(beta_all[rs(ch), h:h + 1], (CHUNK, dk)))
    g_cols = jnp.concatenate([g_all[rs(ch)] for ch in range(nch)], axis=1)
    gam_cols = _hdot(tril, g_cols)
    gam_rows = gam_cols.T
    gam = jnp.stack([jnp.broadcast_to(gam_cols[:, p:p + 1], (CHUNK, dk)) for p in range(len(pairs))], axis=0)
    decay = jnp.exp(jnp.where(lower[None], gam[:, :, 0:CHUNK] - gam_rows[:, None, :], NEG))
    kb = k * beta
    m = jnp.where(strict[None], _bdot_b(kb, k, _BNT) * decay, 0.0)
    tinv = _unit_lower_inverse(m, r, c)
    eg = jnp.exp(gam)
    u = _bdot_b(tinv, v * beta)
    w = _bdot_b(tinv, kb * eg)
    att = _bdot_b(q, k, _BNT) * decay
    qe = q * eg
    g_last = gam[:, CHUNK - 1:CHUNK, :]
    kd = k * jnp.exp(g_last - gam)
    eg_last = jnp.exp(g_last)

    for p, (ch, h) in enumerate(pairs):
        s = s_ref[h]
        v_new = u[p] - _bdot(w[p], s)
        o = _bdot(qe[p], s) + _bdot(att[p], v_new)
        s_ref[h] = eg_last[p] * s + _bdot(kd[p], v_new, _TN)
        o_ref[rs(ch), h * dv:(h + 1) * dv] = _rms(o, nw_ref[...]) * _silu(dz_ref[rs(ch), h * dv:(h + 1) * dv])
    so_ref[0] = s_ref[...]


def _gdn_prompt(proj, cw, a_log, dt_bias, nw, batch, seq, y):
    nt = seq // TB
    row = lambda w, col: pl.BlockSpec((TB, w), lambda b, t: (b * nt + t, col // w))
    full = lambda shape: pl.BlockSpec(shape, lambda b, t: (0,) * len(shape))
    return pl.pallas_call(
        _gdn_prompt_kernel,
        out_shape=(jax.ShapeDtypeStruct(y.shape, F32),
                   jax.ShapeDtypeStruct((batch, GDN_HEADS, GDN_DK, GDN_DV), F32)),
        grid=(batch, nt),
        in_specs=[row(3 * GDN_WIDTH, C_DQKV), row(GDN_WIDTH, C_DZ), row(128, C_SM),
                  full(cw.shape), full(a_log.shape), full(dt_bias.shape), full(nw.shape),
                  pl.BlockSpec(memory_space=pl.ANY)],
        out_specs=(pl.BlockSpec((TB, GDN_WIDTH), lambda b, t: (b * nt + t, (NSA_WIDTH + GLA_WIDTH) // GDN_WIDTH)),
                   pl.BlockSpec((1, GDN_HEADS, GDN_DK, GDN_DV), lambda b, t: (b, 0, 0, 0))),
        scratch_shapes=[pltpu.VMEM((GDN_HEADS, GDN_DK, GDN_DV), F32), pltpu.VMEM((8, 3 * GDN_WIDTH), F32)],
        input_output_aliases={7: 0},
        compiler_params=_params(("parallel", "arbitrary")),
        name="gdn_prompt",
    )(proj, proj, proj, cw, a_log, dt_bias, nw, y)


def _masked_softmax_rows(s):
    valid = s > -1e29
    p = jnp.where(valid, jnp.exp(s - jnp.max(s, axis=-1, keepdims=True)), 0.0)
    return p / jnp.maximum(jnp.sum(p, axis=-1, keepdims=True), 1e-30)


def _head_rows(q_ref, sb, lo, hi):
    dh = HEAD_DIM
    q = jnp.concatenate([q_ref[sb, :, h * dh:(h + 1) * dh] for h in range(lo, hi)], axis=0)
    return (q * (dh ** -0.5)).astype(BF16)


def _kv_rows(kv_ref, sb):
    dh = HEAD_DIM
    return jnp.concatenate([kv_ref[sb, :, r * dh:(r + 1) * dh] for r in range(KV_ROWS)], axis=0)


def _value_weights(p):
    return pltpu.roll(p, NSA_KV_HEADS, 1)


SN = 2


def _nsa_sample_a_kernel(n_pages, nb_past, pt_ref, q_ref, kvc_ref, kvw_ref, sm_ref, *rest):
    page_refs = rest[:SN * n_pages]
    win_ref, w4_ref, bc_ref, bw_ref = rest[SN * n_pages:SN * n_pages + 4]
    part_ref, idx_ref, wo_ref, cmp_ref = rest[-4:]
    dh = HEAD_DIM
    blk_rows = CMP_BLOCK * KV_ROWS
    bpp = PAGE_SIZE // CMP_BLOCK
    wl = win_ref.shape[0] // SN
    w4 = w4_ref[...]
    lanes = cmp_ref.shape[1]
    lane = lax.broadcasted_iota(I32, (NSA_KV_HEADS, lanes), 1)
    gi = lax.broadcasted_iota(I32, (NSA_KV_HEADS, lanes), 0)
    n = lane // KV_ROWS
    cur = nb_past
    cand = ((lane % KV_ROWS) == gi) & (n <= cur)
    forced = (n == cur) | (n == cur - 1) | (n == 0)
    li = lax.broadcasted_iota(I32, (NSA_KV_HEADS, N_SELECT), 1)

    for sb in range(SN):
        for p in range(n_pages):
            x = page_refs[sb * n_pages + p][...]
            sums = []
            for half in range(bpp):
                pr = x[half * blk_rows:(half + 1) * blk_rows] * w4
                s8 = jnp.sum(pr.reshape(blk_rows // 8, 8, dh), axis=0)
                sums.append(s8[0:KV_ROWS] + s8[KV_ROWS:2 * KV_ROWS])
            cmp_ref[sb, p * bpp * KV_ROWS:(p + 1) * bpp * KV_ROWS, :] = jnp.concatenate(sums, axis=0)
        r0 = nb_past * KV_ROWS
        cmp_ref[sb, r0:r0 + KV_ROWS, :] = _kv_rows(kvc_ref, sb) * w4[0:KV_ROWS]
        cmp_ref[sb, r0 + KV_ROWS:, :] = jnp.zeros((lanes - r0 - KV_ROWS, dh), F32)
        cm16 = cmp_ref[sb].astype(BF16)

        q16 = _head_rows(q_ref, sb, 0, NSA_HEADS)
        p = _masked_softmax_rows(_bdot(q16, cm16, _NT) + bc_ref[...])
        o_c = _bdot(_value_weights(p), cm16)

        imp = jnp.concatenate([jnp.sum(p[g * NSA_GROUP:(g + 1) * NSA_GROUP], axis=0, keepdims=True)
                               for g in range(NSA_KV_HEADS)], axis=0)
        score = jnp.where(cand, jnp.where(forced, SEL_FORCE, imp), -3e38)
        top = jnp.zeros((NSA_KV_HEADS, N_SELECT), I32)
        for r in range(N_SELECT):
            a = jnp.argmax(score, axis=-1, keepdims=True).astype(I32)
            top = jnp.where(li == r, a // KV_ROWS, top)
            score = jnp.where(lane == a, -3e38, score)
        idx_ref[sb] = top

        ws = slice(sb * wl, (sb + 1) * wl)
        wo_ref[ws, :] = pltpu.roll(win_ref[ws, :], wl - KV_ROWS, 0)
        wo_ref[(sb + 1) * wl - KV_ROWS:(sb + 1) * wl, :] = _kv_rows(kvw_ref, sb)
        w16 = wo_ref[ws, :].astype(BF16)
        pw = _masked_softmax_rows(_bdot(q16, w16, _NT) + bw_ref[...])
        o_w = _bdot(_value_weights(pw), w16)

        gates = _sigmoid(sm_ref[sb, :, SM_GATE:SM_GATE + 3 * NSA_HEADS])
        for h in range(NSA_HEADS):
            c0 = 3 * h
            part_ref[sb, :, h * dh:(h + 1) * dh] = (
                o_c[h:h + 1] * gates[:, c0:c0 + 1] + o_w[h:h + 1] * gates[:, c0 + 2:c0 + 3])


def _nsa_sample_a(page_table, proj_s3, pool_cmp, win_rows, w4, bias_c, bias_w, layer, n_phys, prev_win):
    nbatch, n_pages = page_table.shape
    nb_past = n_pages * PAGE_SIZE // CMP_BLOCK
    wl = bias_w.shape[1]
    page_rows = PAGE_SIZE * KV_ROWS
    nsteps = nbatch // SN
    col = lambda w, c: pl.BlockSpec((SN, 1, w), lambda i, pt: (i, 0, c // w))
    full = lambda shape: pl.BlockSpec(shape, lambda i, pt: (0,) * len(shape))
    page = lambda sb, p: pl.BlockSpec((page_rows, HEAD_DIM), lambda i, pt: (layer * n_phys + pt[SN * i + sb, p], 0))
    gs = pltpu.PrefetchScalarGridSpec(
        num_scalar_prefetch=1, grid=(nsteps,),
        in_specs=[col(NSA_WIDTH, C_NQ), col(KVW, C_KVC), col(KVW, C_KVW), col(128, C_SM)]
        + [page(sb, p) for sb in range(SN) for p in range(n_pages)]
        + [pl.BlockSpec((SN * wl, HEAD_DIM), lambda i, pt: (layer * nsteps + i, 0)),
           full(w4.shape), full(bias_c.shape), full(bias_w.shape)]
        + [pl.BlockSpec(memory_space=pl.ANY)] * len(prev_win),
        out_specs=(pl.BlockSpec((SN, 1, NSA_WIDTH), lambda i, pt: (i, 0, 0)),
                   pl.BlockSpec((SN, NSA_KV_HEADS, N_SELECT), lambda i, pt: (i, 0, 0)),
                   pl.BlockSpec((SN * wl, HEAD_DIM), lambda i, pt: (layer * nsteps + i, 0))),
        scratch_shapes=[pltpu.VMEM((SN, CMP_PAD * KV_ROWS, HEAD_DIM), F32)])
    return pl.pallas_call(
        functools.partial(_nsa_sample_a_kernel, n_pages, nb_past),
        out_shape=(jax.ShapeDtypeStruct((nbatch, 1, NSA_WIDTH), F32),
                   jax.ShapeDtypeStruct((nbatch, NSA_KV_HEADS, N_SELECT), I32),
                   jax.ShapeDtypeStruct(win_rows.shape, F32)),
        grid_spec=gs,
        input_output_aliases={1 + 4 + SN * n_pages + 4 + a: 2 for a in range(len(prev_win))},
        compiler_params=_params(("arbitrary",)),
        name="nsa_sample_a",
    )(page_table, proj_s3, proj_s3, proj_s3, proj_s3, *([pool_cmp] * (SN * n_pages)), win_rows, w4, bias_c, bias_w,
      *prev_win)


def _nsa_sample_b_kernel(nb_past, pt_ref, ix_ref, q_ref, kvs_ref, sm_ref, part_ref, *rest):
    nsel = NSA_KV_HEADS * N_SELECT
    blk_refs = rest[:SN * nsel]
    bs_ref, o_ref = rest[SN * nsel:]
    i0 = pl.program_id(0) * SN
    hg, dh = NSA_GROUP, HEAD_DIM
    blk_rows = CMP_BLOCK * KV_ROWS
    q16 = jnp.concatenate([_head_rows(q_ref, sb, 0, NSA_HEADS) for sb in range(SN)], axis=0)
    masked = jnp.full((hg, blk_rows), NEG, F32)
    keys, bias = [], []
    for sb in range(SN):
        new_blk = jnp.concatenate([_kv_rows(kvs_ref, sb), jnp.zeros((blk_rows - KV_ROWS, dh), F32)], axis=0)
        for g in range(NSA_KV_HEADS):
            owner = sb * NSA_KV_HEADS + g
            for i in range(N_SELECT):
                idx = ix_ref[i0 + sb, g * N_SELECT + i]
                past_blk = blk_refs[sb * nsel + g * N_SELECT + i][...]
                keys.append(jnp.where(idx >= nb_past, new_blk, past_blk).astype(BF16))
                bt = bs_ref[jnp.clip(idx, 0, nb_past), g * hg:(g + 1) * hg, :]
                bias.append(jnp.concatenate([bt if rg == owner else masked
                                             for rg in range(SN * NSA_KV_HEADS)], axis=0))
    k_all = jnp.concatenate(keys, axis=0)
    p = _masked_softmax_rows(_bdot(q16, k_all, _NT) + jnp.concatenate(bias, axis=1))
    o_s = _bdot(_value_weights(p), k_all)
    for sb in range(SN):
        gates = _sigmoid(sm_ref[sb, :, SM_GATE:SM_GATE + 3 * NSA_HEADS])
        for h in range(NSA_HEADS):
            cs = slice(h * dh, (h + 1) * dh)
            r = sb * NSA_HEADS + h
            o_ref[sb, :, cs] = part_ref[sb, :, cs] + o_s[r:r + 1] * gates[:, 3 * h + 1:3 * h + 2]


def _nsa_sample_b(page_table, top_idx, proj_s3, part, pool_sel, bias_s, layer, n_phys):
    nbatch, n_pages = page_table.shape
    nb_past = n_pages * PAGE_SIZE // CMP_BLOCK
    bpp = PAGE_SIZE // CMP_BLOCK
    nsel = NSA_KV_HEADS * N_SELECT
    col = lambda w, c: pl.BlockSpec((SN, 1, w), lambda i, pt, ix: (i, 0, c // w))

    def blk(sb, gi):
        def imap(i, pt, ix):
            b = SN * i + sb
            ip = jnp.clip(ix[b, gi], 0, nb_past - 1)
            return ((layer * n_phys + pt[b, ip // bpp]) * bpp + ip % bpp, 0)
        return pl.BlockSpec((CMP_BLOCK * KV_ROWS, HEAD_DIM), imap)

    gs = pltpu.PrefetchScalarGridSpec(
        num_scalar_prefetch=2, grid=(nbatch // SN,),
        in_specs=[col(NSA_WIDTH, C_NQ), col(KVW, C_KVS), col(128, C_SM),
                  pl.BlockSpec((SN, 1, NSA_WIDTH), lambda i, pt, ix: (i, 0, 0))]
        + [blk(sb, gi) for sb in range(SN) for gi in range(nsel)]
        + [pl.BlockSpec(bias_s.shape, lambda i, pt, ix: (0, 0, 0))],
        out_specs=pl.BlockSpec((SN, 1, NSA_WIDTH), lambda i, pt, ix: (i, 0, 0)))
    return pl.pallas_call(
        functools.partial(_nsa_sample_b_kernel, nb_past),
        out_shape=jax.ShapeDtypeStruct((nbatch, 1, NSA_WIDTH), F32),
        grid_spec=gs,
        compiler_params=_params(("arbitrary",)),
        name="nsa_sample_b",
    )(page_table, top_idx, proj_s3, proj_s3, proj_s3, part, *([pool_sel] * (SN * nsel)), bias_s)


SB = 8


def _rec_sample_kernel(gq_ref, gk_ref, gv_ref, gg_ref, x_ref, dz_ref, sm_ref, sg_ref, sd_ref, cb_ref,
                       w2_ref, b2_ref, gnw_ref, cw_ref, al_ref, dt_ref, dnw_ref, *rest):
    o_ref, sgo_ref, sdo_ref = rest[-3:]
    sm = sm_ref[...]
    ri = lax.broadcasted_iota(I32, (SB, 128), 0)
    log_a = _log_sigmoid(_bdot(sm[:, SM_GLR:SM_GLR + GLA_GATE_RANK], w2_ref[...]) + b2_ref[...]) / GLA_GATE_NORM
    ea_t = jnp.exp(log_a).T
    k_t = gk_ref[...].T
    q_t = (gq_ref[...] * (GLA_DK ** -0.5)).T
    gv = gv_ref[...]
    for h in range(GLA_HEADS):
        hs = slice(h * GLA_DK, (h + 1) * GLA_DK)
        vs = slice(h * GLA_DV, (h + 1) * GLA_DV)
        o_h = jnp.zeros((SB, GLA_DV), F32)
        for i in range(SB):
            s = ea_t[hs, i:i + 1] * sg_ref[i, h] + k_t[hs, i:i + 1] * gv[i:i + 1, vs]
            sgo_ref[i, h] = s
            o = jnp.sum(q_t[hs, i:i + 1] * s, axis=0, keepdims=True)
            o_h = jnp.where(ri == i, o, o_h)
        o_ref[:, vs] = _rms(o_h, gnw_ref[...]) * _silu(gg_ref[:, vs])
    y = x_ref[...] * cw_ref[CONV_W - 1:CONV_W, :]
    for jw in range(CONV_W - 1):
        y = y + cb_ref[jw] * cw_ref[jw:jw + 1, :]
    y = _silu(y)
    beta = _sigmoid(sm[:, SM_DB:SM_DB + GDN_HEADS])
    eg = jnp.exp(-jnp.exp(al_ref[...]) * _softplus(sm[:, SM_DA:SM_DA + GDN_HEADS] + dt_ref[...]))
    dk, dv = GDN_DK, GDN_DV
    for h in range(GDN_HEADS):
        cq = y[:, h * dk:(h + 1) * dk]
        ck = y[:, GDN_WIDTH + h * dk:GDN_WIDTH + (h + 1) * dk]
        v = y[:, 2 * GDN_WIDTH + h * dv:2 * GDN_WIDTH + (h + 1) * dv]
        q_t = (cq * lax.rsqrt(jnp.sum(cq * cq, axis=-1, keepdims=True) + 1e-6) * (dk ** -0.5)).T
        k_t = (ck * lax.rsqrt(jnp.sum(ck * ck, axis=-1, keepdims=True) + 1e-6)).T
        o_h = jnp.zeros((SB, dv), F32)
        for i in range(SB):
            s = eg[i:i + 1, h:h + 1] * sd_ref[i, h]
            kc = k_t[:, i:i + 1]
            delta = (v[i:i + 1] - jnp.sum(kc * s, axis=0, keepdims=True)) * beta[i:i + 1, h:h + 1]
            s = s + kc * delta
            sdo_ref[i, h] = s
            o = jnp.sum(q_t[:, i:i + 1] * s, axis=0, keepdims=True)
            o_h = jnp.where(ri == i, o, o_h)
        vs = slice(GLA_WIDTH + h * dv, GLA_WIDTH + (h + 1) * dv)
        o_ref[:, vs] = _rms(o_h, dnw_ref[...]) * _silu(dz_ref[:, h * dv:(h + 1) * dv])


def _rec_sample(proj_s, state_gla, state_gdn, conv_t, w2, b2, gnw, cw, a_log, dt_bias, dnw, layer, prev_states):
    nbatch = proj_s.shape[0]
    nblk = nbatch // SB
    depth = state_gla.shape[0] // nbatch
    alias_specs = [pl.BlockSpec(memory_space=pl.ANY)] * len(prev_states)
    n_in = 17
    row = lambda w, col: pl.BlockSpec((SB, w), lambda i: (i, col // w))
    full = lambda shape: pl.BlockSpec(shape, lambda i: (0,) * len(shape))
    return pl.pallas_call(
        _rec_sample_kernel,
        out_shape=(jax.ShapeDtypeStruct((nbatch, GLA_WIDTH + GDN_WIDTH), F32),
                   jax.ShapeDtypeStruct(state_gla.shape, F32),
                   jax.ShapeDtypeStruct(state_gdn.shape, F32)),
        grid=(nblk,),
        in_specs=[row(GLA_HEADS * GLA_DK, C_GQ), row(GLA_HEADS * GLA_DK, C_GK), row(GLA_WIDTH, C_GV),
                  row(GLA_WIDTH, C_GG), row(3 * GDN_WIDTH, C_DQKV), row(GDN_WIDTH, C_DZ), row(128, C_SM),
                  pl.BlockSpec((SB, GLA_HEADS, GLA_DK, GLA_DV), lambda i: (layer * nblk + i, 0, 0, 0)),
                  pl.BlockSpec((SB, GDN_HEADS, GDN_DK, GDN_DV), lambda i: (layer * nblk + i, 0, 0, 0)),
                  pl.BlockSpec((CONV_W - 1, SB, 3 * GDN_WIDTH), lambda i: (0, i, 0)),
                  full(w2.shape), full(b2.shape), full(gnw.shape), full(cw.shape), full(a_log.shape),
                  full(dt_bias.shape), full(dnw.shape)] + alias_specs,
        out_specs=(pl.BlockSpec((SB, GLA_WIDTH + GDN_WIDTH), lambda i: (i, 0)),
                   pl.BlockSpec((SB, GLA_HEADS, GLA_DK, GLA_DV), lambda i: (layer * nblk + i, 0, 0, 0)),
                   pl.BlockSpec((SB, GDN_HEADS, GDN_DK, GDN_DV), lambda i: (layer * nblk + i, 0, 0, 0))),
        input_output_aliases={n_in + a: 1 + a for a in range(len(prev_states))},
        compiler_params=_params(("parallel",)),
        name="rec_sample",
    )(proj_s, proj_s, proj_s, proj_s, proj_s, proj_s, proj_s, state_gla, state_gdn, conv_t,
      w2, b2, gnw, cw, a_log, dt_bias, dnw, *prev_states)


def _reorder_w_in(w):
    return jnp.concatenate([
        w[..., 0:2560],
        w[..., 2584:3096],
        w[..., 4136:5672],
        w[..., 3096:3608],
        w[..., 3624:4136],
        w[..., 5672:6184],
        w[..., 2560:2584],
        w[..., 3608:3624],
        w[..., 6184:6192],
        jnp.zeros(w.shape[:-1] + (PROJ_N - 6192,), w.dtype)], axis=-1)


def _cmp_weight_tile(w_cmp):
    half = NSA_KV_HEADS * HEAD_DIM
    return jnp.concatenate([jnp.broadcast_to(w_cmp[:, 0:1], (CMP_BLOCK, half)),
                            jnp.broadcast_to(w_cmp[:, 1:2], (CMP_BLOCK, half))], axis=1).astype(F32)


def _cmp_weight_rows(w_cmp):
    w = jnp.repeat(w_cmp, NSA_KV_HEADS, axis=1).reshape(CMP_BLOCK * KV_ROWS, 1)
    return jnp.broadcast_to(w, (CMP_BLOCK * KV_ROWS, HEAD_DIM)).astype(F32)


def kernel(x_prompt, x_sample, cache_cmp, cache_sel, cache_win, state_gla, state_gdn, state_conv, page_table,
           p_prompt, p_sample, norm_w, ffn_w1, ffn_w3, ffn_w2, w_in, w_out, nsa_w_cmp, t5_bias,
           gla_w_gk2, gla_b_gk, gla_norm_w, gdn_conv_w, gdn_a_log, gdn_dt_bias, gdn_norm_w,
           ple_w_proj, ple_w_gate):
    depth = w_in.shape[0]
    batch, seq, d = x_prompt.shape
    nbatch = x_sample.shape[0]
    n_phys = cache_cmp.shape[1]
    n_pages = page_table.shape[1]
    past = n_pages * PAGE_SIZE
    win_len = cache_win.shape[2]
    rows_p = batch * seq
    kv_shape = (2, NSA_KV_HEADS, HEAD_DIM)

    x = jnp.concatenate([x_prompt.reshape(rows_p, d), x_sample.reshape(nbatch, d)], axis=0)
    assert win_len == WINDOW and past // CMP_BLOCK + 1 >= N_SELECT and past // CMP_BLOCK < CMP_PAD
    assert seq >= WINDOW
    pool_cmp = cache_cmp.reshape(-1, HEAD_DIM)
    pool_sel = cache_sel.reshape(-1, HEAD_DIM)
    win_all = cache_win.reshape(-1, HEAD_DIM)
    sgla_all = state_gla.reshape((depth * nbatch,) + state_gla.shape[2:])
    sgdn_all = state_gdn.reshape((depth * nbatch,) + state_gdn.shape[2:])

    pb_c, pb_s, pb_w = _prompt_bias_tables(t5_bias, seq)
    sb_c, sb_s, sb_w = _sample_bias_tables(t5_bias, past, win_len)
    nb = seq // CMP_BLOCK
    expand = jnp.asarray(-MASK_BIG * np.repeat(np.eye(nb, dtype=np.float32), CMP_BLOCK, axis=1)
                         .reshape(nb, seq // KEY_CHUNK, KEY_CHUNK).transpose(1, 0, 2), BF16)

    w1_all, w3_all, w2_all = ffn_w1.astype(BF16), ffn_w3.astype(BF16), ffn_w2.astype(BF16)
    w_in_all = _reorder_w_in(w_in.astype(BF16))
    w_out_all, wg_all, wp_all = w_out.astype(BF16), ple_w_gate.astype(BF16), ple_w_proj.astype(BF16)

    outs = {k: [] for k in ("cmp_s", "sel_s", "gla_p", "gdn_p", "conv_p", "conv_s")}
    prev_win, prev_states, prev_rows = (), (), ()
    for l in range(depth):
        nw = norm_w[l].reshape(-1, 1, d)
        x = _ffn(x, nw[0], w1_all, w3_all, w2_all, nw[1], l, 0)

        proj = _proj(x, nw[2], w_in_all, l)
        proj_s = proj[rows_p:]
        proj_s3 = proj_s.reshape(nbatch, 1, PROJ_N)
        wtile = _cmp_weight_tile(nsa_w_cmp[l])
        w_gk2 = gla_w_gk2[l]
        b_gk = gla_b_gk[l].reshape(1, -1)
        gla_nw = gla_norm_w[l].reshape(1, -1)
        gdn_nw = gdn_norm_w[l].reshape(1, -1)
        conv_w = gdn_conv_w[l]
        a_log = gdn_a_log[l].reshape(1, -1)
        dt_bias = gdn_dt_bias[l].reshape(1, -1)

        cmpkv = _compress_prompt(proj, wtile, rows_p)
        rows_out = _cache_rows(proj, batch, seq, depth, l, prev_rows)
        prev_rows, (kvs16, kvw16) = rows_out[:3], rows_out[3:]
        y = _nsa_prompt(proj, cmpkv, kvs16, kvw16, pb_c, pb_s, pb_w, expand, batch, seq)
        y, s_gla_p = _gla_prompt(proj, w_gk2, b_gk, gla_nw, batch, seq, y)
        y, s_gdn_p = _gdn_prompt(proj, conv_w, a_log, dt_bias, gdn_nw, batch, seq, y)

        part, top_idx, win_new = _nsa_sample_a(page_table, proj_s3, pool_cmp, win_all, _cmp_weight_rows(nsa_w_cmp[l]),
                                               sb_c, sb_w, l, n_phys, prev_win)
        prev_win = (win_new,)
        o_nsa_s = _nsa_sample_b(page_table, top_idx.reshape(nbatch, NSA_KV_HEADS * N_SELECT), proj_s3, part,
                                pool_sel, sb_s, l, n_phys)
        conv_t = jnp.swapaxes(state_conv[l], 0, 1)
        o_rec_s, s_gla_s, s_gdn_s = _rec_sample(proj_s, sgla_all, sgdn_all, conv_t, w_gk2, b_gk, gla_nw,
                                                conv_w, a_log, dt_bias, gdn_nw, l, prev_states)
        prev_states = (s_gla_s, s_gdn_s)

        y = lax.dynamic_update_slice(
            y, jnp.concatenate([o_nsa_s.reshape(nbatch, NSA_WIDTH), o_rec_s], axis=1), (rows_p, 0))
        x = _outproj(y, x, w_out_all, nw[3], l)
        x = _ffn(x, nw[4], w1_all, w3_all, w2_all, nw[5], l, 1)
        p = jnp.concatenate([p_prompt[l].reshape(rows_p, -1), p_sample[l].reshape(nbatch, -1)], axis=0)
        x = _ple(x, p, nw[6], wg_all, wp_all, nw[7], l, split_rows=rows_p if l == depth - 1 else None)

        outs["cmp_s"].append(proj_s[:, C_KVC:C_KVC + KVW].reshape((nbatch, 1) + kv_shape))
        outs["sel_s"].append(proj_s[:, C_KVS:C_KVS + KVW].reshape((nbatch, 1) + kv_shape))
        outs["gla_p"].append(s_gla_p)
        outs["gdn_p"].append(s_gdn_p)
        outs["conv_p"].append(jnp.stack([proj[(b + 1) * seq - (CONV_W - 1):(b + 1) * seq, C_DQKV:C_DQKV + 3 * GDN_WIDTH]
                                         for b in range(batch)]))
        outs["conv_s"].append(jnp.concatenate(
            [state_conv[l][:, 1:], proj_s[:, None, C_DQKV:C_DQKV + 3 * GDN_WIDTH]], axis=1))

    st = lambda k: jnp.stack(outs[k])
    return (x[0].reshape(batch, seq, d), x[1].reshape(nbatch, 1, d),
            prev_rows[0].reshape((depth, batch, seq) + kv_shape), st("cmp_s"),
            prev_rows[1].reshape((depth, batch, seq) + kv_shape), st("sel_s"),
            prev_rows[2].reshape((depth, batch, WINDOW) + kv_shape), win_new.reshape(cache_win.shape),
            st("gla_p"), s_gla_s.reshape(state_gla.shape), st("gdn_p"), s_gdn_s.reshape(state_gdn.shape),
            st("conv_p"), st("conv_s"))
```

```python
import functools
import math

import numpy as np
import jax
import jax.numpy as jnp
from jax import lax
from jax.experimental import pallas as pl
from jax.experimental.pallas import tpu as pltpu

F32 = jnp.float32
BF16 = jnp.bfloat16
I32 = jnp.int32

D_MODEL = 2048
HEAD_DIM = 128
NSA_HEADS = 8
NSA_KV_HEADS = 2
NSA_GROUP = NSA_HEADS // NSA_KV_HEADS
CMP_BLOCK = 64
N_SELECT = 8
WINDOW = 512
Q_BLOCK = 128
SEL_FORCE = 1.0e4
GLA_HEADS = 4
GLA_DK = 64
GLA_DV = 128
GLA_GATE_RANK = 16
GLA_GATE_NORM = 16.0
GDN_HEADS = 4
GDN_DK = 128
GDN_DV = 128
CHUNK = 64
CONV_W = 4
N_BUCKETS = 32
T5_MAX_DIST = 128
D_FF = 5632
PLE_DIM = 256
RMS_EPS = 1e-6
NEG = -1e30
MASK_BIG = 2.0 ** 100
PAGE_SIZE = 128

NSA_WIDTH = NSA_HEADS * HEAD_DIM
KVW = 2 * NSA_KV_HEADS * HEAD_DIM
KV_ROWS = 2 * NSA_KV_HEADS
CMP_PAD = 64
GDN_WIDTH = GDN_HEADS * GDN_DV
GLA_WIDTH = GLA_HEADS * GLA_DV
MIX_WIDTH = NSA_WIDTH + GLA_WIDTH + GDN_WIDTH

C_NQ = 0
C_KVC = 1024
C_KVS = 1536
C_KVW = 2048
C_GQ = 2560
C_GK = 2816
C_DQKV = 3072
C_GV = 4608
C_GG = 5120
C_DZ = 5632
C_SM = 6144
PROJ_N = 6272
SM_GATE, SM_GLR, SM_DB, SM_DA = 0, 24, 40, 44

VMEM_LIMIT = 56 * 1024 * 1024
TM = 640
TF = 512
TN_PROJ = 896
TB = 256
TB_GDN = 512


def _sigmoid(x):
    return 1.0 / (1.0 + jnp.exp(-x))


def _silu(x):
    return x * _sigmoid(x)


def _softplus(x):
    return jnp.maximum(x, 0.0) + jnp.log(1.0 + jnp.exp(-jnp.abs(x)))


def _log_sigmoid(x):
    return jnp.minimum(x, 0.0) - jnp.log(1.0 + jnp.exp(-jnp.abs(x)))


def _rms(x, w):
    return x * lax.rsqrt(jnp.mean(x * x, axis=-1, keepdims=True) + RMS_EPS) * w


_NN = (((1,), (0,)), ((), ()))
_NT = (((1,), (1,)), ((), ()))
_TN = (((0,), (0,)), ((), ()))


def _bdot(a, b, dims=_NN):
    return lax.dot_general(a.astype(BF16), b.astype(BF16), dims, preferred_element_type=F32)


def _hdot(a, b, dims=_NN):
    return lax.dot_general(a, b, dims, preferred_element_type=F32, precision=lax.Precision.HIGHEST)


def _params(sem):
    return pltpu.CompilerParams(dimension_semantics=sem, vmem_limit_bytes=VMEM_LIMIT)


def _ffn_kernel(x_ref, nwa_ref, w1_ref, w3_ref, w2_ref, nwb_ref, o_ref, h_ref, acc_ref):
    f = pl.program_id(1)

    @pl.when(f == 0)
    def _():
        h_ref[...] = _rms(x_ref[...], nwa_ref[...]).astype(BF16)
        acc_ref[...] = jnp.zeros_like(acc_ref)

    h = h_ref[...]
    a = jnp.dot(h, w1_ref[...], preferred_element_type=F32)
    b = jnp.dot(h, w3_ref[...], preferred_element_type=F32)
    g = (_silu(a) * b).astype(BF16)
    acc_ref[...] += jnp.dot(g, w2_ref[...], preferred_element_type=F32)

    @pl.when(f == pl.num_programs(1) - 1)
    def _():
        o_ref[...] = x_ref[...] + 0.5 * _rms(acc_ref[...], nwb_ref[...])


def _ffn(x, nwa, w1, w3, w2, nwb, layer, which):
    m, d = x.shape
    dff = w1.shape[-1]
    row = lambda i, f: (i, 0)
    return pl.pallas_call(
        _ffn_kernel,
        out_shape=jax.ShapeDtypeStruct((m, d), F32),
        grid=(m // TM, dff // TF),
        in_specs=[pl.BlockSpec((TM, d), row),
                  pl.BlockSpec((1, d), lambda i, f: (0, 0)),
                  pl.BlockSpec((None, None, d, TF), lambda i, f: (layer, which, 0, f)),
                  pl.BlockSpec((None, None, d, TF), lambda i, f: (layer, which, 0, f)),
                  pl.BlockSpec((None, None, TF, d), lambda i, f: (layer, which, f, 0)),
                  pl.BlockSpec((1, d), lambda i, f: (0, 0))],
        out_specs=pl.BlockSpec((TM, d), row),
        scratch_shapes=[pltpu.VMEM((TM, d), BF16), pltpu.VMEM((TM, d), F32)],
        compiler_params=_params(("parallel", "arbitrary")),
        name="ffn",
    )(x, nwa, w1, w3, w2, nwb)


def _proj_kernel(x_ref, nw_ref, w_ref, o_ref, h_ref):
    @pl.when(pl.program_id(1) == 0)
    def _():
        h_ref[...] = _rms(x_ref[...], nw_ref[...]).astype(BF16)

    o_ref[...] = jnp.dot(h_ref[...], w_ref[...], preferred_element_type=F32)


def _proj(x, nw, w, layer):
    m, d = x.shape
    n = w.shape[-1]
    return pl.pallas_call(
        _proj_kernel,
        out_shape=jax.ShapeDtypeStruct((m, n), F32),
        grid=(m // TM, n // TN_PROJ),
        in_specs=[pl.BlockSpec((TM, d), lambda i, j: (i, 0)),
                  pl.BlockSpec((1, d), lambda i, j: (0, 0)),
                  pl.BlockSpec((None, d, TN_PROJ), lambda i, j: (layer, 0, j))],
        out_specs=pl.BlockSpec((TM, TN_PROJ), lambda i, j: (i, j)),
        scratch_shapes=[pltpu.VMEM((TM, d), BF16)],
        compiler_params=_params(("parallel", "arbitrary")),
        name="proj",
    )(x, nw, w)


def _outproj_kernel(y_ref, x_ref, w_ref, nw_ref, o_ref):
    z = jnp.dot(y_ref[...].astype(BF16), w_ref[...], preferred_element_type=F32)
    o_ref[...] = x_ref[...] + _rms(z, nw_ref[...])


def _outproj(y, x, w, nw, layer):
    m, d = x.shape
    k = y.shape[1]
    return pl.pallas_call(
        _outproj_kernel,
        out_shape=jax.ShapeDtypeStruct((m, d), F32),
        grid=(m // TM,),
        in_specs=[pl.BlockSpec((TM, k), lambda i: (i, 0)),
                  pl.BlockSpec((TM, d), lambda i: (i, 0)),
                  pl.BlockSpec((None, k, d), lambda i: (layer, 0, 0)),
                  pl.BlockSpec((1, d), lambda i: (0, 0))],
        out_specs=pl.BlockSpec((TM, d), lambda i: (i, 0)),
        compiler_params=_params(("parallel",)),
        name="outproj",
    )(y, x, w, nw)


def _ple_kernel(split, x_ref, p_ref, nwa_ref, wg_ref, wp_ref, nwb_ref, *o_refs):
    x = x_ref[...]
    gate = _sigmoid(jnp.dot(_rms(x, nwa_ref[...]).astype(BF16), wg_ref[...], preferred_element_type=F32))
    pp = jnp.dot(p_ref[...].astype(BF16), wp_ref[...], preferred_element_type=F32)
    y = x + _rms(gate * pp, nwb_ref[...])
    o_refs[0][...] = y
    if split is not None:
        @pl.when(pl.program_id(0) == pl.num_programs(0) - 1)
        def _():
            o_refs[1][...] = y[split:]


def _ple(x, p, nwa, wg, wp, nwb, layer, split_rows=None):
    m, d = x.shape
    pd = p.shape[1]
    nblk = m // TM
    out_shape = jax.ShapeDtypeStruct((m, d), F32)
    out_specs = pl.BlockSpec((TM, d), lambda i: (i, 0))
    split = None
    if split_rows is not None:
        split = split_rows - (nblk - 1) * TM
        assert 0 < split and m - split_rows == TM - split
        out_shape = (jax.ShapeDtypeStruct((split_rows, d), F32), jax.ShapeDtypeStruct((m - split_rows, d), F32))
        out_specs = (out_specs, pl.BlockSpec((m - split_rows, d), lambda i: (0, 0)))
    return pl.pallas_call(
        functools.partial(_ple_kernel, split),
        out_shape=out_shape,
        grid=(nblk,),
        in_specs=[pl.BlockSpec((TM, d), lambda i: (i, 0)),
                  pl.BlockSpec((TM, pd), lambda i: (i, 0)),
                  pl.BlockSpec((1, d), lambda i: (0, 0)),
                  pl.BlockSpec((None, d, d), lambda i: (layer, 0, 0)),
                  pl.BlockSpec((None, pd, d), lambda i: (layer, 0, 0)),
                  pl.BlockSpec((1, d), lambda i: (0, 0))],
        out_specs=out_specs,
        compiler_params=_params(("arbitrary",)),
        name="ple",
    )(x, p, nwa, wg, wp, nwb)


def _t5_bucket_np(dist):
    n = np.maximum(dist, 0)
    exact = N_BUCKETS // 2
    val = (np.log(np.maximum(n, 1).astype(np.float32) / np.float32(exact))
           / np.float32(math.log(T5_MAX_DIST / exact)) * np.float32(N_BUCKETS - exact))
    large = exact + val.astype(np.int32)
    return np.where(n < exact, n, np.minimum(large, N_BUCKETS - 1)).astype(np.int32)


def _bias_table(t5_bias, dist, valid):
    bucket = jnp.asarray(np.where(valid, _t5_bucket_np(dist), -1).astype(np.int32))[None]
    hshape = (t5_bias.shape[1],) + (1,) * dist.ndim
    out = jnp.full((t5_bias.shape[1],) + dist.shape, NEG, F32)
    for k in range(N_BUCKETS):
        out = jnp.where(bucket == k, t5_bias[k].reshape(hshape), out)
    return out


def _prompt_bias_tables(t5_bias, seq):
    nb = seq // CMP_BLOCK
    qpos = np.arange(seq)[:, None]
    d_c = qpos - (np.arange(nb) * CMP_BLOCK + CMP_BLOCK - 1)[None, :]
    bias_c = _bias_table(t5_bias, d_c, d_c >= 0)
    i = np.arange(Q_BLOCK)[:, None]
    jj = np.arange(Q_BLOCK)[None, :]
    d_s = np.stack([Q_BLOCK * dl + i - jj for dl in range(3)] + [i - jj - Q_BLOCK])
    bias_s = jnp.transpose(_bias_table(t5_bias, d_s, d_s >= 0), (1, 0, 2, 3))
    nwc = WINDOW // Q_BLOCK + 1
    d_w = np.stack([i + WINDOW - Q_BLOCK * cw - jj for cw in range(nwc)] + [i - jj - Q_BLOCK])
    bias_w = jnp.transpose(_bias_table(t5_bias, d_w, (d_w >= 0) & (d_w < WINDOW)), (1, 0, 2, 3))
    return bias_c, bias_s, bias_w


def _sample_bias_tables(t5_bias, past, win_len):
    nb_past = past // CMP_BLOCK
    head_g = (np.arange(NSA_HEADS) // NSA_GROUP)[:, None]

    def per_head(dist, valid):
        lane_cg = np.arange(dist.shape[-1]) % KV_ROWS
        tbl = _bias_table(t5_bias, dist, valid)
        own = jnp.asarray(lane_cg[None] == head_g)
        return tbl, own

    lane = np.arange(CMP_PAD * KV_ROWS)
    n = lane // KV_ROWS
    d_c = past - (n * CMP_BLOCK + CMP_BLOCK - 1)
    tbl, own = per_head(d_c, (d_c >= 0) & (n <= nb_past))
    bias_c = jnp.where(own, tbl, NEG)
    lane = np.arange(CMP_BLOCK * KV_ROWS)
    blk = np.arange(nb_past + 1)[:, None]
    d_s = past - (blk * CMP_BLOCK + (lane // KV_ROWS)[None, :])
    tbl, own = per_head(d_s, d_s >= 0)
    bias_s = jnp.transpose(jnp.where(own[:, None, :], tbl, NEG), (1, 0, 2))
    lane = np.arange(win_len * KV_ROWS)
    wpos = past - win_len + 1 + lane // KV_ROWS
    d_w = past - wpos
    tbl, own = per_head(d_w, (d_w >= 0) & (d_w < WINDOW) & (wpos >= 0))
    bias_w = jnp.where(own, tbl, NEG)
    return bias_c, bias_s, bias_w


def _compress_kernel(x_ref, w_ref, o_ref):
    r = x_ref.shape[0] // CMP_BLOCK
    x = x_ref[...].reshape(r, CMP_BLOCK, KVW)
    o_ref[...] = jnp.sum(x * w_ref[...][None], axis=1)


def _compress_prompt(proj, wtile, rows):
    rb = 512
    return pl.pallas_call(
        _compress_kernel,
        out_shape=jax.ShapeDtypeStruct((rows // CMP_BLOCK, KVW), F32),
        grid=(rows // rb,),
        in_specs=[pl.BlockSpec((rb, KVW), lambda i: (i, C_KVC // KVW)),
                  pl.BlockSpec((CMP_BLOCK, KVW), lambda i: (0, 0))],
        out_specs=pl.BlockSpec((rb // CMP_BLOCK, KVW), lambda i: (i, 0)),
        compiler_params=_params(("parallel",)),
        name="compress_prompt",
    )(proj, wtile)


KEY_CHUNK = 2 * Q_BLOCK
UNROLL = 2


def _nsa_prompt_kernel(q_ref, sm_ref, cmp_ref, kvs_ref, kvw_ref, bc_ref, bs_ref, bw_ref, e_ref, o_ref,
                       s_ref, mx_ref, l_ref, acc_ref):
    j = pl.program_id(1)
    nb = cmp_ref.shape[0]
    hg, qb, dh, kc = NSA_GROUP, Q_BLOCK, HEAD_DIM, KEY_CHUNK
    rows = hg * qb
    nwt = bw_ref.shape[0] - 1
    gates = _sigmoid(sm_ref[:, SM_GATE:SM_GATE + 3 * NSA_HEADS])
    qi = lax.broadcasted_iota(I32, (qb, nb), 0)
    ni = lax.broadcasted_iota(I32, (qb, nb), 1)
    cur = (qb // CMP_BLOCK) * j + qi // CMP_BLOCK
    forced = (ni == cur) | (ni == cur - 1) | (ni == 0)
    started = ni <= cur
    groups = range(NSA_KV_HEADS)
    kcols = [slice(g * dh, (g + 1) * dh) for g in groups]
    vcols = [slice((NSA_KV_HEADS + g) * dh, (NSA_KV_HEADS + g + 1) * dh) for g in groups]
    head_rows = [slice(g * hg, (g + 1) * hg) for g in groups]

    def fold(x, op):
        out = x[:, 0:qb]
        for t in range(1, x.shape[1] // qb):
            out = op(out, x[:, t * qb:(t + 1) * qb])
        return out

    q16, o_c, score = [], [], []
    for g in groups:
        q = jnp.concatenate([q_ref[:, (g * hg + h) * dh:(g * hg + h + 1) * dh] for h in range(hg)], axis=0)
        q16.append((q * (dh ** -0.5)).astype(BF16))
        bias = bc_ref[head_rows[g]].reshape(rows, nb)
        valid = bias > -1e29
        s = jnp.where(valid, _bdot(q16[g], cmp_ref[:, kcols[g]], _NT) + bias, NEG)
        p = jnp.where(valid, jnp.exp(s - jnp.max(s, axis=-1, keepdims=True)), 0.0)
        p = p / jnp.maximum(jnp.sum(p, axis=-1, keepdims=True), 1e-30)
        o_c.append(_bdot(p, cmp_ref[:, vcols[g]]))
        imp = p[0:qb]
        for h in range(1, hg):
            imp = imp + p[h * qb:(h + 1) * qb]
        score.append(jnp.where(started, jnp.where(forced, SEL_FORCE, imp), -1.0))

    unsel = [jnp.ones((qb, nb), F32) for _ in groups]
    for _ in range(min(N_SELECT, nb)):
        for g in groups:
            hit = ni == jnp.argmax(score[g], axis=-1, keepdims=True).astype(I32)
            unsel[g] = jnp.where(hit & started, 0.0, unsel[g])
            score[g] = jnp.where(hit, -3e38, score[g])
    unsel = [u.astype(BF16) for u in unsel]

    mx_ref[...] = jnp.full(mx_ref.shape, NEG, F32)
    l_ref[...] = jnp.zeros(l_ref.shape, F32)
    acc_ref[...] = jnp.zeros(acc_ref.shape, F32)
    n_chunks = j // (kc // qb) + 1

    def key_rows(c):
        return pl.ds(pl.multiple_of(c * kc, kc), kc)

    def sel_bias(g, c):
        tiles = []
        for t in range(kc // qb):
            back = j - (c * (kc // qb) + t)
            tiles.append(bs_ref[jnp.where(back < 0, 3, jnp.minimum(back, 2)), head_rows[g]].reshape(rows, qb))
        return jnp.concatenate(tiles, axis=1)

    def pass1_chunk(c):
        for g in groups:
            key_mask = jnp.dot(unsel[g], e_ref[c], preferred_element_type=F32)
            s_ = (_bdot(q16[g], kvs_ref[key_rows(c), kcols[g]], _NT) + sel_bias(g, c)
                  + jnp.concatenate([key_mask] * hg, axis=0))
            s_ref[g, c] = s_
            mx_ref[g] = jnp.maximum(mx_ref[g], fold(s_, jnp.maximum))

    def unrolled(chunk_fn):
        def body(t, carry):
            for u in range(UNROLL):
                chunk_fn(t * UNROLL + u)
            return carry
        lax.fori_loop(0, (n_chunks + UNROLL - 1) // UNROLL, body, 0)

    unrolled(pass1_chunk)
    for g in groups:
        mx_ref[g] = jnp.broadcast_to(jnp.max(mx_ref[g], axis=-1, keepdims=True), mx_ref.shape[1:])

    def pass2_chunk(c):
        for g in groups:
            p_ = jnp.exp(s_ref[g, c] - jnp.concatenate([mx_ref[g]] * (kc // qb), axis=1))
            l_ref[g] += fold(p_, jnp.add)
            acc_ref[g] += _bdot(p_, kvs_ref[key_rows(c), vcols[g]])

    unrolled(pass2_chunk)
    o_s = [acc_ref[g] / jnp.maximum(jnp.sum(l_ref[g], axis=-1, keepdims=True), 1e-30) for g in groups]

    lead = jnp.maximum(nwt - 1 - j, 0)
    w_rows = pl.ds(pl.multiple_of(jnp.maximum(j - (nwt - 1), 0) * qb, qb), nwt * qb)
    o_w = []
    for g in groups:
        bias = jnp.concatenate(
            [bw_ref[jnp.minimum(t + lead, nwt), head_rows[g]].reshape(rows, qb) for t in range(nwt)], axis=1)
        s = _bdot(q16[g], kvw_ref[w_rows, kcols[g]], _NT) + bias
        p = jnp.exp(s - jnp.max(fold(s, jnp.maximum), axis=-1, keepdims=True))
        den = jnp.sum(fold(p, jnp.add), axis=-1, keepdims=True)
        o_w.append(_bdot(p, kvw_ref[w_rows, vcols[g]]) / jnp.maximum(den, 1e-30))

    for g in groups:
        for h in range(hg):
            c0 = SM_GATE + (g * hg + h) * 3
            rs = slice(h * qb, (h + 1) * qb)
            o_ref[:, (g * hg + h) * dh:(g * hg + h + 1) * dh] = (
                o_c[g][rs] * gates[:, c0:c0 + 1] + o_s[g][rs] * gates[:, c0 + 1:c0 + 2]
                + o_w[g][rs] * gates[:, c0 + 2:c0 + 3])


def _nsa_prompt(proj, cmpkv, kvs16, kvw16, bias_c, bias_s, bias_w, expand, batch, seq):
    nqb = seq // Q_BLOCK
    nb = seq // CMP_BLOCK
    rows = NSA_GROUP * Q_BLOCK
    full = lambda shape: pl.BlockSpec(shape, lambda b, j: (0,) * len(shape))
    return pl.pallas_call(
        _nsa_prompt_kernel,
        out_shape=jax.ShapeDtypeStruct((proj.shape[0], MIX_WIDTH), F32),
        grid=(batch, nqb),
        in_specs=[pl.BlockSpec((Q_BLOCK, NSA_WIDTH), lambda b, j: (b * nqb + j, 0)),
                  pl.BlockSpec((Q_BLOCK, 128), lambda b, j: (b * nqb + j, C_SM // 128)),
                  pl.BlockSpec((nb, KVW), lambda b, j: (b, 0)),
                  pl.BlockSpec((seq, KVW), lambda b, j: (b, 0)),
                  pl.BlockSpec((seq, KVW), lambda b, j: (b, 0)),
                  pl.BlockSpec((NSA_HEADS, Q_BLOCK, nb), lambda b, j: (0, j, 0)),
                  full(bias_s.shape), full(bias_w.shape), full(expand.shape)],
        out_specs=pl.BlockSpec((Q_BLOCK, NSA_WIDTH), lambda b, j: (b * nqb + j, 0)),
        scratch_shapes=[pltpu.VMEM((NSA_KV_HEADS, seq // KEY_CHUNK, rows, KEY_CHUNK), F32),
                        pltpu.VMEM((NSA_KV_HEADS, rows, Q_BLOCK), F32), pltpu.VMEM((NSA_KV_HEADS, rows, Q_BLOCK), F32),
                        pltpu.VMEM((NSA_KV_HEADS, rows, HEAD_DIM), F32)],
        compiler_params=_params(("parallel", "arbitrary")),
        name="nsa_prompt",
    )(proj, proj, cmpkv, kvs16, kvw16, bias_c, bias_s, bias_w, expand)


ROWS_BLK = 512


def _cache_rows_kernel(per_b, kvc_ref, kvs_ref, kvw_ref, *rest):
    oc_ref, os_ref, ow_ref, s16_ref, w16_ref = rest[-5:]
    n = kvc_ref.shape[0]
    dh = HEAD_DIM

    def put(o_ref, x_ref):
        for cg in range(KV_ROWS):
            o_ref[pl.ds(cg, n, stride=KV_ROWS), :] = x_ref[:, cg * dh:(cg + 1) * dh]

    put(oc_ref, kvc_ref)
    put(os_ref, kvs_ref)
    s16_ref[...] = kvs_ref[...].astype(BF16)
    w16_ref[...] = kvw_ref[...].astype(BF16)

    @pl.when(pl.program_id(0) % per_b == per_b - 1)
    def _():
        put(ow_ref, kvw_ref)


def _cache_rows(proj, batch, seq, depth, layer, prev):
    assert seq % ROWS_BLK == 0 and WINDOW == ROWS_BLK
    per_b = seq // ROWS_BLK
    nblk = batch * per_b
    blk = lambda c: pl.BlockSpec((ROWS_BLK, KVW), lambda i: (i, c // KVW))
    out_rows = ROWS_BLK * KV_ROWS
    return pl.pallas_call(
        functools.partial(_cache_rows_kernel, per_b),
        out_shape=(jax.ShapeDtypeStruct((depth * batch * seq * KV_ROWS, HEAD_DIM), F32),
                   jax.ShapeDtypeStruct((depth * batch * seq * KV_ROWS, HEAD_DIM), F32),
                   jax.ShapeDtypeStruct((depth * batch * WINDOW * KV_ROWS, HEAD_DIM), F32),
                   jax.ShapeDtypeStruct((batch * seq, KVW), BF16),
                   jax.ShapeDtypeStruct((batch * seq, KVW), BF16)),
        grid=(nblk,),
        in_specs=[blk(C_KVC), blk(C_KVS), blk(C_KVW)]
        + [pl.BlockSpec(memory_space=pl.ANY)] * len(prev),
        out_specs=(pl.BlockSpec((out_rows, HEAD_DIM), lambda i: (layer * nblk + i, 0)),
                   pl.BlockSpec((out_rows, HEAD_DIM), lambda i: (layer * nblk + i, 0)),
                   pl.BlockSpec((out_rows, HEAD_DIM), lambda i: (layer * batch + i // per_b, 0)),
                   pl.BlockSpec((ROWS_BLK, KVW), lambda i: (i, 0)),
                   pl.BlockSpec((ROWS_BLK, KVW), lambda i: (i, 0))),
        input_output_aliases={3 + a: a for a in range(len(prev))},
        compiler_params=_params(("arbitrary",)),
        name="cache_rows",
    )(proj, proj, proj, *prev)


_BNN = (((2,), (1,)), ((0,), (0,)))
_BNT = (((2,), (2,)), ((0,), (0,)))
_BTN = (((1,), (1,)), ((0,), (0,)))


def _bdot_b(a, b, dims=_BNN):
    return lax.dot_general(a.astype(BF16), b.astype(BF16), dims, preferred_element_type=F32)


def _tri_masks():
    r = lax.broadcasted_iota(I32, (CHUNK, CHUNK), 0)
    c = lax.broadcasted_iota(I32, (CHUNK, CHUNK), 1)
    return r, c


def _gla_prompt_kernel(q_ref, k_ref, v_ref, gg_ref, sm_ref, w2_ref, b2_ref, nw_ref, _y_ref, o_ref, so_ref, s_ref):
    t = pl.program_id(1)

    @pl.when(t == 0)
    def _():
        s_ref[...] = jnp.zeros_like(s_ref)

    r, c = _tri_masks()
    lower = r >= c
    tril = jnp.where(lower, 1.0, 0.0).astype(F32)
    dk, dv = GLA_DK, GLA_DV
    nch = q_ref.shape[0] // CHUNK
    pairs = [(ch, h) for ch in range(nch) for h in range(GLA_HEADS)]
    rs = lambda ch: slice(ch * CHUNK, (ch + 1) * CHUNK)
    stack = lambda fn: jnp.stack([fn(ch, h) for ch, h in pairs], axis=0)

    log_a = _log_sigmoid(_bdot(sm_ref[:, SM_GLR:SM_GLR + GLA_GATE_RANK], w2_ref[...]) + b2_ref[...]) / GLA_GATE_NORM
    width = GLA_HEADS * dk
    b_all = _hdot(tril, jnp.concatenate([log_a[rs(ch)] for ch in range(nch)], axis=1))
    b_all_t = b_all.T
    b = stack(lambda ch, h: b_all[:, ch * width + h * dk:ch * width + (h + 1) * dk])
    b_last = b[:, CHUNK - 1:CHUNK, :]
    b_last_col = stack(lambda ch, h: b_all_t[ch * width + h * dk:ch * width + (h + 1) * dk, CHUNK - 1:CHUNK])
    q = stack(lambda ch, h: q_ref[rs(ch), h * dk:(h + 1) * dk]) * (dk ** -0.5)
    k = stack(lambda ch, h: k_ref[rs(ch), h * dk:(h + 1) * dk])
    v = stack(lambda ch, h: v_ref[rs(ch), h * dv:(h + 1) * dv])
    qe = q * jnp.exp(b)
    att = jnp.where(lower[None], _bdot_b(qe, k * jnp.exp(-b), _BNT), 0.0)
    o_intra = _bdot_b(att, v)
    kv = _bdot_b(k * jnp.exp(b_last - b), v, _BTN)
    s_decay = jnp.exp(b_last_col)

    for p, (ch, h) in enumerate(pairs):
        s = s_ref[h]
        o = _bdot(qe[p], s) + o_intra[p]
        s_ref[h] = s_decay[p] * s + kv[p]
        o_ref[rs(ch), h * dv:(h + 1) * dv] = _rms(o, nw_ref[...]) * _silu(gg_ref[rs(ch), h * dv:(h + 1) * dv])
    so_ref[0] = s_ref[...]


def _gla_prompt(proj, w2, b2, nw, batch, seq, y):
    nt = seq // TB
    row = lambda w, col: pl.BlockSpec((TB, w), lambda b, t: (b * nt + t, col // w))
    full = lambda shape: pl.BlockSpec(shape, lambda b, t: (0,) * len(shape))
    return pl.pallas_call(
        _gla_prompt_kernel,
        out_shape=(jax.ShapeDtypeStruct(y.shape, F32),
                   jax.ShapeDtypeStruct((batch, GLA_HEADS, GLA_DK, GLA_DV), F32)),
        grid=(batch, nt),
        in_specs=[row(GLA_HEADS * GLA_DK, C_GQ), row(GLA_HEADS * GLA_DK, C_GK), row(GLA_WIDTH, C_GV),
                  row(GLA_WIDTH, C_GG), row(128, C_SM), full(w2.shape), full(b2.shape), full(nw.shape),
                  pl.BlockSpec(memory_space=pl.ANY)],
        out_specs=(pl.BlockSpec((TB, GLA_WIDTH), lambda b, t: (b * nt + t, NSA_WIDTH // GLA_WIDTH)),
                   pl.BlockSpec((1, GLA_HEADS, GLA_DK, GLA_DV), lambda b, t: (b, 0, 0, 0))),
        scratch_shapes=[pltpu.VMEM((GLA_HEADS, GLA_DK, GLA_DV), F32)],
        input_output_aliases={8: 0},
        compiler_params=_params(("parallel", "arbitrary")),
        name="gla_prompt",
    )(proj, proj, proj, proj, proj, w2, b2, nw, y)


def _unit_lower_inverse(m, r, c):
    eye = jnp.where(r == c, 1.0, 0.0).astype(F32)[None]
    base = 8
    m8 = jnp.where(((r // base) == (c // base))[None], m, 0.0)
    m2 = _bdot_b(m8, m8)
    m4 = _bdot_b(m2, m2)
    t = _bdot_b(_bdot_b(eye - m8, eye + m2), eye + m4)
    s = base
    while s < CHUNK:
        off = ((r // (2 * s)) == (c // (2 * s))) & ((r // s) != (c // s))
        t = t - _bdot_b(t, _bdot_b(jnp.where(off[None], m, 0.0), t))
        s *= 2
    return t


def _gdn_prompt_kernel(x_ref, dz_ref, sm_ref, cw_ref, al_ref, dt_ref, nw_ref, _y_ref, o_ref, so_ref, s_ref,
                       tail_ref):
    t = pl.program_id(1)

    @pl.when(t == 0)
    def _():
        s_ref[...] = jnp.zeros_like(s_ref)
        tail_ref[...] = jnp.zeros_like(tail_ref)

    tb = x_ref.shape[0]
    x = x_ref[...]
    xc = jnp.concatenate([tail_ref[...], x], axis=0)
    off = 8 - (CONV_W - 1)
    y = xc[off:off + tb] * cw_ref[0:1, :]
    for jw in range(1, CONV_W):
        y = y + xc[off + jw:off + jw + tb] * cw_ref[jw:jw + 1, :]
    y = _silu(y)
    tail_ref[...] = x[tb - 8:tb]

    beta_all = _sigmoid(sm_ref[:, SM_DB:SM_DB + GDN_HEADS])
    g_all = -jnp.exp(al_ref[...]) * _softplus(sm_ref[:, SM_DA:SM_DA + GDN_HEADS] + dt_ref[...])
    r, c = _tri_masks()
    lower = r >= c
    strict = r > c
    tril = jnp.where(lower, 1.0, 0.0).astype(F32)
    dk, dv = GDN_DK, GDN_DV
    nch = tb // CHUNK
    pairs = [(ch, h) for ch in range(nch) for h in range(GDN_HEADS)]
    rs = lambda ch: slice(ch * CHUNK, (ch + 1) * CHUNK)
    stack = lambda fn: jnp.stack([fn(ch, h) for ch, h in pairs], axis=0)

    cq = stack(lambda ch, h: y[rs(ch), h * dk:(h + 1) * dk])
    ck = stack(lambda ch, h: y[rs(ch), GDN_WIDTH + h * dk:GDN_WIDTH + (h + 1) * dk])
    v = stack(lambda ch, h: y[rs(ch), 2 * GDN_WIDTH + h * dv:2 * GDN_WIDTH + (h + 1) * dv])
    q = cq * lax.rsqrt(jnp.sum(cq * cq, axis=-1, keepdims=True) + 1e-6) * (dk ** -0.5)
    k = ck * lax.rsqrt(jnp.sum(ck * ck, axis=-1, keepdims=True) + 1e-6)
    beta = stack(lambda ch, h: jnp.broadcast_to(beta_all[rs(ch), h:h + 1], (CHUNK, dk)))
    g_cols = jnp.concatenate([g_all[rs(ch)] for ch in range(nch)], axis=1)
    gam_cols = _hdot(tril, g_cols)
    gam_rows = gam_cols.T
    gam = jnp.stack([jnp.broadcast_to(gam_cols[:, p:p + 1], (CHUNK, dk)) for p in range(len(pairs))], axis=0)
    decay = jnp.exp(jnp.where(lower[None], gam[:, :, 0:CHUNK] - gam_rows[:, None, :], NEG))
    kb = k * beta
    m = jnp.where(strict[None], _bdot_b(kb, k, _BNT) * decay, 0.0)
    tinv = _unit_lower_inverse(m, r, c)
    eg = jnp.exp(gam)
    u = _bdot_b(tinv, v * beta)
    w = _bdot_b(tinv, kb * eg)
    att = _bdot_b(q, k, _BNT) * decay
    qe = q * eg
    g_last = gam[:, CHUNK - 1:CHUNK, :]
    kd = k * jnp.exp(g_last - gam)
    eg_last = jnp.exp(g_last)

    for p, (ch, h) in enumerate(pairs):
        s = s_ref[h]
        v_new = u[p] - _bdot(w[p], s)
        o = _bdot(qe[p], s) + _bdot(att[p], v_new)
        s_ref[h] = eg_last[p] * s + _bdot(kd[p], v_new, _TN)
        o_ref[rs(ch), h * dv:(h + 1) * dv] = _rms(o, nw_ref[...]) * _silu(dz_ref[rs(ch), h * dv:(h + 1) * dv])
    so_ref[0] = s_ref[...]


def _gdn_prompt(proj, cw, a_log, dt_bias, nw, batch, seq, y):
    nt = seq // TB_GDN
    row = lambda w, col: pl.BlockSpec((TB_GDN, w), lambda b, t: (b * nt + t, col // w))
    full = lambda shape: pl.BlockSpec(shape, lambda b, t: (0,) * len(shape))
    return pl.pallas_call(
        _gdn_prompt_kernel,
        out_shape=(jax.ShapeDtypeStruct(y.shape, F32),
                   jax.ShapeDtypeStruct((batch, GDN_HEADS, GDN_DK, GDN_DV), F32)),
        grid=(batch, nt),
        in_specs=[row(3 * GDN_WIDTH, C_DQKV), row(GDN_WIDTH, C_DZ), row(128, C_SM),
                  full(cw.shape), full(a_log.shape), full(dt_bias.shape), full(nw.shape),
                  pl.BlockSpec(memory_space=pl.ANY)],
        out_specs=(pl.BlockSpec((TB_GDN, GDN_WIDTH), lambda b, t: (b * nt + t, (NSA_WIDTH + GLA_WIDTH) // GDN_WIDTH)),
                   pl.BlockSpec((1, GDN_HEADS, GDN_DK, GDN_DV), lambda b, t: (b, 0, 0, 0))),
        scratch_shapes=[pltpu.VMEM((GDN_HEADS, GDN_DK, GDN_DV), F32), pltpu.VMEM((8, 3 * GDN_WIDTH), F32)],
        input_output_aliases={7: 0},
        compiler_params=_params(("parallel", "arbitrary")),
        name="gdn_prompt",
    )(proj, proj, proj, cw, a_log, dt_bias, nw, y)


def _masked_softmax_rows(s):
    valid = s > -1e29
    p = jnp.where(valid, jnp.exp(s - jnp.max(s, axis=-1, keepdims=True)), 0.0)
    return p / jnp.maximum(jnp.sum(p, axis=-1, keepdims=True), 1e-30)


def _head_rows(q_ref, sb, lo, hi):
    dh = HEAD_DIM
    q = jnp.concatenate([q_ref[sb, :, h * dh:(h + 1) * dh] for h in range(lo, hi)], axis=0)
    return (q * (dh ** -0.5)).astype(BF16)


def _kv_rows(kv_ref, sb):
    dh = HEAD_DIM
    return jnp.concatenate([kv_ref[sb, :, r * dh:(r + 1) * dh] for r in range(KV_ROWS)], axis=0)


def _value_weights(p):
    return pltpu.roll(p, NSA_KV_HEADS, 1)


SN = 2


def _nsa_sample_a_kernel(n_pages, nb_past, pt_ref, q_ref, kvc_ref, kvw_ref, sm_ref, *rest):
    page_refs = rest[:SN * n_pages]
    win_ref, w4_ref, bc_ref, bw_ref = rest[SN * n_pages:SN * n_pages + 4]
    part_ref, idx_ref, wo_ref, cmp_ref = rest[-4:]
    dh = HEAD_DIM
    blk_rows = CMP_BLOCK * KV_ROWS
    bpp = PAGE_SIZE // CMP_BLOCK
    wl = win_ref.shape[0] // SN
    w4 = w4_ref[...]
    lanes = cmp_ref.shape[1]
    lane = lax.broadcasted_iota(I32, (NSA_KV_HEADS, lanes), 1)
    gi = lax.broadcasted_iota(I32, (NSA_KV_HEADS, lanes), 0)
    n = lane // KV_ROWS
    cur = nb_past
    cand = ((lane % KV_ROWS) == gi) & (n <= cur)
    forced = (n == cur) | (n == cur - 1) | (n == 0)
    li = lax.broadcasted_iota(I32, (NSA_KV_HEADS, N_SELECT), 1)

    q16s, o_cs, scores = [], [], []
    for sb in range(SN):
        for p in range(n_pages):
            x = page_refs[sb * n_pages + p][...]
            sums = []
            for half in range(bpp):
                pr = x[half * blk_rows:(half + 1) * blk_rows] * w4
                s8 = jnp.sum(pr.reshape(blk_rows // 8, 8, dh), axis=0)
                sums.append(s8[0:KV_ROWS] + s8[KV_ROWS:2 * KV_ROWS])
            cmp_ref[sb, p * bpp * KV_ROWS:(p + 1) * bpp * KV_ROWS, :] = jnp.concatenate(sums, axis=0)
        r0 = nb_past * KV_ROWS
        cmp_ref[sb, r0:r0 + KV_ROWS, :] = _kv_rows(kvc_ref, sb) * w4[0:KV_ROWS]
        cmp_ref[sb, r0 + KV_ROWS:, :] = jnp.zeros((lanes - r0 - KV_ROWS, dh), F32)
        cm16 = cmp_ref[sb].astype(BF16)

        q16 = _head_rows(q_ref, sb, 0, NSA_HEADS)
        p = _masked_softmax_rows(_bdot(q16, cm16, _NT) + bc_ref[...])
        q16s.append(q16)
        o_cs.append(_bdot(_value_weights(p), cm16))
        imp = jnp.concatenate([jnp.sum(p[g * NSA_GROUP:(g + 1) * NSA_GROUP], axis=0, keepdims=True)
                               for g in range(NSA_KV_HEADS)], axis=0)
        scores.append(jnp.where(cand, jnp.where(forced, SEL_FORCE, imp), -3e38))

    tops = [jnp.zeros((NSA_KV_HEADS, N_SELECT), I32) for _ in range(SN)]
    for r in range(N_SELECT):
        for sb in range(SN):
            a = jnp.argmax(scores[sb], axis=-1, keepdims=True).astype(I32)
            tops[sb] = jnp.where(li == r, a // KV_ROWS, tops[sb])
            scores[sb] = jnp.where(lane == a, -3e38, scores[sb])

    for sb in range(SN):
        idx_ref[sb] = tops[sb]
        ws = slice(sb * wl, (sb + 1) * wl)
        wo_ref[ws, :] = pltpu.roll(win_ref[ws, :], wl - KV_ROWS, 0)
        wo_ref[(sb + 1) * wl - KV_ROWS:(sb + 1) * wl, :] = _kv_rows(kvw_ref, sb)
        w16 = wo_ref[ws, :].astype(BF16)
        pw = _masked_softmax_rows(_bdot(q16s[sb], w16, _NT) + bw_ref[...])
        o_w = _bdot(_value_weights(pw), w16)

        gates = _sigmoid(sm_ref[sb, :, SM_GATE:SM_GATE + 3 * NSA_HEADS])
        for h in range(NSA_HEADS):
            c0 = 3 * h
            part_ref[sb, :, h * dh:(h + 1) * dh] = (
                o_cs[sb][h:h + 1] * gates[:, c0:c0 + 1] + o_w[h:h + 1] * gates[:, c0 + 2:c0 + 3])


def _nsa_sample_a(page_table, proj_s3, pool_cmp, win_rows, w4, bias_c, bias_w, layer, n_phys, prev_win):
    nbatch, n_pages = page_table.shape
    nb_past = n_pages * PAGE_SIZE // CMP_BLOCK
    wl = bias_w.shape[1]
    page_rows = PAGE_SIZE * KV_ROWS
    nsteps = nbatch // SN
    col = lambda w, c: pl.BlockSpec((SN, 1, w), lambda i, pt: (i, 0, c // w))
    full = lambda shape: pl.BlockSpec(shape, lambda i, pt: (0,) * len(shape))
    page = lambda sb, p: pl.BlockSpec((page_rows, HEAD_DIM), lambda i, pt: (layer * n_phys + pt[SN * i + sb, p], 0))
    gs = pltpu.PrefetchScalarGridSpec(
        num_scalar_prefetch=1, grid=(nsteps,),
        in_specs=[col(NSA_WIDTH, C_NQ), col(KVW, C_KVC), col(KVW, C_KVW), col(128, C_SM)]
        + [page(sb, p) for sb in range(SN) for p in range(n_pages)]
        + [pl.BlockSpec((SN * wl, HEAD_DIM), lambda i, pt: (layer * nsteps + i, 0)),
           full(w4.shape), full(bias_c.shape), full(bias_w.shape)]
        + [pl.BlockSpec(memory_space=pl.ANY)] * len(prev_win),
        out_specs=(pl.BlockSpec((SN, 1, NSA_WIDTH), lambda i, pt: (i, 0, 0)),
                   pl.BlockSpec((SN, NSA_KV_HEADS, N_SELECT), lambda i, pt: (i, 0, 0)),
                   pl.BlockSpec((SN * wl, HEAD_DIM), lambda i, pt: (layer * nsteps + i, 0))),
        scratch_shapes=[pltpu.VMEM((SN, CMP_PAD * KV_ROWS, HEAD_DIM), F32)])
    return pl.pallas_call(
        functools.partial(_nsa_sample_a_kernel, n_pages, nb_past),
        out_shape=(jax.ShapeDtypeStruct((nbatch, 1, NSA_WIDTH), F32),
                   jax.ShapeDtypeStruct((nbatch, NSA_KV_HEADS, N_SELECT), I32),
                   jax.ShapeDtypeStruct(win_rows.shape, F32)),
        grid_spec=gs,
        input_output_aliases={1 + 4 + SN * n_pages + 4 + a: 2 for a in range(len(prev_win))},
        compiler_params=_params(("arbitrary",)),
        name="nsa_sample_a",
    )(page_table, proj_s3, proj_s3, proj_s3, proj_s3, *([pool_cmp] * (SN * n_pages)), win_rows, w4, bias_c, bias_w,
      *prev_win)


def _nsa_sample_b_kernel(nb_past, pt_ref, ix_ref, q_ref, kvs_ref, sm_ref, part_ref, *rest):
    nsel = NSA_KV_HEADS * N_SELECT
    blk_refs = rest[:SN * nsel]
    bs_ref, o_ref = rest[SN * nsel:]
    i0 = pl.program_id(0) * SN
    hg, dh = NSA_GROUP, HEAD_DIM
    blk_rows = CMP_BLOCK * KV_ROWS
    q16 = jnp.concatenate([_head_rows(q_ref, sb, 0, NSA_HEADS) for sb in range(SN)], axis=0)
    masked = jnp.full((hg, blk_rows), NEG, F32)
    keys, bias = [], []
    for sb in range(SN):
        new_blk = jnp.concatenate([_kv_rows(kvs_ref, sb), jnp.zeros((blk_rows - KV_ROWS, dh), F32)], axis=0)
        for g in range(NSA_KV_HEADS):
            owner = sb * NSA_KV_HEADS + g
            for i in range(N_SELECT):
                idx = ix_ref[i0 + sb, g * N_SELECT + i]
                past_blk = blk_refs[sb * nsel + g * N_SELECT + i][...]
                keys.append(jnp.where(idx >= nb_past, new_blk, past_blk).astype(BF16))
                bt = bs_ref[jnp.clip(idx, 0, nb_past), g * hg:(g + 1) * hg, :]
                bias.append(jnp.concatenate([bt if rg == owner else masked
                                             for rg in range(SN * NSA_KV_HEADS)], axis=0))
    k_all = jnp.concatenate(keys, axis=0)
    p = _masked_softmax_rows(_bdot(q16, k_all, _NT) + jnp.concatenate(bias, axis=1))
    o_s = _bdot(_value_weights(p), k_all)
    for sb in range(SN):
        gates = _sigmoid(sm_ref[sb, :, SM_GATE:SM_GATE + 3 * NSA_HEADS])
        for h in range(NSA_HEADS):
            cs = slice(h * dh, (h + 1) * dh)
            r = sb * NSA_HEADS + h
            o_ref[sb, :, cs] = part_ref[sb, :, cs] + o_s[r:r + 1] * gates[:, 3 * h + 1:3 * h + 2]


def _nsa_sample_b(page_table, top_idx, proj_s3, part, pool_sel, bias_s, layer, n_phys):
    nbatch, n_pages = page_table.shape
    nb_past = n_pages * PAGE_SIZE // CMP_BLOCK
    bpp = PAGE_SIZE // CMP_BLOCK
    nsel = NSA_KV_HEADS * N_SELECT
    col = lambda w, c: pl.BlockSpec((SN, 1, w), lambda i, pt, ix: (i, 0, c // w))

    def blk(sb, gi):
        def imap(i, pt, ix):
            b = SN * i + sb
            ip = jnp.clip(ix[b, gi], 0, nb_past - 1)
            return ((layer * n_phys + pt[b, ip // bpp]) * bpp + ip % bpp, 0)
        return pl.BlockSpec((CMP_BLOCK * KV_ROWS, HEAD_DIM), imap)

    gs = pltpu.PrefetchScalarGridSpec(
        num_scalar_prefetch=2, grid=(nbatch // SN,),
        in_specs=[col(NSA_WIDTH, C_NQ), col(KVW, C_KVS), col(128, C_SM),
                  pl.BlockSpec((SN, 1, NSA_WIDTH), lambda i, pt, ix: (i, 0, 0))]
        + [blk(sb, gi) for sb in range(SN) for gi in range(nsel)]
        + [pl.BlockSpec(bias_s.shape, lambda i, pt, ix: (0, 0, 0))],
        out_specs=pl.BlockSpec((SN, 1, NSA_WIDTH), lambda i, pt, ix: (i, 0, 0)))
    return pl.pallas_call(
        functools.partial(_nsa_sample_b_kernel, nb_past),
        out_shape=jax.ShapeDtypeStruct((nbatch, 1, NSA_WIDTH), F32),
        grid_spec=gs,
        compiler_params=_params(("arbitrary",)),
        name="nsa_sample_b",
    )(page_table, top_idx, proj_s3, proj_s3, proj_s3, part, *([pool_sel] * (SN * nsel)), bias_s)


SB = 8


def _rec_sample_kernel(gq_ref, gk_ref, gv_ref, gg_ref, x_ref, dz_ref, sm_ref, sg_ref, sd_ref, cb_ref,
                       w2_ref, b2_ref, gnw_ref, cw_ref, al_ref, dt_ref, dnw_ref, *rest):
    o_ref, sgo_ref, sdo_ref = rest[-3:]
    sm = sm_ref[...]
    ri = lax.broadcasted_iota(I32, (SB, 128), 0)
    log_a = _log_sigmoid(_bdot(sm[:, SM_GLR:SM_GLR + GLA_GATE_RANK], w2_ref[...]) + b2_ref[...]) / GLA_GATE_NORM
    ea_t = jnp.exp(log_a).T
    k_t = gk_ref[...].T
    q_t = (gq_ref[...] * (GLA_DK ** -0.5)).T
    gv = gv_ref[...]
    for h in range(GLA_HEADS):
        hs = slice(h * GLA_DK, (h + 1) * GLA_DK)
        vs = slice(h * GLA_DV, (h + 1) * GLA_DV)
        o_h = jnp.zeros((SB, GLA_DV), F32)
        for i in range(SB):
            s = ea_t[hs, i:i + 1] * sg_ref[i, h] + k_t[hs, i:i + 1] * gv[i:i + 1, vs]
            sgo_ref[i, h] = s
            o = jnp.sum(q_t[hs, i:i + 1] * s, axis=0, keepdims=True)
            o_h = jnp.where(ri == i, o, o_h)
        o_ref[:, vs] = _rms(o_h, gnw_ref[...]) * _silu(gg_ref[:, vs])
    y = x_ref[...] * cw_ref[CONV_W - 1:CONV_W, :]
    for jw in range(CONV_W - 1):
        y = y + cb_ref[jw] * cw_ref[jw:jw + 1, :]
    y = _silu(y)
    beta = _sigmoid(sm[:, SM_DB:SM_DB + GDN_HEADS])
    eg = jnp.exp(-jnp.exp(al_ref[...]) * _softplus(sm[:, SM_DA:SM_DA + GDN_HEADS] + dt_ref[...]))
    dk, dv = GDN_DK, GDN_DV
    for h in range(GDN_HEADS):
        cq = y[:, h * dk:(h + 1) * dk]
        ck = y[:, GDN_WIDTH + h * dk:GDN_WIDTH + (h + 1) * dk]
        v = y[:, 2 * GDN_WIDTH + h * dv:2 * GDN_WIDTH + (h + 1) * dv]
        q_t = (cq * lax.rsqrt(jnp.sum(cq * cq, axis=-1, keepdims=True) + 1e-6) * (dk ** -0.5)).T
        k_t = (ck * lax.rsqrt(jnp.sum(ck * ck, axis=-1, keepdims=True) + 1e-6)).T
        o_h = jnp.zeros((SB, dv), F32)
        for i in range(SB):
            s = eg[i:i + 1, h:h + 1] * sd_ref[i, h]
            kc = k_t[:, i:i + 1]
            delta = (v[i:i + 1] - jnp.sum(kc * s, axis=0, keepdims=True)) * beta[i:i + 1, h:h + 1]
            s = s + kc * delta
            sdo_ref[i, h] = s
            o = jnp.sum(q_t[:, i:i + 1] * s, axis=0, keepdims=True)
            o_h = jnp.where(ri == i, o, o_h)
        vs = slice(GLA_WIDTH + h * dv, GLA_WIDTH + (h + 1) * dv)
        o_ref[:, vs] = _rms(o_h, dnw_ref[...]) * _silu(dz_ref[:, h * dv:(h + 1) * dv])


def _rec_sample(proj_s, state_gla, state_gdn, conv_t, w2, b2, gnw, cw, a_log, dt_bias, dnw, layer, prev_states):
    nbatch = proj_s.shape[0]
    nblk = nbatch // SB
    depth = state_gla.shape[0] // nbatch
    alias_specs = [pl.BlockSpec(memory_space=pl.ANY)] * len(prev_states)
    n_in = 17
    row = lambda w, col: pl.BlockSpec((SB, w), lambda i: (i, col // w))
    full = lambda shape: pl.BlockSpec(shape, lambda i: (0,) * len(shape))
    return pl.pallas_call(
        _rec_sample_kernel,
        out_shape=(jax.ShapeDtypeStruct((nbatch, GLA_WIDTH + GDN_WIDTH), F32),
                   jax.ShapeDtypeStruct(state_gla.shape, F32),
                   jax.ShapeDtypeStruct(state_gdn.shape, F32)),
        grid=(nblk,),
        in_specs=[row(GLA_HEADS * GLA_DK, C_GQ), row(GLA_HEADS * GLA_DK, C_GK), row(GLA_WIDTH, C_GV),
                  row(GLA_WIDTH, C_GG), row(3 * GDN_WIDTH, C_DQKV), row(GDN_WIDTH, C_DZ), row(128, C_SM),
                  pl.BlockSpec((SB, GLA_HEADS, GLA_DK, GLA_DV), lambda i: (layer * nblk + i, 0, 0, 0)),
                  pl.BlockSpec((SB, GDN_HEADS, GDN_DK, GDN_DV), lambda i: (layer * nblk + i, 0, 0, 0)),
                  pl.BlockSpec((CONV_W - 1, SB, 3 * GDN_WIDTH), lambda i: (0, i, 0)),
                  full(w2.shape), full(b2.shape), full(gnw.shape), full(cw.shape), full(a_log.shape),
                  full(dt_bias.shape), full(dnw.shape)] + alias_specs,
        out_specs=(pl.BlockSpec((SB, GLA_WIDTH + GDN_WIDTH), lambda i: (i, 0)),
                   pl.BlockSpec((SB, GLA_HEADS, GLA_DK, GLA_DV), lambda i: (layer * nblk + i, 0, 0, 0)),
                   pl.BlockSpec((SB, GDN_HEADS, GDN_DK, GDN_DV), lambda i: (layer * nblk + i, 0, 0, 0))),
        input_output_aliases={n_in + a: 1 + a for a in range(len(prev_states))},
        compiler_params=_params(("parallel",)),
        name="rec_sample",
    )(proj_s, proj_s, proj_s, proj_s, proj_s, proj_s, proj_s, state_gla, state_gdn, conv_t,
      w2, b2, gnw, cw, a_log, dt_bias, dnw, *prev_states)


def _reorder_w_in(w):
    return jnp.concatenate([
        w[..., 0:2560],
        w[..., 2584:3096],
        w[..., 4136:5672],
        w[..., 3096:3608],
        w[..., 3624:4136],
        w[..., 5672:6184],
        w[..., 2560:2584],
        w[..., 3608:3624],
        w[..., 6184:6192],
        jnp.zeros(w.shape[:-1] + (PROJ_N - 6192,), w.dtype)], axis=-1)


def _cmp_weight_tile(w_cmp):
    half = NSA_KV_HEADS * HEAD_DIM
    return jnp.concatenate([jnp.broadcast_to(w_cmp[:, 0:1], (CMP_BLOCK, half)),
                            jnp.broadcast_to(w_cmp[:, 1:2], (CMP_BLOCK, half))], axis=1).astype(F32)


def _cmp_weight_rows(w_cmp):
    w = jnp.repeat(w_cmp, NSA_KV_HEADS, axis=1).reshape(CMP_BLOCK * KV_ROWS, 1)
    return jnp.broadcast_to(w, (CMP_BLOCK * KV_ROWS, HEAD_DIM)).astype(F32)


def kernel(x_prompt, x_sample, cache_cmp, cache_sel, cache_win, state_gla, state_gdn, state_conv, page_table,
           p_prompt, p_sample, norm_w, ffn_w1, ffn_w3, ffn_w2, w_in, w_out, nsa_w_cmp, t5_bias,
           gla_w_gk2, gla_b_gk, gla_norm_w, gdn_conv_w, gdn_a_log, gdn_dt_bias, gdn_norm_w,
           ple_w_proj, ple_w_gate):
    depth = w_in.shape[0]
    batch, seq, d = x_prompt.shape
    nbatch = x_sample.shape[0]
    n_phys = cache_cmp.shape[1]
    n_pages = page_table.shape[1]
    past = n_pages * PAGE_SIZE
    win_len = cache_win.shape[2]
    rows_p = batch * seq
    kv_shape = (2, NSA_KV_HEADS, HEAD_DIM)

    x = jnp.concatenate([x_prompt.reshape(rows_p, d), x_sample.reshape(nbatch, d)], axis=0)
    assert win_len == WINDOW and past // CMP_BLOCK + 1 >= N_SELECT and past // CMP_BLOCK < CMP_PAD
    assert seq >= WINDOW
    pool_cmp = cache_cmp.reshape(-1, HEAD_DIM)
    pool_sel = cache_sel.reshape(-1, HEAD_DIM)
    win_all = cache_win.reshape(-1, HEAD_DIM)
    sgla_all = state_gla.reshape((depth * nbatch,) + state_gla.shape[2:])
    sgdn_all = state_gdn.reshape((depth * nbatch,) + state_gdn.shape[2:])

    pb_c, pb_s, pb_w = _prompt_bias_tables(t5_bias, seq)
    sb_c, sb_s, sb_w = _sample_bias_tables(t5_bias, past, win_len)
    nb = seq // CMP_BLOCK
    expand = jnp.asarray(-MASK_BIG * np.repeat(np.eye(nb, dtype=np.float32), CMP_BLOCK, axis=1)
                         .reshape(nb, seq // KEY_CHUNK, KEY_CHUNK).transpose(1, 0, 2), BF16)

    w1_all, w3_all, w2_all = ffn_w1.astype(BF16), ffn_w3.astype(BF16), ffn_w2.astype(BF16)
    w_in_all = _reorder_w_in(w_in.astype(BF16))
    w_out_all, wg_all, wp_all = w_out.astype(BF16), ple_w_gate.astype(BF16), ple_w_proj.astype(BF16)

    outs = {k: [] for k in ("cmp_s", "sel_s", "gla_p", "gdn_p", "conv_p", "conv_s")}
    prev_win, prev_states, prev_rows = (), (), ()
    for l in range(depth):
        nw = norm_w[l].reshape(-1, 1, d)
        x = _ffn(x, nw[0], w1_all, w3_all, w2_all, nw[1], l, 0)

        proj = _proj(x, nw[2], w_in_all, l)
        proj_s = proj[rows_p:]
        proj_s3 = proj_s.reshape(nbatch, 1, PROJ_N)
        wtile = _cmp_weight_tile(nsa_w_cmp[l])
        w_gk2 = gla_w_gk2[l]
        b_gk = gla_b_gk[l].reshape(1, -1)
        gla_nw = gla_norm_w[l].reshape(1, -1)
        gdn_nw = gdn_norm_w[l].reshape(1, -1)
        conv_w = gdn_conv_w[l]
        a_log = gdn_a_log[l].reshape(1, -1)
        dt_bias = gdn_dt_bias[l].reshape(1, -1)

        cmpkv = _compress_prompt(proj, wtile, rows_p)
        rows_out = _cache_rows(proj, batch, seq, depth, l, prev_rows)
        prev_rows, (kvs16, kvw16) = rows_out[:3], rows_out[3:]
        y = _nsa_prompt(proj, cmpkv, kvs16, kvw16, pb_c, pb_s, pb_w, expand, batch, seq)
        y, s_gla_p = _gla_prompt(proj, w_gk2, b_gk, gla_nw, batch, seq, y)
        y, s_gdn_p = _gdn_prompt(proj, conv_w, a_log, dt_bias, gdn_nw, batch, seq, y)

        part, top_idx, win_new = _nsa_sample_a(page_table, proj_s3, pool_cmp, win_all, _cmp_weight_rows(nsa_w_cmp[l]),
                                               sb_c, sb_w, l, n_phys, prev_win)
        prev_win = (win_new,)
        o_nsa_s = _nsa_sample_b(page_table, top_idx.reshape(nbatch, NSA_KV_HEADS * N_SELECT), proj_s3, part,
                                pool_sel, sb_s, l, n_phys)
        conv_t = jnp.swapaxes(state_conv[l], 0, 1)
        o_rec_s, s_gla_s, s_gdn_s = _rec_sample(proj_s, sgla_all, sgdn_all, conv_t, w_gk2, b_gk, gla_nw,
                                                conv_w, a_log, dt_bias, gdn_nw, l, prev_states)
        prev_states = (s_gla_s, s_gdn_s)

        y = lax.dynamic_update_slice(
            y, jnp.concatenate([o_nsa_s.reshape(nbatch, NSA_WIDTH), o_rec_s], axis=1), (rows_p, 0))
        x = _outproj(y, x, w_out_all, nw[3], l)
        x = _ffn(x, nw[4], w1_all, w3_all, w2_all, nw[5], l, 1)
        p = jnp.concatenate([p_prompt[l].reshape(rows_p, -1), p_sample[l].reshape(nbatch, -1)], axis=0)
        x = _ple(x, p, nw[6], wg_all, wp_all, nw[7], l, split_rows=rows_p if l == depth - 1 else None)

        outs["cmp_s"].append(proj_s[:, C_KVC:C_KVC + KVW].reshape((nbatch, 1) + kv_shape))
        outs["sel_s"].append(proj_s[:, C_KVS:C_KVS + KVW].reshape((nbatch, 1) + kv_shape))
        outs["gla_p"].append(s_gla_p)
        outs["gdn_p"].append(s_gdn_p)
        outs["conv_p"].append(jnp.stack([proj[(b + 1) * seq - (CONV_W - 1):(b + 1) * seq, C_DQKV:C_DQKV + 3 * GDN_WIDTH]
                                         for b in range(batch)]))
        outs["conv_s"].append(jnp.concatenate(
            [state_conv[l][:, 1:], proj_s[:, None, C_DQKV:C_DQKV + 3 * GDN_WIDTH]], axis=1))

    st = lambda k: jnp.stack(outs[k])
    return (x[0].reshape(batch, seq, d), x[1].reshape(nbatch, 1, d),
            prev_rows[0].reshape((depth, batch, seq) + kv_shape), st("cmp_s"),
            prev_rows[1].reshape((depth, batch, seq) + kv_shape), st("sel_s"),
            prev_rows[2].reshape((depth, batch, WINDOW) + kv_shape), win_new.reshape(cache_win.shape),
            st("gla_p"), s_gla_s.reshape(state_gla.shape), st("gdn_p"), s_gdn_s.reshape(state_gdn.shape),
            st("conv_p"), st("conv_s"))
```

```python
import functools
import math

import numpy as np
import jax
import jax.numpy as jnp
from jax import lax
from jax.experimental import pallas as pl
from jax.experimental.pallas import tpu as pltpu

F32 = jnp.float32
BF16 = jnp.bfloat16
I32 = jnp.int32

D_MODEL = 2048
HEAD_DIM = 128
NSA_HEADS = 8
NSA_KV_HEADS = 2
NSA_GROUP = NSA_HEADS // NSA_KV_HEADS
CMP_BLOCK = 64
N_SELECT = 8
WINDOW = 512
Q_BLOCK = 128
SEL_FORCE = 1.0e4
GLA_HEADS = 4
GLA_DK = 64
GLA_DV = 128
GLA_GATE_RANK = 16
GLA_GATE_NORM = 16.0
GDN_HEADS = 4
GDN_DK = 128
GDN_DV = 128
CHUNK = 64
CONV_W = 4
N_BUCKETS = 32
T5_MAX_DIST = 128
D_FF = 5632
PLE_DIM = 256
RMS_EPS = 1e-6
NEG = -1e30
MASK_BIG = 2.0 ** 100
PAGE_SIZE = 128

NSA_WIDTH = NSA_HEADS * HEAD_DIM
KVW = 2 * NSA_KV_HEADS * HEAD_DIM
KV_ROWS = 2 * NSA_KV_HEADS
CMP_PAD = 64
GDN_WIDTH = GDN_HEADS * GDN_DV
GLA_WIDTH = GLA_HEADS * GLA_DV
MIX_WIDTH = NSA_WIDTH + GLA_WIDTH + GDN_WIDTH

C_NQ = 0
C_KVC = 1024
C_KVS = 1536
C_KVW = 2048
C_GQ = 2560
C_GK = 2816
C_DQKV = 3072
C_GV = 4608
C_GG = 5120
C_DZ = 5632
C_SM = 6144
PROJ_N = 6272
SM_GATE, SM_GLR, SM_DB, SM_DA = 0, 24, 40, 44

VMEM_LIMIT = 56 * 1024 * 1024
TM = 640
TF = 512
TN_PROJ = 896
TB = 256
TB_GDN = 512


def _sigmoid(x):
    return 1.0 / (1.0 + jnp.exp(-x))


def _silu(x):
    return x * _sigmoid(x)


def _softplus(x):
    return jnp.maximum(x, 0.0) + jnp.log(1.0 + jnp.exp(-jnp.abs(x)))


def _log_sigmoid(x):
    return jnp.minimum(x, 0.0) - jnp.log(1.0 + jnp.exp(-jnp.abs(x)))


def _rms(x, w):
    return x * lax.rsqrt(jnp.mean(x * x, axis=-1, keepdims=True) + RMS_EPS) * w


_NN = (((1,), (0,)), ((), ()))
_NT = (((1,), (1,)), ((), ()))
_TN = (((0,), (0,)), ((), ()))


def _bdot(a, b, dims=_NN):
    return lax.dot_general(a.astype(BF16), b.astype(BF16), dims, preferred_element_type=F32)


def _hdot(a, b, dims=_NN):
    return lax.dot_general(a, b, dims, preferred_element_type=F32, precision=lax.Precision.HIGHEST)


def _params(sem):
    return pltpu.CompilerParams(dimension_semantics=sem, vmem_limit_bytes=VMEM_LIMIT)


def _ffn_kernel(x_ref, nwa_ref, w1_ref, w3_ref, w2_ref, nwb_ref, o_ref, h_ref, acc_ref):
    f = pl.program_id(1)

    @pl.when(f == 0)
    def _():
        h_ref[...] = _rms(x_ref[...], nwa_ref[...]).astype(BF16)
        acc_ref[...] = jnp.zeros_like(acc_ref)

    h = h_ref[...]
    a = jnp.dot(h, w1_ref[...], preferred_element_type=F32)
    b = jnp.dot(h, w3_ref[...], preferred_element_type=F32)
    g = (_silu(a) * b).astype(BF16)
    acc_ref[...] += jnp.dot(g, w2_ref[...], preferred_element_type=F32)

    @pl.when(f == pl.num_programs(1) - 1)
    def _():
        o_ref[...] = x_ref[...] + 0.5 * _rms(acc_ref[...], nwb_ref[...])


def _ffn(x, nwa, w1, w3, w2, nwb, layer, which):
    m, d = x.shape
    dff = w1.shape[-1]
    row = lambda i, f: (i, 0)
    return pl.pallas_call(
        _ffn_kernel,
        out_shape=jax.ShapeDtypeStruct((m, d), F32),
        grid=(m // TM, dff // TF),
        in_specs=[pl.BlockSpec((TM, d), row),
                  pl.BlockSpec((1, d), lambda i, f: (0, 0)),
                  pl.BlockSpec((None, None, d, TF), lambda i, f: (layer, which, 0, f)),
                  pl.BlockSpec((None, None, d, TF), lambda i, f: (layer, which, 0, f)),
                  pl.BlockSpec((None, None, TF, d), lambda i, f: (layer, which, f, 0)),
                  pl.BlockSpec((1, d), lambda i, f: (0, 0))],
        out_specs=pl.BlockSpec((TM, d), row),
        scratch_shapes=[pltpu.VMEM((TM, d), BF16), pltpu.VMEM((TM, d), F32)],
        compiler_params=_params(("parallel", "arbitrary")),
        name="ffn",
    )(x, nwa, w1, w3, w2, nwb)


def _proj_kernel(x_ref, nw_ref, w_ref, o_ref, h_ref):
    @pl.when(pl.program_id(1) == 0)
    def _():
        h_ref[...] = _rms(x_ref[...], nw_ref[...]).astype(BF16)

    o_ref[...] = jnp.dot(h_ref[...], w_ref[...], preferred_element_type=F32)


def _proj(x, nw, w, layer):
    m, d = x.shape
    n = w.shape[-1]
    return pl.pallas_call(
        _proj_kernel,
        out_shape=jax.ShapeDtypeStruct((m, n), F32),
        grid=(m // TM, n // TN_PROJ),
        in_specs=[pl.BlockSpec((TM, d), lambda i, j: (i, 0)),
                  pl.BlockSpec((1, d), lambda i, j: (0, 0)),
                  pl.BlockSpec((None, d, TN_PROJ), lambda i, j: (layer, 0, j))],
        out_specs=pl.BlockSpec((TM, TN_PROJ), lambda i, j: (i, j)),
        scratch_shapes=[pltpu.VMEM((TM, d), BF16)],
        compiler_params=_params(("parallel", "arbitrary")),
        name="proj",
    )(x, nw, w)


def _outproj_kernel(y_ref, x_ref, w_ref, nw_ref, o_ref):
    z = jnp.dot(y_ref[...].astype(BF16), w_ref[...], preferred_element_type=F32)
    o_ref[...] = x_ref[...] + _rms(z, nw_ref[...])


def _outproj(y, x, w, nw, layer):
    m, d = x.shape
    k = y.shape[1]
    return pl.pallas_call(
        _outproj_kernel,
        out_shape=jax.ShapeDtypeStruct((m, d), F32),
        grid=(m // TM,),
        in_specs=[pl.BlockSpec((TM, k), lambda i: (i, 0)),
                  pl.BlockSpec((TM, d), lambda i: (i, 0)),
                  pl.BlockSpec((None, k, d), lambda i: (layer, 0, 0)),
                  pl.BlockSpec((1, d), lambda i: (0, 0))],
        out_specs=pl.BlockSpec((TM, d), lambda i: (i, 0)),
        compiler_params=_params(("parallel",)),
        name="outproj",
    )(y, x, w, nw)


def _ple_kernel(split, x_ref, p_ref, nwa_ref, wg_ref, wp_ref, nwb_ref, *o_refs):
    x = x_ref[...]
    gate = _sigmoid(jnp.dot(_rms(x, nwa_ref[...]).astype(BF16), wg_ref[...], preferred_element_type=F32))
    pp = jnp.dot(p_ref[...].astype(BF16), wp_ref[...], preferred_element_type=F32)
    y = x + _rms(gate * pp, nwb_ref[...])
    o_refs[0][...] = y
    if split is not None:
        @pl.when(pl.program_id(0) == pl.num_programs(0) - 1)
        def _():
            o_refs[1][...] = y[split:]


def _ple(x, p, nwa, wg, wp, nwb, layer, split_rows=None):
    m, d = x.shape
    pd = p.shape[1]
    nblk = m // TM
    out_shape = jax.ShapeDtypeStruct((m, d), F32)
    out_specs = pl.BlockSpec((TM, d), lambda i: (i, 0))
    split = None
    if split_rows is not None:
        split = split_rows - (nblk - 1) * TM
        assert 0 < split and m - split_rows == TM - split
        out_shape = (jax.ShapeDtypeStruct((split_rows, d), F32), jax.ShapeDtypeStruct((m - split_rows, d), F32))
        out_specs = (out_specs, pl.BlockSpec((m - split_rows, d), lambda i: (0, 0)))
    return pl.pallas_call(
        functools.partial(_ple_kernel, split),
        out_shape=out_shape,
        grid=(nblk,),
        in_specs=[pl.BlockSpec((TM, d), lambda i: (i, 0)),
                  pl.BlockSpec((TM, pd), lambda i: (i, 0)),
                  pl.BlockSpec((1, d), lambda i: (0, 0)),
                  pl.BlockSpec((None, d, d), lambda i: (layer, 0, 0)),
                  pl.BlockSpec((None, pd, d), lambda i: (layer, 0, 0)),
                  pl.BlockSpec((1, d), lambda i: (0, 0))],
        out_specs=out_specs,
        compiler_params=_params(("arbitrary",)),
        name="ple",
    )(x, p, nwa, wg, wp, nwb)


def _t5_bucket_np(dist):
    n = np.maximum(dist, 0)
    exact = N_BUCKETS // 2
    val = (np.log(np.maximum(n, 1).astype(np.float32) / np.float32(exact))
           / np.float32(math.log(T5_MAX_DIST / exact)) * np.float32(N_BUCKETS - exact))
    large = exact + val.astype(np.int32)
    return np.where(n < exact, n, np.minimum(large, N_BUCKETS - 1)).astype(np.int32)


def _bias_table(t5_bias, dist, valid):
    bucket = jnp.asarray(np.where(valid, _t5_bucket_np(dist), -1).astype(np.int32))[None]
    hshape = (t5_bias.shape[1],) + (1,) * dist.ndim
    out = jnp.full((t5_bias.shape[1],) + dist.shape, NEG, F32)
    for k in range(N_BUCKETS):
        out = jnp.where(bucket == k, t5_bias[k].reshape(hshape), out)
    return out


def _prompt_bias_tables(t5_bias, seq):
    nb = seq // CMP_BLOCK
    qpos = np.arange(seq)[:, None]
    d_c = qpos - (np.arange(nb) * CMP_BLOCK + CMP_BLOCK - 1)[None, :]
    bias_c = _bias_table(t5_bias, d_c, d_c >= 0)
    i = np.arange(Q_BLOCK)[:, None]
    jj = np.arange(Q_BLOCK)[None, :]
    d_s = np.stack([Q_BLOCK * dl + i - jj for dl in range(3)] + [i - jj - Q_BLOCK])
    bias_s = jnp.transpose(_bias_table(t5_bias, d_s, d_s >= 0), (1, 0, 2, 3))
    nwc = WINDOW // Q_BLOCK + 1
    d_w = np.stack([i + WINDOW - Q_BLOCK * cw - jj for cw in range(nwc)] + [i - jj - Q_BLOCK])
    bias_w = jnp.transpose(_bias_table(t5_bias, d_w, (d_w >= 0) & (d_w < WINDOW)), (1, 0, 2, 3))
    return bias_c, bias_s, bias_w


def _sample_bias_tables(t5_bias, past, win_len):
    nb_past = past // CMP_BLOCK
    head_g = (np.arange(NSA_HEADS) // NSA_GROUP)[:, None]

    def per_head(dist, valid):
        lane_cg = np.arange(dist.shape[-1]) % KV_ROWS
        tbl = _bias_table(t5_bias, dist, valid)
        own = jnp.asarray(lane_cg[None] == head_g)
        return tbl, own

    lane = np.arange(CMP_PAD * KV_ROWS)
    n = lane // KV_ROWS
    d_c = past - (n * CMP_BLOCK + CMP_BLOCK - 1)
    tbl, own = per_head(d_c, (d_c >= 0) & (n <= nb_past))
    bias_c = jnp.where(own, tbl, NEG)
    lane = np.arange(CMP_BLOCK * KV_ROWS)
    blk = np.arange(nb_past + 1)[:, None]
    d_s = past - (blk * CMP_BLOCK + (lane // KV_ROWS)[None, :])
    tbl, own = per_head(d_s, d_s >= 0)
    bias_s = jnp.transpose(jnp.where(own[:, None, :], tbl, NEG), (1, 0, 2))
    lane = np.arange(win_len * KV_ROWS)
    wpos = past - win_len + 1 + lane // KV_ROWS
    d_w = past - wpos
    tbl, own = per_head(d_w, (d_w >= 0) & (d_w < WINDOW) & (wpos >= 0))
    bias_w = jnp.where(own, tbl, NEG)
    return bias_c, bias_s, bias_w


def _compress_kernel(x_ref, w_ref, o_ref):
    r = x_ref.shape[0] // CMP_BLOCK
    x = x_ref[...].reshape(r, CMP_BLOCK, KVW)
    o_ref[...] = jnp.sum(x * w_ref[...][None], axis=1)


def _compress_prompt(proj, wtile, rows):
    rb = 512
    return pl.pallas_call(
        _compress_kernel,
        out_shape=jax.ShapeDtypeStruct((rows // CMP_BLOCK, KVW), F32),
        grid=(rows // rb,),
        in_specs=[pl.BlockSpec((rb, KVW), lambda i: (i, C_KVC // KVW)),
                  pl.BlockSpec((CMP_BLOCK, KVW), lambda i: (0, 0))],
        out_specs=pl.BlockSpec((rb // CMP_BLOCK, KVW), lambda i: (i, 0)),
        compiler_params=_params(("parallel",)),
        name="compress_prompt",
    )(proj, wtile)


KEY_CHUNK = 2 * Q_BLOCK
UNROLL = 2


def _nsa_prompt_kernel(q_ref, sm_ref, cmp_ref, kvs_ref, kvw_ref, bc_ref, bs_ref, bw_ref, e_ref, o_ref,
                       s_ref, mx_ref, l_ref, acc_ref):
    j = pl.program_id(1)
    nb = cmp_ref.shape[0]
    hg, qb, dh, kc = NSA_GROUP, Q_BLOCK, HEAD_DIM, KEY_CHUNK
    rows = hg * qb
    nwt = bw_ref.shape[0] - 1
    gates = _sigmoid(sm_ref[:, SM_GATE:SM_GATE + 3 * NSA_HEADS])
    qi = lax.broadcasted_iota(I32, (qb, nb), 0)
    ni = lax.broadcasted_iota(I32, (qb, nb), 1)
    cur = (qb // CMP_BLOCK) * j + qi // CMP_BLOCK
    forced = (ni == cur) | (ni == cur - 1) | (ni == 0)
    started = ni <= cur
    groups = range(NSA_KV_HEADS)
    kcols = [slice(g * dh, (g + 1) * dh) for g in groups]
    vcols = [slice((NSA_KV_HEADS + g) * dh, (NSA_KV_HEADS + g + 1) * dh) for g in groups]
    head_rows = [slice(g * hg, (g + 1) * hg) for g in groups]

    def fold(x, op):
        out = x[:, 0:qb]
        for t in range(1, x.shape[1] // qb):
            out = op(out, x[:, t * qb:(t + 1) * qb])
        return out

    q16, o_c, score = [], [], []
    for g in groups:
        q = jnp.concatenate([q_ref[:, (g * hg + h) * dh:(g * hg + h + 1) * dh] for h in range(hg)], axis=0)
        q16.append((q * (dh ** -0.5)).astype(BF16))
        bias = bc_ref[head_rows[g]].reshape(rows, nb)
        valid = bias > -1e29
        s = jnp.where(valid, _bdot(q16[g], cmp_ref[:, kcols[g]], _NT) + bias, NEG)
        p = jnp.where(valid, jnp.exp(s - jnp.max(s, axis=-1, keepdims=True)), 0.0)
        p = p / jnp.maximum(jnp.sum(p, axis=-1, keepdims=True), 1e-30)
        o_c.append(_bdot(p, cmp_ref[:, vcols[g]]))
        imp = p[0:qb]
        for h in range(1, hg):
            imp = imp + p[h * qb:(h + 1) * qb]
        score.append(jnp.where(started, jnp.where(forced, SEL_FORCE, imp), -1.0))

    unsel = [jnp.ones((qb, nb), F32) for _ in groups]
    for _ in range(min(N_SELECT, nb)):
        for g in groups:
            hit = ni == jnp.argmax(score[g], axis=-1, keepdims=True).astype(I32)
            unsel[g] = jnp.where(hit & started, 0.0, unsel[g])
            score[g] = jnp.where(hit, -3e38, score[g])
    unsel = [u.astype(BF16) for u in unsel]

    mx_ref[...] = jnp.full(mx_ref.shape, NEG, F32)
    l_ref[...] = jnp.zeros(l_ref.shape, F32)
    acc_ref[...] = jnp.zeros(acc_ref.shape, F32)
    n_chunks = j // (kc // qb) + 1

    def key_rows(c):
        return pl.ds(pl.multiple_of(c * kc, kc), kc)

    def sel_bias(g, c):
        tiles = []
        for t in range(kc // qb):
            back = j - (c * (kc // qb) + t)
            tiles.append(bs_ref[jnp.where(back < 0, 3, jnp.minimum(back, 2)), head_rows[g]].reshape(rows, qb))
        return jnp.concatenate(tiles, axis=1)

    def pass1_chunk(c):
        for g in groups:
            key_mask = jnp.dot(unsel[g], e_ref[c], preferred_element_type=F32)
            s_ = (_bdot(q16[g], kvs_ref[key_rows(c), kcols[g]], _NT) + sel_bias(g, c)
                  + jnp.concatenate([key_mask] * hg, axis=0))
            s_ref[g, c] = s_
            mx_ref[g] = jnp.maximum(mx_ref[g], fold(s_, jnp.maximum))

    def unrolled(chunk_fn):
        def body(t, carry):
            for u in range(UNROLL):
                chunk_fn(t * UNROLL + u)
            return carry
        lax.fori_loop(0, (n_chunks + UNROLL - 1) // UNROLL, body, 0)

    unrolled(pass1_chunk)
    for g in groups:
        mx_ref[g] = jnp.broadcast_to(jnp.max(mx_ref[g], axis=-1, keepdims=True), mx_ref.shape[1:])

    def pass2_chunk(c):
        for g in groups:
            p_ = jnp.exp(s_ref[g, c] - jnp.concatenate([mx_ref[g]] * (kc // qb), axis=1))
            l_ref[g] += fold(p_, jnp.add)
            acc_ref[g] += _bdot(p_, kvs_ref[key_rows(c), vcols[g]])

    unrolled(pass2_chunk)
    o_s = [acc_ref[g] / jnp.maximum(jnp.sum(l_ref[g], axis=-1, keepdims=True), 1e-30) for g in groups]

    lead = jnp.maximum(nwt - 1 - j, 0)
    w_rows = pl.ds(pl.multiple_of(jnp.maximum(j - (nwt - 1), 0) * qb, qb), nwt * qb)
    o_w = []
    for g in groups:
        bias = jnp.concatenate(
            [bw_ref[jnp.minimum(t + lead, nwt), head_rows[g]].reshape(rows, qb) for t in range(nwt)], axis=1)
        s = _bdot(q16[g], kvw_ref[w_rows, kcols[g]], _NT) + bias
        p = jnp.exp(s - jnp.max(fold(s, jnp.maximum), axis=-1, keepdims=True))
        den = jnp.sum(fold(p, jnp.add), axis=-1, keepdims=True)
        o_w.append(_bdot(p, kvw_ref[w_rows, vcols[g]]) / jnp.maximum(den, 1e-30))

    for g in groups:
        for h in range(hg):
            c0 = SM_GATE + (g * hg + h) * 3
            rs = slice(h * qb, (h + 1) * qb)
            o_ref[:, (g * hg + h) * dh:(g * hg + h + 1) * dh] = (
                o_c[g][rs] * gates[:, c0:c0 + 1] + o_s[g][rs] * gates[:, c0 + 1:c0 + 2]
                + o_w[g][rs] * gates[:, c0 + 2:c0 + 3])


def _nsa_prompt(proj, cmpkv, kvs16, kvw16, bias_c, bias_s, bias_w, expand, batch, seq):
    nqb = seq // Q_BLOCK
    nb = seq // CMP_BLOCK
    rows = NSA_GROUP * Q_BLOCK
    full = lambda shape: pl.BlockSpec(shape, lambda b, j: (0,) * len(shape))
    return pl.pallas_call(
        _nsa_prompt_kernel,
        out_shape=jax.ShapeDtypeStruct((proj.shape[0], MIX_WIDTH), F32),
        grid=(batch, nqb),
        in_specs=[pl.BlockSpec((Q_BLOCK, NSA_WIDTH), lambda b, j: (b * nqb + j, 0)),
                  pl.BlockSpec((Q_BLOCK, 128), lambda b, j: (b * nqb + j, C_SM // 128)),
                  pl.BlockSpec((nb, KVW), lambda b, j: (b, 0)),
                  pl.BlockSpec((seq, KVW), lambda b, j: (b, 0)),
                  pl.BlockSpec((seq, KVW), lambda b, j: (b, 0)),
                  pl.BlockSpec((NSA_HEADS, Q_BLOCK, nb), lambda b, j: (0, j, 0)),
                  full(bias_s.shape), full(bias_w.shape), full(expand.shape)],
        out_specs=pl.BlockSpec((Q_BLOCK, NSA_WIDTH), lambda b, j: (b * nqb + j, 0)),
        scratch_shapes=[pltpu.VMEM((NSA_KV_HEADS, seq // KEY_CHUNK, rows, KEY_CHUNK), F32),
                        pltpu.VMEM((NSA_KV_HEADS, rows, Q_BLOCK), F32), pltpu.VMEM((NSA_KV_HEADS, rows, Q_BLOCK), F32),
                        pltpu.VMEM((NSA_KV_HEADS, rows, HEAD_DIM), F32)],
        compiler_params=_params(("parallel", "arbitrary")),
        name="nsa_prompt",
    )(proj, proj, cmpkv, kvs16, kvw16, bias_c, bias_s, bias_w, expand)


ROWS_BLK = 512


def _cache_rows_kernel(per_b, kvc_ref, kvs_ref, kvw_ref, *rest):
    oc_ref, os_ref, ow_ref, s16_ref, w16_ref = rest[-5:]
    n = kvc_ref.shape[0]
    dh = HEAD_DIM

    def put(o_ref, x_ref):
        for cg in range(KV_ROWS):
            o_ref[pl.ds(cg, n, stride=KV_ROWS), :] = x_ref[:, cg * dh:(cg + 1) * dh]

    put(oc_ref, kvc_ref)
    put(os_ref, kvs_ref)
    s16_ref[...] = kvs_ref[...].astype(BF16)
    w16_ref[...] = kvw_ref[...].astype(BF16)

    @pl.when(pl.program_id(0) % per_b == per_b - 1)
    def _():
        put(ow_ref, kvw_ref)


def _cache_rows(proj, batch, seq, depth, layer, prev):
    assert seq % ROWS_BLK == 0 and WINDOW == ROWS_BLK
    per_b = seq // ROWS_BLK
    nblk = batch * per_b
    blk = lambda c: pl.BlockSpec((ROWS_BLK, KVW), lambda i: (i, c // KVW))
    out_rows = ROWS_BLK * KV_ROWS
    return pl.pallas_call(
        functools.partial(_cache_rows_kernel, per_b),
        out_shape=(jax.ShapeDtypeStruct((depth * batch * seq * KV_ROWS, HEAD_DIM), F32),
                   jax.ShapeDtypeStruct((depth * batch * seq * KV_ROWS, HEAD_DIM), F32),
                   jax.ShapeDtypeStruct((depth * batch * WINDOW * KV_ROWS, HEAD_DIM), F32),
                   jax.ShapeDtypeStruct((batch * seq, KVW), BF16),
                   jax.ShapeDtypeStruct((batch * seq, KVW), BF16)),
        grid=(nblk,),
        in_specs=[blk(C_KVC), blk(C_KVS), blk(C_KVW)]
        + [pl.BlockSpec(memory_space=pl.ANY)] * len(prev),
        out_specs=(pl.BlockSpec((out_rows, HEAD_DIM), lambda i: (layer * nblk + i, 0)),
                   pl.BlockSpec((out_rows, HEAD_DIM), lambda i: (layer * nblk + i, 0)),
                   pl.BlockSpec((out_rows, HEAD_DIM), lambda i: (layer * batch + i // per_b, 0)),
                   pl.BlockSpec((ROWS_BLK, KVW), lambda i: (i, 0)),
                   pl.BlockSpec((ROWS_BLK, KVW), lambda i: (i, 0))),
        input_output_aliases={3 + a: a for a in range(len(prev))},
        compiler_params=_params(("arbitrary",)),
        name="cache_rows",
    )(proj, proj, proj, *prev)


_BNN = (((2,), (1,)), ((0,), (0,)))
_BNT = (((2,), (2,)), ((0,), (0,)))
_BTN = (((1,), (1,)), ((0,), (0,)))


def _bdot_b(a, b, dims=_BNN):
    return lax.dot_general(a.astype(BF16), b.astype(BF16), dims, preferred_element_type=F32)


def _tri_masks():
    r = lax.broadcasted_iota(I32, (CHUNK, CHUNK), 0)
    c = lax.broadcasted_iota(I32, (CHUNK, CHUNK), 1)
    return r, c


def _gla_prompt_kernel(q_ref, k_ref, v_ref, gg_ref, sm_ref, w2_ref, b2_ref, nw_ref, _y_ref, o_ref, so_ref, s_ref):
    t = pl.program_id(1)

    @pl.when(t == 0)
    def _():
        s_ref[...] = jnp.zeros_like(s_ref)

    r, c = _tri_masks()
    lower = r >= c
    tril = jnp.where(lower, 1.0, 0.0).astype(F32)
    dk, dv = GLA_DK, GLA_DV
    nch = q_ref.shape[0] // CHUNK
    pairs = [(ch, h) for ch in range(nch) for h in range(GLA_HEADS)]
    rs = lambda ch: slice(ch * CHUNK, (ch + 1) * CHUNK)
    stack = lambda fn: jnp.stack([fn(ch, h) for ch, h in pairs], axis=0)

    log_a = _log_sigmoid(_bdot(sm_ref[:, SM_GLR:SM_GLR + GLA_GATE_RANK], w2_ref[...]) + b2_ref[...]) / GLA_GATE_NORM
    width = GLA_HEADS * dk
    b_all = _hdot(tril, jnp.concatenate([log_a[rs(ch)] for ch in range(nch)], axis=1))
    b_all_t = b_all.T
    b = stack(lambda ch, h: b_all[:, ch * width + h * dk:ch * width + (h + 1) * dk])
    b_last = b[:, CHUNK - 1:CHUNK, :]
    b_last_col = stack(lambda ch, h: b_all_t[ch * width + h * dk:ch * width + (h + 1) * dk, CHUNK - 1:CHUNK])
    q = stack(lambda ch, h: q_ref[rs(ch), h * dk:(h + 1) * dk]) * (dk ** -0.5)
    k = stack(lambda ch, h: k_ref[rs(ch), h * dk:(h + 1) * dk])
    v = stack(lambda ch, h: v_ref[rs(ch), h * dv:(h + 1) * dv])
    qe = q * jnp.exp(b)
    att = jnp.where(lower[None], _bdot_b(qe, k * jnp.exp(-b), _BNT), 0.0)
    o_intra = _bdot_b(att, v)
    kv = _bdot_b(k * jnp.exp(b_last - b), v, _BTN)
    s_decay = jnp.exp(b_last_col)

    for p, (ch, h) in enumerate(pairs):
        s = s_ref[h]
        o = _bdot(qe[p], s) + o_intra[p]
        s_ref[h] = s_decay[p] * s + kv[p]
        o_ref[rs(ch), h * dv:(h + 1) * dv] = _rms(o, nw_ref[...]) * _silu(gg_ref[rs(ch), h * dv:(h + 1) * dv])
    so_ref[0] = s_ref[...]


def _gla_prompt(proj, w2, b2, nw, batch, seq, y):
    nt = seq // TB
    row = lambda w, col: pl.BlockSpec((TB, w), lambda b, t: (b * nt + t, col // w))
    full = lambda shape: pl.BlockSpec(shape, lambda b, t: (0,) * len(shape))
    return pl.pallas_call(
        _gla_prompt_kernel,
        out_shape=(jax.ShapeDtypeStruct(y.shape, F32),
                   jax.ShapeDtypeStruct((batch, GLA_HEADS, GLA_DK, GLA_DV), F32)),
        grid=(batch, nt),
        in_specs=[row(GLA_HEADS * GLA_DK, C_GQ), row(GLA_HEADS * GLA_DK, C_GK), row(GLA_WIDTH, C_GV),
                  row(GLA_WIDTH, C_GG), row(128, C_SM), full(w2.shape), full(b2.shape), full(nw.shape),
                  pl.BlockSpec(memory_space=pl.ANY)],
        out_specs=(pl.BlockSpec((TB, GLA_WIDTH), lambda b, t: (b * nt + t, NSA_WIDTH // GLA_WIDTH)),
                   pl.BlockSpec((1, GLA_HEADS, GLA_DK, GLA_DV), lambda b, t: (b, 0, 0, 0))),
        scratch_shapes=[pltpu.VMEM((GLA_HEADS, GLA_DK, GLA_DV), F32)],
        input_output_aliases={8: 0},
        compiler_params=_params(("parallel", "arbitrary")),
        name="gla_prompt",
    )(proj, proj, proj, proj, proj, w2, b2, nw, y)


def _unit_lower_inverse(m, r, c):
    eye = jnp.where(r == c, 1.0, 0.0).astype(F32)[None]
    base = 8
    m8 = jnp.where(((r // base) == (c // base))[None], m, 0.0)
    m2 = _bdot_b(m8, m8)
    m4 = _bdot_b(m2, m2)
    t = _bdot_b(_bdot_b(eye - m8, eye + m2), eye + m4)
    s = base
    while s < CHUNK:
        off = ((r // (2 * s)) == (c // (2 * s))) & ((r // s) != (c // s))
        t = t - _bdot_b(t, _bdot_b(jnp.where(off[None], m, 0.0), t))
        s *= 2
    return t


def _gdn_prompt_kernel(x_ref, dz_ref, sm_ref, cw_ref, al_ref, dt_ref, nw_ref, _y_ref, o_ref, so_ref, s_ref,
                       tail_ref):
    t = pl.program_id(1)

    @pl.when(t == 0)
    def _():
        s_ref[...] = jnp.zeros_like(s_ref)
        tail_ref[...] = jnp.zeros_like(tail_ref)

    tb = x_ref.shape[0]
    x = x_ref[...]
    xc = jnp.concatenate([tail_ref[...], x], axis=0)
    off = 8 - (CONV_W - 1)
    y = xc[off:off + tb] * cw_ref[0:1, :]
    for jw in range(1, CONV_W):
        y = y + xc[off + jw:off + jw + tb] * cw_ref[jw:jw + 1, :]
    y = _silu(y)
    tail_ref[...] = x[tb - 8:tb]

    beta_all = _sigmoid(sm_ref[:, SM_DB:SM_DB + GDN_HEADS])
    g_all = -jnp.exp(al_ref[...]) * _softplus(sm_ref[:, SM_DA:SM_DA + GDN_HEADS] + dt_ref[...])
    r, c = _tri_masks()
    lower = r >= c
    strict = r > c
    tril = jnp.where(lower, 1.0, 0.0).astype(F32)
    dk, dv = GDN_DK, GDN_DV
    nch = tb // CHUNK
    pairs = [(ch, h) for ch in range(nch) for h in range(GDN_HEADS)]
    rs = lambda ch: slice(ch * CHUNK, (ch + 1) * CHUNK)
    stack = lambda fn: jnp.stack([fn(ch, h) for ch, h in pairs], axis=0)

    cq = stack(lambda ch, h: y[rs(ch), h * dk:(h + 1) * dk])
    ck = stack(lambda ch, h: y[rs(ch), GDN_WIDTH + h * dk:GDN_WIDTH + (h + 1) * dk])
    v = stack(lambda ch, h: y[rs(ch), 2 * GDN_WIDTH + h * dv:2 * GDN_WIDTH + (h + 1) * dv])
    q = cq * lax.rsqrt(jnp.sum(cq * cq, axis=-1, keepdims=True) + 1e-6) * (dk ** -0.5)
    k = ck * lax.rsqrt(jnp.sum(ck * ck, axis=-1, keepdims=True) + 1e-6)
    beta = stack(lambda ch, h: jnp.broadcast_to(beta_all[rs(ch), h:h + 1], (CHUNK, dk)))
    g_cols = jnp.concatenate([g_all[rs(ch)] for ch in range(nch)], axis=1)
    gam_cols = _hdot(tril, g_cols)
    gam_rows = gam_cols.T
    gam = jnp.stack([jnp.broadcast_to(gam_cols[:, p:p + 1], (CHUNK, dk)) for p in range(len(pairs))], axis=0)
    decay = jnp.exp(jnp.where(lower[None], gam[:, :, 0:CHUNK] - gam_rows[:, None, :], NEG))
    kb = k * beta
    m = jnp.where(strict[None], _bdot_b(kb, k, _BNT) * decay, 0.0)
    tinv = _unit_lower_inverse(m, r, c)
    eg = jnp.exp(gam)
    u = _bdot_b(tinv, v * beta)
    w = _bdot_b(tinv, kb * eg)
    att = _bdot_b(q, k, _BNT) * decay
    qe = q * eg
    g_last = gam[:, CHUNK - 1:CHUNK, :]
    kd = k * jnp.exp(g_last - gam)
    eg_last = jnp.exp(g_last)

    for p, (ch, h) in enumerate(pairs):
        s = s_ref[h]
        v_new = u[p] - _bdot(w[p], s)
        o = _bdot(qe[p], s) + _bdot(att[p], v_new)
        s_ref[h] = eg_last[p] * s + _bdot(kd[p], v_new, _TN)
        o_ref[rs(ch), h * dv:(h + 1) * dv] = _rms(o, nw_ref[...]) * _silu(dz_ref[rs(ch), h * dv:(h + 1) * dv])
    so_ref[0] = s_ref[...]


def _gdn_prompt(proj, cw, a_log, dt_bias, nw, batch, seq, y):
    nt = seq // TB_GDN
    row = lambda w, col: pl.BlockSpec((TB_GDN, w), lambda b, t: (b * nt + t, col // w))
    full = lambda shape: pl.BlockSpec(shape, lambda b, t: (0,) * len(shape))
    return pl.pallas_call(
        _gdn_prompt_kernel,
        out_shape=(jax.ShapeDtypeStruct(y.shape, F32),
                   jax.ShapeDtypeStruct((batch, GDN_HEADS, GDN_DK, GDN_DV), F32)),
        grid=(batch, nt),
        in_specs=[row(3 * GDN_WIDTH, C_DQKV), row(GDN_WIDTH, C_DZ), row(128, C_SM),
                  full(cw.shape), full(a_log.shape), full(dt_bias.shape), full(nw.shape),
                  pl.BlockSpec(memory_space=pl.ANY)],
        out_specs=(pl.BlockSpec((TB_GDN, GDN_WIDTH), lambda b, t: (b * nt + t, (NSA_WIDTH + GLA_WIDTH) // GDN_WIDTH)),
                   pl.BlockSpec((1, GDN_HEADS, GDN_DK, GDN_DV), lambda b, t: (b, 0, 0, 0))),
        scratch_shapes=[pltpu.VMEM((GDN_HEADS, GDN_DK, GDN_DV), F32), pltpu.VMEM((8, 3 * GDN_WIDTH), F32)],
        input_output_aliases={7: 0},
        compiler_params=_params(("parallel", "arbitrary")),
        name="gdn_prompt",
    )(proj, proj, proj, cw, a_log, dt_bias, nw, y)


def _masked_softmax_rows(s):
    valid = s > -1e29
    p = jnp.where(valid, jnp.exp(s - jnp.max(s, axis=-1, keepdims=True)), 0.0)
    return p / jnp.maximum(jnp.sum(p, axis=-1, keepdims=True), 1e-30)


def _head_rows(q_ref, sb, lo, hi):
    dh = HEAD_DIM
    q = jnp.concatenate([q_ref[sb, :, h * dh:(h + 1) * dh] for h in range(lo, hi)], axis=0)
    return (q * (dh ** -0.5)).astype(BF16)


def _kv_rows(kv_ref, sb):
    dh = HEAD_DIM
    return jnp.concatenate([kv_ref[sb, :, r * dh:(r + 1) * dh] for r in range(KV_ROWS)], axis=0)


def _value_weights(p):
    return pltpu.roll(p, NSA_KV_HEADS, 1)


SN = 2


def _nsa_sample_a_kernel(n_pages, nb_past, pt_ref, q_ref, kvc_ref, kvw_ref, sm_ref, *rest):
    page_refs = rest[:SN * n_pages]
    win_ref, w4_ref, bc_ref, bw_ref = rest[SN * n_pages:SN * n_pages + 4]
    part_ref, idx_ref, wo_ref, cmp_ref = rest[-4:]
    dh = HEAD_DIM
    blk_rows = CMP_BLOCK * KV_ROWS
    bpp = PAGE_SIZE // CMP_BLOCK
    wl = win_ref.shape[0] // SN
    w4 = w4_ref[...]
    lanes = cmp_ref.shape[1]
    lane = lax.broadcasted_iota(I32, (NSA_KV_HEADS, lanes), 1)
    gi = lax.broadcasted_iota(I32, (NSA_KV_HEADS, lanes), 0)
    n = lane // KV_ROWS
    cur = nb_past
    cand = ((lane % KV_ROWS) == gi) & (n <= cur)
    forced = (n == cur) | (n == cur - 1) | (n == 0)
    li = lax.broadcasted_iota(I32, (NSA_KV_HEADS, N_SELECT), 1)

    q16s, o_cs, scores = [], [], []
    for sb in range(SN):
        for p in range(n_pages):
            x = page_refs[sb * n_pages + p][...]
            sums = []
            for half in range(bpp):
                pr = x[half * blk_rows:(half + 1) * blk_rows] * w4
                s8 = jnp.sum(pr.reshape(blk_rows // 8, 8, dh), axis=0)
                sums.append(s8[0:KV_ROWS] + s8[KV_ROWS:2 * KV_ROWS])
            cmp_ref[sb, p * bpp * KV_ROWS:(p + 1) * bpp * KV_ROWS, :] = jnp.concatenate(sums, axis=0)
        r0 = nb_past * KV_ROWS
        cmp_ref[sb, r0:r0 + KV_ROWS, :] = _kv_rows(kvc_ref, sb) * w4[0:KV_ROWS]
        cmp_ref[sb, r0 + KV_ROWS:, :] = jnp.zeros((lanes - r0 - KV_ROWS, dh), F32)
        cm16 = cmp_ref[sb].astype(BF16)

        q16 = _head_rows(q_ref, sb, 0, NSA_HEADS)
        p = _masked_softmax_rows(_bdot(q16, cm16, _NT) + bc_ref[...])
        q16s.append(q16)
        o_cs.append(_bdot(_value_weights(p), cm16))
        imp = jnp.concatenate([jnp.sum(p[g * NSA_GROUP:(g + 1) * NSA_GROUP], axis=0, keepdims=True)
                               for g in range(NSA_KV_HEADS)], axis=0)
        scores.append(jnp.where(cand, jnp.where(forced, SEL_FORCE, imp), -3e38))

    tops = [jnp.zeros((NSA_KV_HEADS, N_SELECT), I32) for _ in range(SN)]
    for r in range(N_SELECT):
        for sb in range(SN):
            a = jnp.argmax(scores[sb], axis=-1, keepdims=True).astype(I32)
            tops[sb] = jnp.where(li == r, a // KV_ROWS, tops[sb])
            scores[sb] = jnp.where(lane == a, -3e38, scores[sb])

    for sb in range(SN):
        idx_ref[sb] = tops[sb]
        ws = slice(sb * wl, (sb + 1) * wl)
        wo_ref[ws, :] = pltpu.roll(win_ref[ws, :], wl - KV_ROWS, 0)
        wo_ref[(sb + 1) * wl - KV_ROWS:(sb + 1) * wl, :] = _kv_rows(kvw_ref, sb)
        w16 = wo_ref[ws, :].astype(BF16)
        pw = _masked_softmax_rows(_bdot(q16s[sb], w16, _NT) + bw_ref[...])
        o_w = _bdot(_value_weights(pw), w16)

        gates = _sigmoid(sm_ref[sb, :, SM_GATE:SM_GATE + 3 * NSA_HEADS])
        for h in range(NSA_HEADS):
            c0 = 3 * h
            part_ref[sb, :, h * dh:(h + 1) * dh] = (
                o_cs[sb][h:h + 1] * gates[:, c0:c0 + 1] + o_w[h:h + 1] * gates[:, c0 + 2:c0 + 3])


def _nsa_sample_a(page_table, proj_s3, pool_cmp, win_rows, w4, bias_c, bias_w, layer, n_phys, prev_win):
    nbatch, n_pages = page_table.shape
    nb_past = n_pages * PAGE_SIZE // CMP_BLOCK
    wl = bias_w.shape[1]
    page_rows = PAGE_SIZE * KV_ROWS
    nsteps = nbatch // SN
    col = lambda w, c: pl.BlockSpec((SN, 1, w), lambda i, pt: (i, 0, c // w))
    full = lambda shape: pl.BlockSpec(shape, lambda i, pt: (0,) * len(shape))
    page = lambda sb, p: pl.BlockSpec((page_rows, HEAD_DIM), lambda i, pt: (layer * n_phys + pt[SN * i + sb, p], 0))
    gs = pltpu.PrefetchScalarGridSpec(
        num_scalar_prefetch=1, grid=(nsteps,),
        in_specs=[col(NSA_WIDTH, C_NQ), col(KVW, C_KVC), col(KVW, C_KVW), col(128, C_SM)]
        + [page(sb, p) for sb in range(SN) for p in range(n_pages)]
        + [pl.BlockSpec((SN * wl, HEAD_DIM), lambda i, pt: (layer * nsteps + i, 0)),
           full(w4.shape), full(bias_c.shape), full(bias_w.shape)]
        + [pl.BlockSpec(memory_space=pl.ANY)] * len(prev_win),
        out_specs=(pl.BlockSpec((SN, 1, NSA_WIDTH), lambda i, pt: (i, 0, 0)),
                   pl.BlockSpec((SN, NSA_KV_HEADS, N_SELECT), lambda i, pt: (i, 0, 0)),
                   pl.BlockSpec((SN * wl, HEAD_DIM), lambda i, pt: (layer * nsteps + i, 0))),
        scratch_shapes=[pltpu.VMEM((SN, CMP_PAD * KV_ROWS, HEAD_DIM), F32)])
    return pl.pallas_call(
        functools.partial(_nsa_sample_a_kernel, n_pages, nb_past),
        out_shape=(jax.ShapeDtypeStruct((nbatch, 1, NSA_WIDTH), F32),
                   jax.ShapeDtypeStruct((nbatch, NSA_KV_HEADS, N_SELECT), I32),
                   jax.ShapeDtypeStruct(win_rows.shape, F32)),
        grid_spec=gs,
        input_output_aliases={1 + 4 + SN * n_pages + 4 + a: 2 for a in range(len(prev_win))},
        compiler_params=_params(("arbitrary",)),
        name="nsa_sample_a",
    )(page_table, proj_s3, proj_s3, proj_s3, proj_s3, *([pool_cmp] * (SN * n_pages)), win_rows, w4, bias_c, bias_w,
      *prev_win)


def _nsa_sample_b_kernel(nb_past, pt_ref, ix_ref, q_ref, kvs_ref, sm_ref, part_ref, *rest):
    nsel = NSA_KV_HEADS * N_SELECT
    blk_refs = rest[:SN * nsel]
    bs_ref, o_ref = rest[SN * nsel:]
    i0 = pl.program_id(0) * SN
    hg, dh = NSA_GROUP, HEAD_DIM
    blk_rows = CMP_BLOCK * KV_ROWS
    q16 = jnp.concatenate([_head_rows(q_ref, sb, 0, NSA_HEADS) for sb in range(SN)], axis=0)
    masked = jnp.full((hg, blk_rows), NEG, F32)
    keys, bias = [], []
    for sb in range(SN):
        new_blk = jnp.concatenate([_kv_rows(kvs_ref, sb), jnp.zeros((blk_rows - KV_ROWS, dh), F32)], axis=0)
        for g in range(NSA_KV_HEADS):
            owner = sb * NSA_KV_HEADS + g
            for i in range(N_SELECT):
                idx = ix_ref[i0 + sb, g * N_SELECT + i]
                past_blk = blk_refs[sb * nsel + g * N_SELECT + i][...]
                keys.append(jnp.where(idx >= nb_past, new_blk, past_blk).astype(BF16))
                bt = bs_ref[jnp.clip(idx, 0, nb_past), g * hg:(g + 1) * hg, :]
                bias.append(jnp.concatenate([bt if rg == owner else masked
                                             for rg in range(SN * NSA_KV_HEADS)], axis=0))
    k_all = jnp.concatenate(keys, axis=0)
    p = _masked_softmax_rows(_bdot(q16, k_all, _NT) + jnp.concatenate(bias, axis=1))
    o_s = _bdot(_value_weights(p), k_all)
    for sb in range(SN):
        gates = _sigmoid(sm_ref[sb, :, SM_GATE:SM_GATE + 3 * NSA_HEADS])
        for h in range(NSA_HEADS):
            cs = slice(h * dh, (h + 1) * dh)
            r = sb * NSA_HEADS + h
            o_ref[sb, :, cs] = part_ref[sb, :, cs] + o_s[r:r + 1] * gates[:, 3 * h + 1:3 * h + 2]


def _nsa_sample_b(page_table, top_idx, proj_s3, part, pool_sel, bias_s, layer, n_phys):
    nbatch, n_pages = page_table.shape
    nb_past = n_pages * PAGE_SIZE // CMP_BLOCK
    bpp = PAGE_SIZE // CMP_BLOCK
    nsel = NSA_KV_HEADS * N_SELECT
    col = lambda w, c: pl.BlockSpec((SN, 1, w), lambda i, pt, ix: (i, 0, c // w))

    def blk(sb, gi):
        def imap(i, pt, ix):
            b = SN * i + sb
            ip = jnp.clip(ix[b, gi], 0, nb_past - 1)
            return ((layer * n_phys + pt[b, ip // bpp]) * bpp + ip % bpp, 0)
        return pl.BlockSpec((CMP_BLOCK * KV_ROWS, HEAD_DIM), imap)

    gs = pltpu.PrefetchScalarGridSpec(
        num_scalar_prefetch=2, grid=(nbatch // SN,),
        in_specs=[col(NSA_WIDTH, C_NQ), col(KVW, C_KVS), col(128, C_SM),
                  pl.BlockSpec((SN, 1, NSA_WIDTH), lambda i, pt, ix: (i, 0, 0))]
        + [blk(sb, gi) for sb in range(SN) for gi in range(nsel)]
        + [pl.BlockSpec(bias_s.shape, lambda i, pt, ix: (0, 0, 0))],
        out_specs=pl.BlockSpec((SN, 1, NSA_WIDTH), lambda i, pt, ix: (i, 0, 0)))
    return pl.pallas_call(
        functools.partial(_nsa_sample_b_kernel, nb_past),
        out_shape=jax.ShapeDtypeStruct((nbatch, 1, NSA_WIDTH), F32),
        grid_spec=gs,
        compiler_params=_params(("arbitrary",)),
        name="nsa_sample_b",
    )(page_table, top_idx, proj_s3, proj_s3, proj_s3, part, *([pool_sel] * (SN * nsel)), bias_s)


SB = 16


def _rec_sample_kernel(gq_ref, gk_ref, gv_ref, gg_ref, x_ref, dz_ref, sm_ref, sg_ref, sd_ref, cb_ref,
                       w2_ref, b2_ref, gnw_ref, cw_ref, al_ref, dt_ref, dnw_ref, *rest):
    o_ref, sgo_ref, sdo_ref = rest[-3:]
    sm = sm_ref[...]
    ri = lax.broadcasted_iota(I32, (SB, 128), 0)
    log_a = _log_sigmoid(_bdot(sm[:, SM_GLR:SM_GLR + GLA_GATE_RANK], w2_ref[...]) + b2_ref[...]) / GLA_GATE_NORM
    ea_t = jnp.exp(log_a).T
    k_t = gk_ref[...].T
    q_t = (gq_ref[...] * (GLA_DK ** -0.5)).T
    gv = gv_ref[...]
    for h in range(GLA_HEADS):
        hs = slice(h * GLA_DK, (h + 1) * GLA_DK)
        vs = slice(h * GLA_DV, (h + 1) * GLA_DV)
        o_h = jnp.zeros((SB, GLA_DV), F32)
        for i in range(SB):
            s = ea_t[hs, i:i + 1] * sg_ref[i, h] + k_t[hs, i:i + 1] * gv[i:i + 1, vs]
            sgo_ref[i, h] = s
            o = jnp.sum(q_t[hs, i:i + 1] * s, axis=0, keepdims=True)
            o_h = jnp.where(ri == i, o, o_h)
        o_ref[:, vs] = _rms(o_h, gnw_ref[...]) * _silu(gg_ref[:, vs])
    y = x_ref[...] * cw_ref[CONV_W - 1:CONV_W, :]
    for jw in range(CONV_W - 1):
        y = y + cb_ref[jw] * cw_ref[jw:jw + 1, :]
    y = _silu(y)
    beta = _sigmoid(sm[:, SM_DB:SM_DB + GDN_HEADS])
    eg = jnp.exp(-jnp.exp(al_ref[...]) * _softplus(sm[:, SM_DA:SM_DA + GDN_HEADS] + dt_ref[...]))
    dk, dv = GDN_DK, GDN_DV
    for h in range(GDN_HEADS):
        cq = y[:, h * dk:(h + 1) * dk]
        ck = y[:, GDN_WIDTH + h * dk:GDN_WIDTH + (h + 1) * dk]
        v = y[:, 2 * GDN_WIDTH + h * dv:2 * GDN_WIDTH + (h + 1) * dv]
        q_t = (cq * lax.rsqrt(jnp.sum(cq * cq, axis=-1, keepdims=True) + 1e-6) * (dk ** -0.5)).T
        k_t = (ck * lax.rsqrt(jnp.sum(ck * ck, axis=-1, keepdims=True) + 1e-6)).T
        o_h = jnp.zeros((SB, dv), F32)
        for i in range(SB):
            s = eg[i:i + 1, h:h + 1] * sd_ref[i, h]
            kc = k_t[:, i:i + 1]
            delta = (v[i:i + 1] - jnp.sum(kc * s, axis=0, keepdims=True)) * beta[i:i + 1, h:h + 1]
            s = s + kc * delta
            sdo_ref[i, h] = s
            o = jnp.sum(q_t[:, i:i + 1] * s, axis=0, keepdims=True)
            o_h = jnp.where(ri == i, o, o_h)
        vs = slice(GLA_WIDTH + h * dv, GLA_WIDTH + (h + 1) * dv)
        o_ref[:, vs] = _rms(o_h, dnw_ref[...]) * _silu(dz_ref[:, h * dv:(h + 1) * dv])


def _rec_sample(proj_s, state_gla, state_gdn, conv_t, w2, b2, gnw, cw, a_log, dt_bias, dnw, layer, prev_states):
    nbatch = proj_s.shape[0]
    nblk = nbatch // SB
    depth = state_gla.shape[0] // nbatch
    alias_specs = [pl.BlockSpec(memory_space=pl.ANY)] * len(prev_states)
    n_in = 17
    row = lambda w, col: pl.BlockSpec((SB, w), lambda i: (i, col // w))
    full = lambda shape: pl.BlockSpec(shape, lambda i: (0,) * len(shape))
    return pl.pallas_call(
        _rec_sample_kernel,
        out_shape=(jax.ShapeDtypeStruct((nbatch, GLA_WIDTH + GDN_WIDTH), F32),
                   jax.ShapeDtypeStruct(state_gla.shape, F32),
                   jax.ShapeDtypeStruct(state_gdn.shape, F32)),
        grid=(nblk,),
        in_specs=[row(GLA_HEADS * GLA_DK, C_GQ), row(GLA_HEADS * GLA_DK, C_GK), row(GLA_WIDTH, C_GV),
                  row(GLA_WIDTH, C_GG), row(3 * GDN_WIDTH, C_DQKV), row(GDN_WIDTH, C_DZ), row(128, C_SM),
                  pl.BlockSpec((SB, GLA_HEADS, GLA_DK, GLA_DV), lambda i: (layer * nblk + i, 0, 0, 0)),
                  pl.BlockSpec((SB, GDN_HEADS, GDN_DK, GDN_DV), lambda i: (layer * nblk + i, 0, 0, 0)),
                  pl.BlockSpec((CONV_W - 1, SB, 3 * GDN_WIDTH), lambda i: (0, i, 0)),
                  full(w2.shape), full(b2.shape), full(gnw.shape), full(cw.shape), full(a_log.shape),
                  full(dt_bias.shape), full(dnw.shape)] + alias_specs,
        out_specs=(pl.BlockSpec((SB, GLA_WIDTH + GDN_WIDTH), lambda i: (i, 0)),
                   pl.BlockSpec((SB, GLA_HEADS, GLA_DK, GLA_DV), lambda i: (layer * nblk + i, 0, 0, 0)),
                   pl.BlockSpec((SB, GDN_HEADS, GDN_DK, GDN_DV), lambda i: (layer * nblk + i, 0, 0, 0))),
        input_output_aliases={n_in + a: 1 + a for a in range(len(prev_states))},
        compiler_params=_params(("parallel",)),
        name="rec_sample",
    )(proj_s, proj_s, proj_s, proj_s, proj_s, proj_s, proj_s, state_gla, state_gdn, conv_t,
      w2, b2, gnw, cw, a_log, dt_bias, dnw, *prev_states)


def _reorder_w_in(w):
    return jnp.concatenate([
        w[..., 0:2560],
        w[..., 2584:3096],
        w[..., 4136:5672],
        w[..., 3096:3608],
        w[..., 3624:4136],
        w[..., 5672:6184],
        w[..., 2560:2584],
        w[..., 3608:3624],
        w[..., 6184:6192],
        jnp.zeros(w.shape[:-1] + (PROJ_N - 6192,), w.dtype)], axis=-1)


def _cmp_weight_tile(w_cmp):
    half = NSA_KV_HEADS * HEAD_DIM
    return jnp.concatenate([jnp.broadcast_to(w_cmp[:, 0:1], (CMP_BLOCK, half)),
                            jnp.broadcast_to(w_cmp[:, 1:2], (CMP_BLOCK, half))], axis=1).astype(F32)


def _cmp_weight_rows(w_cmp):
    w = jnp.repeat(w_cmp, NSA_KV_HEADS, axis=1).reshape(CMP_BLOCK * KV_ROWS, 1)
    return jnp.broadcast_to(w, (CMP_BLOCK * KV_ROWS, HEAD_DIM)).astype(F32)


def kernel(x_prompt, x_sample, cache_cmp, cache_sel, cache_win, state_gla, state_gdn, state_conv, page_table,
           p_prompt, p_sample, norm_w, ffn_w1, ffn_w3, ffn_w2, w_in, w_out, nsa_w_cmp, t5_bias,
           gla_w_gk2, gla_b_gk, gla_norm_w, gdn_conv_w, gdn_a_log, gdn_dt_bias, gdn_norm_w,
           ple_w_proj, ple_w_gate):
    depth = w_in.shape[0]
    batch, seq, d = x_prompt.shape
    nbatch = x_sample.shape[0]
    n_phys = cache_cmp.shape[1]
    n_pages = page_table.shape[1]
    past = n_pages * PAGE_SIZE
    win_len = cache_win.shape[2]
    rows_p = batch * seq
    kv_shape = (2, NSA_KV_HEADS, HEAD_DIM)

    x = jnp.concatenate([x_prompt.reshape(rows_p, d), x_sample.reshape(nbatch, d)], axis=0)
    assert win_len == WINDOW and past // CMP_BLOCK + 1 >= N_SELECT and past // CMP_BLOCK < CMP_PAD
    assert seq >= WINDOW
    pool_cmp = cache_cmp.reshape(-1, HEAD_DIM)
    pool_sel = cache_sel.reshape(-1, HEAD_DIM)
    win_all = cache_win.reshape(-1, HEAD_DIM)
    sgla_all = state_gla.reshape((depth * nbatch,) + state_gla.shape[2:])
    sgdn_all = state_gdn.reshape((depth * nbatch,) + state_gdn.shape[2:])

    pb_c, pb_s, pb_w = _prompt_bias_tables(t5_bias, seq)
    sb_c, sb_s, sb_w = _sample_bias_tables(t5_bias, past, win_len)
    nb = seq // CMP_BLOCK
    expand = jnp.asarray(-MASK_BIG * np.repeat(np.eye(nb, dtype=np.float32), CMP_BLOCK, axis=1)
                         .reshape(nb, seq // KEY_CHUNK, KEY_CHUNK).transpose(1, 0, 2), BF16)

    w1_all, w3_all, w2_all = ffn_w1.astype(BF16), ffn_w3.astype(BF16), ffn_w2.astype(BF16)
    w_in_all = _reorder_w_in(w_in.astype(BF16))
    w_out_all, wg_all, wp_all = w_out.astype(BF16), ple_w_gate.astype(BF16), ple_w_proj.astype(BF16)

    outs = {k: [] for k in ("cmp_s", "sel_s", "gla_p", "gdn_p", "conv_p", "conv_s")}
    prev_win, prev_states, prev_rows = (), (), ()
    for l in range(depth):
        nw = norm_w[l].reshape(-1, 1, d)
        x = _ffn(x, nw[0], w1_all, w3_all, w2_all, nw[1], l, 0)

        proj = _proj(x, nw[2], w_in_all, l)
        proj_s = proj[rows_p:]
        proj_s3 = proj_s.reshape(nbatch, 1, PROJ_N)
        wtile = _cmp_weight_tile(nsa_w_cmp[l])
        w_gk2 = gla_w_gk2[l]
        b_gk = gla_b_gk[l].reshape(1, -1)
        gla_nw = gla_norm_w[l].reshape(1, -1)
        gdn_nw = gdn_norm_w[l].reshape(1, -1)
        conv_w = gdn_conv_w[l]
        a_log = gdn_a_log[l].reshape(1, -1)
        dt_bias = gdn_dt_bias[l].reshape(1, -1)

        cmpkv = _compress_prompt(proj, wtile, rows_p)
        rows_out = _cache_rows(proj, batch, seq, depth, l, prev_rows)
        prev_rows, (kvs16, kvw16) = rows_out[:3], rows_out[3:]
        y = _nsa_prompt(proj, cmpkv, kvs16, kvw16, pb_c, pb_s, pb_w, expand, batch, seq)
        y, s_gla_p = _gla_prompt(proj, w_gk2, b_gk, gla_nw, batch, seq, y)
        y, s_gdn_p = _gdn_prompt(proj, conv_w, a_log, dt_bias, gdn_nw, batch, seq, y)

        part, top_idx, win_new = _nsa_sample_a(page_table, proj_s3, pool_cmp, win_all, _cmp_weight_rows(nsa_w_cmp[l]),
                                               sb_c, sb_w, l, n_phys, prev_win)
        prev_win = (win_new,)
        o_nsa_s = _nsa_sample_b(page_table, top_idx.reshape(nbatch, NSA_KV_HEADS * N_SELECT), proj_s3, part,
                                pool_sel, sb_s, l, n_phys)
        conv_t = jnp.swapaxes(state_conv[l], 0, 1)
        o_rec_s, s_gla_s, s_gdn_s = _rec_sample(proj_s, sgla_all, sgdn_all, conv_t, w_gk2, b_gk, gla_nw,
                                                conv_w, a_log, dt_bias, gdn_nw, l, prev_states)
        prev_states = (s_gla_s, s_gdn_s)

        y = lax.dynamic_update_slice(
            y, jnp.concatenate([o_nsa_s.reshape(nbatch, NSA_WIDTH), o_rec_s], axis=1), (rows_p, 0))
        x = _outproj(y, x, w_out_all, nw[3], l)
        x = _ffn(x, nw[4], w1_all, w3_all, w2_all, nw[5], l, 1)
        p = jnp.concatenate([p_prompt[l].reshape(rows_p, -1), p_sample[l].reshape(nbatch, -1)], axis=0)
        x = _ple(x, p, nw[6], wg_all, wp_all, nw[7], l, split_rows=rows_p if l == depth - 1 else None)

        outs["cmp_s"].append(proj_s[:, C_KVC:C_KVC + KVW].reshape((nbatch, 1) + kv_shape))
        outs["sel_s"].append(proj_s[:, C_KVS:C_KVS + KVW].reshape((nbatch, 1) + kv_shape))
        outs["gla_p"].append(s_gla_p)
        outs["gdn_p"].append(s_gdn_p)
        outs["conv_p"].append(jnp.stack([proj[(b + 1) * seq - (CONV_W - 1):(b + 1) * seq, C_DQKV:C_DQKV + 3 * GDN_WIDTH]
                                         for b in range(batch)]))
        outs["conv_s"].append(jnp.concatenate(
            [state_conv[l][:, 1:], proj_s[:, None, C_DQKV:C_DQKV + 3 * GDN_WIDTH]], axis=1))

    st = lambda k: jnp.stack(outs[k])
    return (x[0].reshape(batch, seq, d), x[1].reshape(nbatch, 1, d),
            prev_rows[0].reshape((depth, batch, seq) + kv_shape), st("cmp_s"),
            prev_rows[1].reshape((depth, batch, seq) + kv_shape), st("sel_s"),
            prev_rows[2].reshape((depth, batch, WINDOW) + kv_shape), win_new.reshape(cache_win.shape),
            st("gla_p"), s_gla_s.reshape(state_gla.shape), st("gdn_p"), s_gdn_s.reshape(state_gdn.shape),
            st("conv_p"), st("conv_s"))
```
